```python
import jax
import jax.numpy as jnp
from jax import lax
import numpy as np


D_MODEL = 2048
BATCH = 2
SEQ = 4096
DEPTH = 1

GRID_W = 64
CTX_LEN = 256
HEAD_DIM = 128
N_Q_HEADS = 16
N_KV_HEADS = 4
Q_PER_KV = N_Q_HEADS // N_KV_HEADS
ATTN_WIDTH = N_Q_HEADS * HEAD_DIM
KV_WIDTH = N_KV_HEADS * HEAD_DIM
N_FOURIER_GROUPS = 4
FOURIER_GROUP_DIM = 256
FOURIER_WIDTH = N_FOURIER_GROUPS * FOURIER_GROUP_DIM
N_BRANCHES = 2
Q_END = ATTN_WIDTH
K_END = Q_END + KV_WIDTH
V_END = K_END + KV_WIDTH
F_END = V_END + FOURIER_WIDTH
IN_WIDTH = F_END + N_BRANCHES * D_MODEL
D_FF = 4 * D_MODEL
N_MOD = 6
Q_BLOCK = 128
ROPE_THETA = 10000.0
ROPE_AXIS_PAIRS = HEAD_DIM // 4
EPS = 1e-6

kernel_name = 'hybrid_gqa_fnet_adaln_prefix_block'


def rmsnorm(x, g):
    xf = x.astype(jnp.float32)
    xf = xf * lax.rsqrt(jnp.mean(xf * xf, axis=-1, keepdims=True) + EPS)
    return (xf * g.astype(jnp.float32)).astype(x.dtype)


def modulate(h, shift, scale):
    return h * (1 + scale) + shift


def axial_rope_tables(rows):
    inv = 1.0 / (ROPE_THETA ** (jnp.arange(ROPE_AXIS_PAIRS, dtype=jnp.float32) / ROPE_AXIS_PAIRS))
    row_ang = jnp.arange(rows, dtype=jnp.float32)[:, None] * inv
    col_ang = jnp.arange(GRID_W, dtype=jnp.float32)[:, None] * inv
    ang = jnp.concatenate([
        jnp.broadcast_to(row_ang[:, None, :], (rows, GRID_W, ROPE_AXIS_PAIRS)),
        jnp.broadcast_to(col_ang[None, :, :], (rows, GRID_W, ROPE_AXIS_PAIRS))], axis=-1)
    ang = ang.reshape(rows * GRID_W, 2 * ROPE_AXIS_PAIRS)
    return jnp.cos(ang), jnp.sin(ang)


def apply_rope(x, cos, sin):
    xf = x.astype(jnp.float32).reshape(x.shape[:-1] + (HEAD_DIM // 2, 2))
    x1, x2 = xf[..., 0], xf[..., 1]
    c = cos[None, :, None, :]
    s = sin[None, :, None, :]
    out = jnp.stack([x1 * c - x2 * s, x1 * s + x2 * c], axis=-1)
    return out.reshape(x.shape).astype(x.dtype)


def heads(t, n_heads):
    return t.reshape(t.shape[0], t.shape[1], n_heads, HEAD_DIM)


def block_attention(q, k, v):
    b, n = q.shape[0], q.shape[1]
    nblk = n // Q_BLOCK
    qb = q.reshape(b, nblk, Q_BLOCK, N_KV_HEADS, Q_PER_KV, HEAD_DIM).transpose(1, 0, 2, 3, 4, 5)
    scale = HEAD_DIM ** -0.5

    def one_block(q_blk):
        s = jnp.einsum('bqhgd,bkhd->bhgqk', q_blk, k).astype(jnp.float32) * scale
        p = jax.nn.softmax(s, axis=-1).astype(v.dtype)
        return jnp.einsum('bhgqk,bkhd->bqhgd', p, v)

    o = lax.map(one_block, qb)
    return o.transpose(1, 0, 2, 3, 4, 5).reshape(b, n, ATTN_WIDTH)


def fourier_mix(u):
    b, n = u.shape[0], u.shape[1]
    ug = u.astype(jnp.float32).reshape(b, n, N_FOURIER_GROUPS, FOURIER_GROUP_DIM)
    f = jnp.fft.fft2(ug, axes=(1, 3), norm='ortho').real
    return f.reshape(b, n, FOURIER_WIDTH).astype(u.dtype)


def merge_branches(attn_o, four_in, gate_logits, w_attn_o, w_fourier, w_out):
    a = attn_o @ w_attn_o
    f = fourier_mix(four_in) @ w_fourier
    g_a, g_f = jnp.split(gate_logits, N_BRANCHES, axis=-1)
    return (jax.nn.sigmoid(g_a) * a + jax.nn.sigmoid(g_f) * f) @ w_out


def sq_relu_mlp(h, w1, w2):
    return jnp.square(jax.nn.relu(h @ w1)) @ w2


def setup_inputs(seed: int = 0) -> dict:
    key = jax.random.key(seed)
    ks = jax.random.split(key, 17)

    def nrm(k, shape, scale):
        return jax.random.normal(k, shape, dtype=jnp.float32) * scale

    return {
        'x': nrm(ks[0], (BATCH, SEQ, D_MODEL), 1.0),
        'c': nrm(ks[1], (BATCH, D_MODEL), 1.0),
        'ctx': nrm(ks[2], (BATCH, CTX_LEN, D_MODEL), 1.0),
        'c_ctx': nrm(ks[3], (D_MODEL,), 1.0),
        'w_ada': nrm(ks[4], (DEPTH, D_MODEL, N_MOD * D_MODEL), 0.5 * D_MODEL ** -0.5),
        'b_ada': nrm(ks[5], (DEPTH, N_MOD * D_MODEL), 0.02),
        'norm1_g': 1.0 + nrm(ks[6], (DEPTH, D_MODEL), 0.02),
        'w_in': nrm(ks[7], (DEPTH, D_MODEL, IN_WIDTH), D_MODEL ** -0.5),
        'q_norm_g': 1.0 + nrm(ks[8], (DEPTH, HEAD_DIM), 0.02),
        'k_norm_g': 1.0 + nrm(ks[9], (DEPTH, HEAD_DIM), 0.02),
        'w_attn_o': nrm(ks[10], (DEPTH, ATTN_WIDTH, D_MODEL), ATTN_WIDTH ** -0.5),
        'w_fourier': nrm(ks[11], (DEPTH, FOURIER_WIDTH, D_MODEL), FOURIER_WIDTH ** -0.5),
        'w_out': nrm(ks[12], (DEPTH, D_MODEL, D_MODEL), D_MODEL ** -0.5),
        'norm2_g': 1.0 + nrm(ks[13], (DEPTH, D_MODEL), 0.02),
        'w1': nrm(ks[14], (DEPTH, D_MODEL, D_FF), D_MODEL ** -0.5),
        'w2': nrm(ks[15], (DEPTH, D_FF, D_MODEL), D_FF ** -0.5),
        'final_norm_g': 1.0 + nrm(ks[16], (D_MODEL,), 0.02),
    }


def reference(x, c, ctx, c_ctx, w_ada, b_ada, norm1_g, w_in, q_norm_g, k_norm_g,
              w_attn_o, w_fourier, w_out, norm2_g, w1, w2, final_norm_g):
    n_lat = x.shape[1]
    ROWS = n_lat // GRID_W
    cos, sin = axial_rope_tables(ROWS)
    xc = ctx
    for l in range(DEPTH):
        need_ctx_out = l < DEPTH - 1
        w_in_l = w_in[l]
        mod = jax.nn.silu(c) @ w_ada[l] + b_ada[l]
        mod_c = jax.nn.silu(c_ctx) @ w_ada[l] + b_ada[l]
        sh1, sc1, g1, sh2, sc2, g2 = jnp.split(mod[:, None, :], N_MOD, axis=-1)
        csh1, csc1, cg1, csh2, csc2, cg2 = jnp.split(mod_c, N_MOD, axis=-1)

        h = modulate(rmsnorm(x, norm1_g[l]), sh1, sc1)
        hc = modulate(rmsnorm(xc, norm1_g[l]), csh1, csc1)
        p = h @ w_in_l
        q = apply_rope(rmsnorm(heads(p[..., :Q_END], N_Q_HEADS), q_norm_g[l]), cos, sin)
        k = apply_rope(rmsnorm(heads(p[..., Q_END:K_END], N_KV_HEADS), k_norm_g[l]), cos, sin)
        v = heads(p[..., K_END:V_END], N_KV_HEADS)
        kc = rmsnorm(heads(hc @ w_in_l[:, Q_END:K_END], N_KV_HEADS), k_norm_g[l])
        vc = heads(hc @ w_in_l[:, K_END:V_END], N_KV_HEADS)
        k_all = jnp.concatenate([k, kc], axis=1)
        v_all = jnp.concatenate([v, vc], axis=1)
        o = block_attention(q, k_all, v_all)
        y = merge_branches(o, p[..., V_END:F_END], p[..., F_END:], w_attn_o[l], w_fourier[l], w_out[l])

        if need_ctx_out:
            qc = rmsnorm(heads(hc @ w_in_l[:, :Q_END], N_Q_HEADS), q_norm_g[l])
            pcr = hc @ w_in_l[:, V_END:]
            oc = block_attention(qc, kc, vc)
            yc = merge_branches(oc, pcr[..., :FOURIER_WIDTH], pcr[..., FOURIER_WIDTH:],
                                w_attn_o[l], w_fourier[l], w_out[l])
            xc = xc + cg1 * yc
            hc2 = modulate(rmsnorm(xc, norm2_g[l]), csh2, csc2)
            xc = xc + cg2 * sq_relu_mlp(hc2, w1[l], w2[l])

        x = x + g1 * y
        h2 = modulate(rmsnorm(x, norm2_g[l]), sh2, sc2)
        x = x + g2 * sq_relu_mlp(h2, w1[l], w2[l])
    return rmsnorm(x, final_norm_g)
```

```python
import functools
import math

import jax
import jax.numpy as jnp
from jax import lax
from jax.experimental import pallas as pl
from jax.experimental.pallas import tpu as pltpu

D_MODEL = 2048
GRID_W = 64
HEAD_DIM = 128
N_Q_HEADS = 16
N_KV_HEADS = 4
Q_PER_KV = N_Q_HEADS // N_KV_HEADS
ATTN_WIDTH = N_Q_HEADS * HEAD_DIM
KV_WIDTH = N_KV_HEADS * HEAD_DIM
N_FOURIER_GROUPS = 4
FOURIER_GROUP_DIM = 256
FOURIER_WIDTH = N_FOURIER_GROUPS * FOURIER_GROUP_DIM
Q_END = ATTN_WIDTH
K_END = Q_END + KV_WIDTH
V_END = K_END + KV_WIDTH
F_END = V_END + FOURIER_WIDTH
IN_WIDTH = F_END + 2 * D_MODEL
D_FF = 4 * D_MODEL
N_MOD = 6
ROPE_THETA = 10000.0
ROPE_AXIS_PAIRS = HEAD_DIM // 4
EPS = 1e-6

LANES = 128
SUBLANES = 8
VMEM_LIMIT = 56 * 1024 * 1024

BF16 = jnp.bfloat16
F32 = jnp.float32


def _cparams(sem):
    return pltpu.CompilerParams(dimension_semantics=sem, vmem_limit_bytes=VMEM_LIMIT)


def _dot(a, b):
    return jnp.dot(a, b, preferred_element_type=F32)


def _rmsnorm_rows(x, g):
    ms = jnp.mean(x * x, axis=-1, keepdims=True)
    return x * lax.rsqrt(ms + EPS) * g


ADA_TN = 512


def _adaln_kernel(cb_ref, w_ref, b_ref, out_ref):
    n_rows = cb_ref.shape[0]
    acts = []
    for r in range(n_rows):
        cr = cb_ref[r]
        acts.append(cr * jax.nn.sigmoid(cr))
    out_ref[...] = jnp.zeros(out_ref.shape, F32)
    for t in range(ADA_TN // LANES):
        wt = w_ref[:, t * LANES:(t + 1) * LANES]
        for r in range(n_rows):
            prod = wt * acts[r]
            red = prod.reshape(D_MODEL // SUBLANES, SUBLANES, LANES).sum(axis=0)
            tot = red.sum(axis=0, keepdims=True) + b_ref[:, t * LANES:(t + 1) * LANES]
            out_ref[r:r + 1, t * LANES:(t + 1) * LANES] = tot


def _adaln(cb, w_ada, b_ada):
    n_out = w_ada.shape[1]
    return pl.pallas_call(
        _adaln_kernel,
        grid=(n_out // ADA_TN,),
        in_specs=[
            pl.BlockSpec(cb.shape, lambda j: (0, 0, 0)),
            pl.BlockSpec((D_MODEL, ADA_TN), lambda j: (0, j)),
            pl.BlockSpec((1, ADA_TN), lambda j: (0, j)),
        ],
        out_specs=pl.BlockSpec((SUBLANES, ADA_TN), lambda j: (0, j)),
        out_shape=jax.ShapeDtypeStruct((SUBLANES, n_out), F32),
        compiler_params=_cparams(("arbitrary",)),
        name="adaln",
    )(cb, w_ada, b_ada)


INP_TM = 1024
INP_TN = 512
Q_SCALE = HEAD_DIM ** -0.5 * math.log2(math.e)


def _head_norm(xh, g):
    ms = jnp.mean(xh * xh, axis=-1, keepdims=True)
    return xh * lax.rsqrt(ms + EPS) * g


def _rope(xn, rope_ref):
    nxt = pltpu.roll(xn, LANES - 1, 1)
    prv = pltpu.roll(xn, 1, 1)
    return xn * rope_ref[0] + nxt * rope_ref[1] + prv * rope_ref[2]


def _inproj_kernel(x_ref, mod_ref, n1g_ref, w_ref, rope_ref, qg_ref, kg_ref, out_ref, h_ref,
                   *, col_block0, use_rope):
    j = pl.program_id(1)

    @pl.when(j == 0)
    def _():
        xn = _rmsnorm_rows(x_ref[...], n1g_ref[...])
        h = xn * (1.0 + mod_ref[0, 1:2, :]) + mod_ref[0, 0:1, :]
        h_ref[...] = h.astype(BF16)

    acc = _dot(h_ref[...], w_ref[...])
    col = (j + col_block0) * INP_TN

    def heads_epilogue(g_ref, mult):
        for hh in range(INP_TN // HEAD_DIM):
            sl = slice(hh * HEAD_DIM, (hh + 1) * HEAD_DIM)
            xn = _head_norm(acc[:, sl], g_ref[...])
            if use_rope:
                xn = _rope(xn, rope_ref)
            out_ref[:, sl] = (xn * mult).astype(BF16)

    @pl.when(col < Q_END)
    def _():
        heads_epilogue(qg_ref, Q_SCALE)

    @pl.when(jnp.logical_and(col >= Q_END, col < K_END))
    def _():
        heads_epilogue(kg_ref, 1.0)

    @pl.when(jnp.logical_and(col >= K_END, col < F_END))
    def _():
        out_ref[...] = acc.astype(BF16)

    @pl.when(col >= F_END)
    def _():
        out_ref[...] = jax.nn.sigmoid(acc).astype(BF16)


def _inproj(x2, mod3, mod_row_of_tile, n1g, w_bf, rope, qg, kg, *, tm, col_block0, n_col_blocks,
            use_rope, rope_tiles, name):
    rows = x2.shape[0]
    kern = functools.partial(_inproj_kernel, col_block0=col_block0, use_rope=use_rope)
    return pl.pallas_call(
        kern,
        grid=(rows // tm, n_col_blocks),
        in_specs=[
            pl.BlockSpec((tm, D_MODEL), lambda i, j: (i, 0)),
            pl.BlockSpec((1, N_MOD, D_MODEL), lambda i, j: (mod_row_of_tile(i), 0, 0)),
            pl.BlockSpec((1, D_MODEL), lambda i, j: (0, 0)),
            pl.BlockSpec((D_MODEL, INP_TN), lambda i, j: (0, j + col_block0)),
            pl.BlockSpec((3, tm, HEAD_DIM), lambda i, j: (0, i % rope_tiles, 0)),
            pl.BlockSpec((1, HEAD_DIM), lambda i, j: (0, 0)),
            pl.BlockSpec((1, HEAD_DIM), lambda i, j: (0, 0)),
        ],
        out_specs=pl.BlockSpec((tm, INP_TN), lambda i, j: (i, j)),
        out_shape=jax.ShapeDtypeStruct((rows, n_col_blocks * INP_TN), BF16),
        scratch_shapes=[pltpu.VMEM((tm, D_MODEL), BF16)],
        compiler_params=_cparams(("arbitrary", "arbitrary")),
        name=name,
    )(x2, mod3, n1g, w_bf, rope, qg, kg)


ATT_TQ = 256


def _attn_kernel(q_ref, k_ref, v_ref, kc_ref, vc_ref, o_ref, kall_ref, vt_ref):
    qi = pl.program_id(2)
    n_lat = k_ref.shape[0]

    @pl.when(qi == 0)
    def _():
        kall_ref[0:n_lat, :] = k_ref[...]
        kall_ref[n_lat:, :] = kc_ref[...]
        vt_ref[:, 0:n_lat] = v_ref[...].astype(F32).T.astype(BF16)
        vt_ref[:, n_lat:] = vc_ref[...].astype(F32).T.astype(BF16)

    k_all = kall_ref[...]
    v_t = vt_ref[...]
    for g in range(Q_PER_KV):
        sl = slice(g * HEAD_DIM, (g + 1) * HEAD_DIM)
        qg = q_ref[:, sl]
        s_t = lax.dot_general(k_all, qg, (((1,), (1,)), ((), ())), preferred_element_type=F32)
        m = jnp.max(s_t, axis=0, keepdims=True)
        p_t = jnp.exp2(s_t - m)
        l = jnp.sum(p_t, axis=0, keepdims=True)
        o_t = _dot(v_t, p_t.astype(BF16)) * (1.0 / l)
        o_ref[:, sl] = o_t.T.astype(BF16)


def _attention(p, pc, batch, n_lat, n_ctx):
    q_tiles = n_lat // ATT_TQ
    kcol = Q_END // HEAD_DIM
    vcol = K_END // HEAD_DIM
    n_keys = n_lat + n_ctx
    return pl.pallas_call(
        _attn_kernel,
        grid=(batch, N_KV_HEADS, q_tiles),
        in_specs=[
            pl.BlockSpec((ATT_TQ, Q_PER_KV * HEAD_DIM), lambda b, h, qi: (b * q_tiles + qi, h)),
            pl.BlockSpec((n_lat, HEAD_DIM), lambda b, h, qi: (b, kcol + h)),
            pl.BlockSpec((n_lat, HEAD_DIM), lambda b, h, qi: (b, vcol + h)),
            pl.BlockSpec((n_ctx, HEAD_DIM), lambda b, h, qi: (b, h)),
            pl.BlockSpec((n_ctx, HEAD_DIM), lambda b, h, qi: (b, N_KV_HEADS + h)),
        ],
        out_specs=pl.BlockSpec((ATT_TQ, Q_PER_KV * HEAD_DIM), lambda b, h, qi: (b * q_tiles + qi, h)),
        out_shape=jax.ShapeDtypeStruct((batch * n_lat, ATTN_WIDTH), BF16),
        scratch_shapes=[pltpu.VMEM((n_keys, HEAD_DIM), BF16), pltpu.VMEM((HEAD_DIM, n_keys), BF16)],
        compiler_params=_cparams(("arbitrary", "arbitrary", "arbitrary")),
        name="attn",
    )(p, p, p, pc, pc)


FOU_TM = 512
FOU_TK = 1024


def _fourier_kernel(cs_ref, u_ref, chan_ref, f_ref, acc_ref):
    kk = pl.program_id(2)

    @pl.when(kk == 0)
    def _():
        acc_ref[...] = jnp.zeros(acc_ref.shape, F32)

    lhs = cs_ref[...].reshape(2 * FOU_TM, FOU_TK)
    acc_ref[...] += _dot(lhs, u_ref[...])

    @pl.when(kk == pl.num_programs(2) - 1)
    def _():
        for g in range(N_FOURIER_GROUPS):
            sl = slice(g * FOURIER_GROUP_DIM, (g + 1) * FOURIER_GROUP_DIM)
            gc = acc_ref[0:FOU_TM, sl].astype(BF16)
            gs = acc_ref[FOU_TM:, sl].astype(BF16)
            f_ref[:, sl] = (_dot(gc, chan_ref[0]) - _dot(gs, chan_ref[1])).astype(BF16)


def _fourier(cs_pos, p, cs_chan, batch, n_lat):
    ucol = V_END // FOURIER_WIDTH
    m_tiles = n_lat // FOU_TM
    k_tiles = n_lat // FOU_TK
    return pl.pallas_call(
        _fourier_kernel,
        grid=(batch, m_tiles, k_tiles),
        in_specs=[
            pl.BlockSpec((2, FOU_TM, FOU_TK), lambda b, i, kk: (0, i, kk)),
            pl.BlockSpec((FOU_TK, FOURIER_WIDTH), lambda b, i, kk: (b * k_tiles + kk, ucol)),
            pl.BlockSpec(cs_chan.shape, lambda b, i, kk: (0, 0, 0)),
        ],
        out_specs=pl.BlockSpec((FOU_TM, FOURIER_WIDTH), lambda b, i, kk: (b * m_tiles + i, 0)),
        out_shape=jax.ShapeDtypeStruct((batch * n_lat, FOURIER_WIDTH), BF16),
        scratch_shapes=[pltpu.VMEM((2 * FOU_TM, FOURIER_WIDTH), F32)],
        compiler_params=_cparams(("arbitrary", "arbitrary", "arbitrary")),
        name="fourier",
    )(cs_pos, p, cs_chan)


MRG_TM = 256


def _merge_kernel(o_ref, f_ref, gates_ref, x_ref, mod_ref, wa_ref, wf_ref, wo_ref, x1_ref):
    a = _dot(o_ref[...], wa_ref[...])
    fm = _dot(f_ref[...], wf_ref[...])
    z = gates_ref[:, 0:D_MODEL].astype(F32) * a + gates_ref[:, D_MODEL:].astype(F32) * fm
    y = _dot(z.astype(BF16), wo_ref[...])
    x1_ref[...] = x_ref[...] + mod_ref[0, 2:3, :] * y


def _merge(o, f, p, x2, mod3, wa, wf, wo, n_lat):
    rows = x2.shape[0]
    tiles_per_batch = n_lat // MRG_TM
    const = lambda i: (0, 0)
    single = pl.Buffered(1)
    return pl.pallas_call(
        _merge_kernel,
        grid=(rows // MRG_TM,),
        in_specs=[
            pl.BlockSpec((MRG_TM, ATTN_WIDTH), lambda i: (i, 0)),
            pl.BlockSpec((MRG_TM, FOURIER_WIDTH), lambda i: (i, 0)),
            pl.BlockSpec((MRG_TM, 2 * D_MODEL), lambda i: (i, F_END // (2 * D_MODEL))),
            pl.BlockSpec((MRG_TM, D_MODEL), lambda i: (i, 0)),
            pl.BlockSpec((1, N_MOD, D_MODEL), lambda i: (i // tiles_per_batch, 0, 0)),
            pl.BlockSpec(wa.shape, const, pipeline_mode=single),
            pl.BlockSpec(wf.shape, const, pipeline_mode=single),
            pl.BlockSpec(wo.shape, const, pipeline_mode=single),
        ],
        out_specs=pl.BlockSpec((MRG_TM, D_MODEL), lambda i: (i, 0)),
        out_shape=jax.ShapeDtypeStruct((rows, D_MODEL), F32),
        compiler_params=_cparams(("arbitrary",)),
        name="merge",
    )(o, f, p, x2, mod3, wa, wf, wo)


MLP_TM = 512
MLP_TF = 1024


def _mlp_kernel(x1_ref, mod_ref, n2g_ref, w1_ref, w2_ref, fg_ref, out_ref, h_ref, acc_ref):
    kf = pl.program_id(1)

    @pl.when(kf == 0)
    def _():
        xn = _rmsnorm_rows(x1_ref[...], n2g_ref[...])
        h = xn * (1.0 + mod_ref[0, 4:5, :]) + mod_ref[0, 3:4, :]
        h_ref[...] = h.astype(BF16)
        acc_ref[...] = jnp.zeros(acc_ref.shape, F32)

    hid = jnp.maximum(_dot(h_ref[...], w1_ref[...]), 0.0)
    acc_ref[...] += _dot((hid * hid).astype(BF16), w2_ref[...])

    @pl.when(kf == pl.num_programs(1) - 1)
    def _():
        x2 = x1_ref[...] + mod_ref[0, 5:6, :] * acc_ref[...]
        out_ref[...] = _rmsnorm_rows(x2, fg_ref[...])


def _mlp(x1, mod3, n2g, w1, w2, fg, n_lat):
    rows = x1.shape[0]
    tiles_per_batch = n_lat // MLP_TM
    return pl.pallas_call(
        _mlp_kernel,
        grid=(rows // MLP_TM, D_FF // MLP_TF),
        in_specs=[
            pl.BlockSpec((MLP_TM, D_MODEL), lambda i, kf: (i, 0)),
            pl.BlockSpec((1, N_MOD, D_MODEL), lambda i, kf: (i // tiles_per_batch, 0, 0)),
            pl.BlockSpec((1, D_MODEL), lambda i, kf: (0, 0)),
            pl.BlockSpec((D_MODEL, MLP_TF), lambda i, kf: (0, kf)),
            pl.BlockSpec((MLP_TF, D_MODEL), lambda i, kf: (kf, 0)),
            pl.BlockSpec((1, D_MODEL), lambda i, kf: (0, 0)),
        ],
        out_specs=pl.BlockSpec((MLP_TM, D_MODEL), lambda i, kf: (i, 0)),
        out_shape=jax.ShapeDtypeStruct((rows, D_MODEL), F32),
        scratch_shapes=[pltpu.VMEM((MLP_TM, D_MODEL), BF16), pltpu.VMEM((MLP_TM, D_MODEL), F32)],
        compiler_params=_cparams(("arbitrary", "arbitrary")),
        name="mlp",
    )(x1, mod3, n2g, w1, w2, fg)


def _rope_tables(rows):
    inv = 1.0 / (ROPE_THETA ** (jnp.arange(ROPE_AXIS_PAIRS, dtype=F32) / ROPE_AXIS_PAIRS))
    row_ang = jnp.arange(rows, dtype=F32)[:, None] * inv
    col_ang = jnp.arange(GRID_W, dtype=F32)[:, None] * inv
    ang = jnp.concatenate([
        jnp.broadcast_to(row_ang[:, None, :], (rows, GRID_W, ROPE_AXIS_PAIRS)),
        jnp.broadcast_to(col_ang[None, :, :], (rows, GRID_W, ROPE_AXIS_PAIRS))], axis=-1)
    ang = ang.reshape(rows * GRID_W, 2 * ROPE_AXIS_PAIRS)
    cos, sin = jnp.cos(ang), jnp.sin(ang)
    zero = jnp.zeros_like(sin)
    cos_e = jnp.stack([cos, cos], axis=-1).reshape(-1, HEAD_DIM)
    sin_nxt = jnp.stack([-sin, zero], axis=-1).reshape(-1, HEAD_DIM)
    sin_prv = jnp.stack([zero, sin], axis=-1).reshape(-1, HEAD_DIM)
    return jnp.stack([cos_e, sin_nxt, sin_prv], axis=0)


def _dft_tables(n, scale):
    idx = jnp.arange(n, dtype=jnp.int32)
    ang = ((idx[:, None] * idx[None, :]) % n).astype(F32) * (2.0 * math.pi / n)
    return jnp.stack([jnp.cos(ang) * scale, jnp.sin(ang) * scale], axis=0).astype(BF16)


def kernel(x, c, ctx, c_ctx, w_ada, b_ada, norm1_g, w_in, q_norm_g, k_norm_g, w_attn_o, w_fourier,
           w_out, norm2_g, w1, w2, final_norm_g):
    batch, n_lat, _ = x.shape
    n_ctx = ctx.shape[1]
    assert w_ada.shape[0] == 1, "single-layer block"
    assert n_lat % INP_TM == 0 and n_ctx % SUBLANES == 0

    cc = jnp.concatenate([c, c_ctx[None, :]], axis=0)
    cb = jnp.broadcast_to(cc[:, :, None], (batch + 1, D_MODEL, LANES))
    mod = _adaln(cb, w_ada[0], b_ada)
    mod3 = mod.reshape(SUBLANES, N_MOD, D_MODEL)

    w_in_bf = w_in[0].astype(BF16)
    n1g = norm1_g
    qg, kg = q_norm_g, k_norm_g
    rope = _rope_tables(n_lat // GRID_W)

    x2 = x.reshape(batch * n_lat, D_MODEL)
    tiles_per_batch = n_lat // INP_TM
    p = _inproj(x2, mod3, lambda i: i // tiles_per_batch, n1g, w_in_bf, rope, qg, kg,
                tm=INP_TM, col_block0=0, n_col_blocks=IN_WIDTH // INP_TN, use_rope=True,
                rope_tiles=tiles_per_batch, name="inproj")
    ctx2 = ctx.reshape(batch * n_ctx, D_MODEL)
    pc = _inproj(ctx2, mod3, lambda i: batch, n1g, w_in_bf, rope, qg, kg,
                 tm=batch * n_ctx, col_block0=Q_END // INP_TN, n_col_blocks=(V_END - Q_END) // INP_TN,
                 use_rope=False, rope_tiles=1, name="ctxproj")

    o = _attention(p, pc, batch, n_lat, n_ctx)

    cs_pos = _dft_tables(n_lat, n_lat ** -0.5)
    cs_chan = _dft_tables(FOURIER_GROUP_DIM, FOURIER_GROUP_DIM ** -0.5)
    f = _fourier(cs_pos, p, cs_chan, batch, n_lat)

    x1 = _merge(o, f, p, x2, mod3, w_attn_o[0].astype(BF16), w_fourier[0].astype(BF16),
                w_out[0].astype(BF16), n_lat)
    out = _mlp(x1, mod3, norm2_g, w1[0].astype(BF16), w2[0].astype(BF16),
               final_norm_g[None, :], n_lat)
    return out.reshape(batch, n_lat, D_MODEL)
```

```python
import functools
import math

import jax
import jax.numpy as jnp
from jax import lax
from jax.experimental import pallas as pl
from jax.experimental.pallas import tpu as pltpu

D_MODEL = 2048
GRID_W = 64
HEAD_DIM = 128
N_Q_HEADS = 16
N_KV_HEADS = 4
Q_PER_KV = N_Q_HEADS // N_KV_HEADS
ATTN_WIDTH = N_Q_HEADS * HEAD_DIM
KV_WIDTH = N_KV_HEADS * HEAD_DIM
N_FOURIER_GROUPS = 4
FOURIER_GROUP_DIM = 256
FOURIER_WIDTH = N_FOURIER_GROUPS * FOURIER_GROUP_DIM
Q_END = ATTN_WIDTH
K_END = Q_END + KV_WIDTH
V_END = K_END + KV_WIDTH
F_END = V_END + FOURIER_WIDTH
IN_WIDTH = F_END + 2 * D_MODEL
D_FF = 4 * D_MODEL
N_MOD = 6
ROPE_THETA = 10000.0
ROPE_AXIS_PAIRS = HEAD_DIM // 4
EPS = 1e-6

LANES = 128
SUBLANES = 8
VMEM_LIMIT = 56 * 1024 * 1024

BF16 = jnp.bfloat16
F32 = jnp.float32


def _cparams(sem):
    return pltpu.CompilerParams(dimension_semantics=sem, vmem_limit_bytes=VMEM_LIMIT)


def _dot(a, b):
    return jnp.dot(a, b, preferred_element_type=F32)


def _rmsnorm_rows(x, g):
    ms = jnp.mean(x * x, axis=-1, keepdims=True)
    return x * lax.rsqrt(ms + EPS) * g


ADA_TN = 512


def _adaln_kernel(cb_ref, w_ref, b_ref, out_ref):
    n_rows = cb_ref.shape[0]
    acts = []
    for r in range(n_rows):
        cr = cb_ref[r]
        acts.append(cr * jax.nn.sigmoid(cr))
    out_ref[...] = jnp.zeros(out_ref.shape, F32)
    for t in range(ADA_TN // LANES):
        wt = w_ref[:, t * LANES:(t + 1) * LANES]
        for r in range(n_rows):
            prod = wt * acts[r]
            red = prod.reshape(D_MODEL // SUBLANES, SUBLANES, LANES).sum(axis=0)
            tot = red.sum(axis=0, keepdims=True) + b_ref[:, t * LANES:(t + 1) * LANES]
            out_ref[r:r + 1, t * LANES:(t + 1) * LANES] = tot


def _adaln(cb, w_ada, b_ada):
    n_out = w_ada.shape[1]
    return pl.pallas_call(
        _adaln_kernel,
        grid=(n_out // ADA_TN,),
        in_specs=[
            pl.BlockSpec(cb.shape, lambda j: (0, 0, 0)),
            pl.BlockSpec((D_MODEL, ADA_TN), lambda j: (0, j)),
            pl.BlockSpec((1, ADA_TN), lambda j: (0, j)),
        ],
        out_specs=pl.BlockSpec((SUBLANES, ADA_TN), lambda j: (0, j)),
        out_shape=jax.ShapeDtypeStruct((SUBLANES, n_out), F32),
        compiler_params=_cparams(("arbitrary",)),
        name="adaln",
    )(cb, w_ada, b_ada)


INP_TM = 1024
INP_TN = 512
Q_SCALE = HEAD_DIM ** -0.5 * math.log2(math.e)


def _head_norm(xh, g):
    ms = jnp.mean(xh * xh, axis=-1, keepdims=True)
    return xh * lax.rsqrt(ms + EPS) * g


def _rope(xn, rope_ref):
    nxt = pltpu.roll(xn, LANES - 1, 1)
    prv = pltpu.roll(xn, 1, 1)
    return xn * rope_ref[0] + nxt * rope_ref[1] + prv * rope_ref[2]


def _inproj_kernel(x_ref, mod_ref, n1g_ref, w_ref, rope_ref, qg_ref, kg_ref, out_ref, h_ref,
                   *, col_block0, use_rope):
    j = pl.program_id(1)

    @pl.when(j == 0)
    def _():
        xn = _rmsnorm_rows(x_ref[...], n1g_ref[...])
        h = xn * (1.0 + mod_ref[0, 1:2, :]) + mod_ref[0, 0:1, :]
        h_ref[...] = h.astype(BF16)

    acc = _dot(h_ref[...], w_ref[...])
    col = (j + col_block0) * INP_TN

    def heads_epilogue(g_ref, mult):
        for hh in range(INP_TN // HEAD_DIM):
            sl = slice(hh * HEAD_DIM, (hh + 1) * HEAD_DIM)
            xn = _head_norm(acc[:, sl], g_ref[...])
            if use_rope:
                xn = _rope(xn, rope_ref)
            out_ref[:, sl] = (xn * mult).astype(BF16)

    @pl.when(col < Q_END)
    def _():
        heads_epilogue(qg_ref, Q_SCALE)

    @pl.when(jnp.logical_and(col >= Q_END, col < K_END))
    def _():
        heads_epilogue(kg_ref, 1.0)

    @pl.when(jnp.logical_and(col >= K_END, col < F_END))
    def _():
        out_ref[...] = acc.astype(BF16)

    @pl.when(col >= F_END)
    def _():
        out_ref[...] = jax.nn.sigmoid(acc).astype(BF16)


def _inproj(x2, mod3, mod_row_of_tile, n1g, w_bf, rope, qg, kg, *, tm, col_block0, n_col_blocks,
            use_rope, rope_tiles, name):
    rows = x2.shape[0]
    kern = functools.partial(_inproj_kernel, col_block0=col_block0, use_rope=use_rope)
    return pl.pallas_call(
        kern,
        grid=(rows // tm, n_col_blocks),
        in_specs=[
            pl.BlockSpec((tm, D_MODEL), lambda i, j: (i, 0)),
            pl.BlockSpec((1, N_MOD, D_MODEL), lambda i, j: (mod_row_of_tile(i), 0, 0)),
            pl.BlockSpec((1, D_MODEL), lambda i, j: (0, 0)),
            pl.BlockSpec((D_MODEL, INP_TN), lambda i, j: (0, j + col_block0)),
            pl.BlockSpec((3, tm, HEAD_DIM), lambda i, j: (0, i % rope_tiles, 0)),
            pl.BlockSpec((1, HEAD_DIM), lambda i, j: (0, 0)),
            pl.BlockSpec((1, HEAD_DIM), lambda i, j: (0, 0)),
        ],
        out_specs=pl.BlockSpec((tm, INP_TN), lambda i, j: (i, j)),
        out_shape=jax.ShapeDtypeStruct((rows, n_col_blocks * INP_TN), BF16),
        scratch_shapes=[pltpu.VMEM((tm, D_MODEL), BF16)],
        compiler_params=_cparams(("arbitrary", "arbitrary")),
        name=name,
    )(x2, mod3, n1g, w_bf, rope, qg, kg)


ATT_SUB = 256
ATT_NSUB = 2
ATT_TQ = ATT_SUB * ATT_NSUB
ATT_CHUNK = 1024


def _key_chunks(n_keys):
    chunks = [(c0, ATT_CHUNK) for c0 in range(0, n_keys - ATT_CHUNK + 1, ATT_CHUNK)]
    done = len(chunks) * ATT_CHUNK
    if done < n_keys:
        chunks.append((done, n_keys - done))
    return chunks


def _attn_kernel(q_ref, k_ref, v_ref, kc_ref, vc_ref, o_ref, kall_ref, vt_ref, s_ref):
    qi = pl.program_id(2)
    n_lat = k_ref.shape[0]
    n_keys = kall_ref.shape[0]

    @pl.when(qi == 0)
    def _():
        kall_ref[0:n_lat, :] = k_ref[...]
        kall_ref[n_lat:, :] = kc_ref[...]
        vt_ref[:, 0:n_lat] = v_ref[...].astype(F32).T.astype(BF16)
        vt_ref[:, n_lat:] = vc_ref[...].astype(F32).T.astype(BF16)

    units = [(sub, g) for sub in range(ATT_NSUB) for g in range(Q_PER_KV)]
    chunks = _key_chunks(n_keys)

    def q_of(u):
        sub, g = units[u]
        return q_ref[sub * ATT_SUB:(sub + 1) * ATT_SUB, g * HEAD_DIM:(g + 1) * HEAD_DIM]

    m_prev = None
    for t in range(len(units) + 1):
        m_new = l = o_acc = None
        for c0, cs in chunks:
            if t < len(units):
                s_c = lax.dot_general(kall_ref[c0:c0 + cs, :], q_of(t), (((1,), (1,)), ((), ())),
                                      preferred_element_type=F32)
                s_ref[t % 2, c0:c0 + cs, :] = s_c
                mc = jnp.max(s_c, axis=0, keepdims=True)
                m_new = mc if m_new is None else jnp.maximum(m_new, mc)
            if t >= 1:
                p_c = jnp.exp2(s_ref[(t - 1) % 2, c0:c0 + cs, :] - m_prev)
                lc = jnp.sum(p_c, axis=0, keepdims=True)
                l = lc if l is None else l + lc
                pv = _dot(vt_ref[:, c0:c0 + cs], p_c.astype(BF16))
                o_acc = pv if o_acc is None else o_acc + pv
        if t >= 1:
            sub, g = units[t - 1]
            o_ref[sub * ATT_SUB:(sub + 1) * ATT_SUB, g * HEAD_DIM:(g + 1) * HEAD_DIM] = (
                (o_acc * (1.0 / l)).T.astype(BF16))
        m_prev = m_new


def _attention(p, pc, batch, n_lat, n_ctx):
    q_tiles = n_lat // ATT_TQ
    kcol = Q_END // HEAD_DIM
    vcol = K_END // HEAD_DIM
    n_keys = n_lat + n_ctx
    return pl.pallas_call(
        _attn_kernel,
        grid=(batch, N_KV_HEADS, q_tiles),
        in_specs=[
            pl.BlockSpec((ATT_TQ, Q_PER_KV * HEAD_DIM), lambda b, h, qi: (b * q_tiles + qi, h)),
            pl.BlockSpec((n_lat, HEAD_DIM), lambda b, h, qi: (b, kcol + h)),
            pl.BlockSpec((n_lat, HEAD_DIM), lambda b, h, qi: (b, vcol + h)),
            pl.BlockSpec((n_ctx, HEAD_DIM), lambda b, h, qi: (b, h)),
            pl.BlockSpec((n_ctx, HEAD_DIM), lambda b, h, qi: (b, N_KV_HEADS + h)),
        ],
        out_specs=pl.BlockSpec((ATT_TQ, Q_PER_KV * HEAD_DIM), lambda b, h, qi: (b * q_tiles + qi, h)),
        out_shape=jax.ShapeDtypeStruct((batch * n_lat, ATTN_WIDTH), BF16),
        scratch_shapes=[pltpu.VMEM((n_keys, HEAD_DIM), BF16), pltpu.VMEM((HEAD_DIM, n_keys), BF16),
                        pltpu.VMEM((2, n_keys, ATT_SUB), F32)],
        compiler_params=_cparams(("arbitrary", "arbitrary", "arbitrary")),
        name="attn",
    )(p, p, p, pc, pc)


FOU_TM = 512
FOU_TK = 1024


def _fourier_kernel(cs_ref, u_ref, chan_ref, f_ref, acc_ref):
    kk = pl.program_id(2)

    @pl.when(kk == 0)
    def _():
        acc_ref[...] = jnp.zeros(acc_ref.shape, F32)

    lhs = cs_ref[...].reshape(2 * FOU_TM, FOU_TK)
    acc_ref[...] += _dot(lhs, u_ref[...])

    @pl.when(kk == pl.num_programs(2) - 1)
    def _():
        for g in range(N_FOURIER_GROUPS):
            sl = slice(g * FOURIER_GROUP_DIM, (g + 1) * FOURIER_GROUP_DIM)
            gc = acc_ref[0:FOU_TM, sl].astype(BF16)
            gs = acc_ref[FOU_TM:, sl].astype(BF16)
            f_ref[:, sl] = (_dot(gc, chan_ref[0]) - _dot(gs, chan_ref[1])).astype(BF16)


def _fourier(cs_pos, p, cs_chan, batch, n_lat):
    ucol = V_END // FOURIER_WIDTH
    m_tiles = n_lat // FOU_TM
    k_tiles = n_lat // FOU_TK
    return pl.pallas_call(
        _fourier_kernel,
        grid=(batch, m_tiles, k_tiles),
        in_specs=[
            pl.BlockSpec((2, FOU_TM, FOU_TK), lambda b, i, kk: (0, i, kk)),
            pl.BlockSpec((FOU_TK, FOURIER_WIDTH), lambda b, i, kk: (b * k_tiles + kk, ucol)),
            pl.BlockSpec(cs_chan.shape, lambda b, i, kk: (0, 0, 0)),
        ],
        out_specs=pl.BlockSpec((FOU_TM, FOURIER_WIDTH), lambda b, i, kk: (b * m_tiles + i, 0)),
        out_shape=jax.ShapeDtypeStruct((batch * n_lat, FOURIER_WIDTH), BF16),
        scratch_shapes=[pltpu.VMEM((2 * FOU_TM, FOURIER_WIDTH), F32)],
        compiler_params=_cparams(("arbitrary", "arbitrary", "arbitrary")),
        name="fourier",
    )(cs_pos, p, cs_chan)


MRG_TM = 256


def _merge_kernel(o_ref, f_ref, gates_ref, x_ref, mod_ref, wa_ref, wf_ref, wo_ref, x1_ref):
    a = _dot(o_ref[...], wa_ref[...])
    fm = _dot(f_ref[...], wf_ref[...])
    z = gates_ref[:, 0:D_MODEL].astype(F32) * a + gates_ref[:, D_MODEL:].astype(F32) * fm
    y = _dot(z.astype(BF16), wo_ref[...])
    x1_ref[...] = x_ref[...] + mod_ref[0, 2:3, :] * y


def _merge(o, f, p, x2, mod3, wa, wf, wo, n_lat):
    rows = x2.shape[0]
    tiles_per_batch = n_lat // MRG_TM
    const = lambda i: (0, 0)
    single = pl.Buffered(1)
    return pl.pallas_call(
        _merge_kernel,
        grid=(rows // MRG_TM,),
        in_specs=[
            pl.BlockSpec((MRG_TM, ATTN_WIDTH), lambda i: (i, 0)),
            pl.BlockSpec((MRG_TM, FOURIER_WIDTH), lambda i: (i, 0)),
            pl.BlockSpec((MRG_TM, 2 * D_MODEL), lambda i: (i, F_END // (2 * D_MODEL))),
            pl.BlockSpec((MRG_TM, D_MODEL), lambda i: (i, 0)),
            pl.BlockSpec((1, N_MOD, D_MODEL), lambda i: (i // tiles_per_batch, 0, 0)),
            pl.BlockSpec(wa.shape, const, pipeline_mode=single),
            pl.BlockSpec(wf.shape, const, pipeline_mode=single),
            pl.BlockSpec(wo.shape, const, pipeline_mode=single),
        ],
        out_specs=pl.BlockSpec((MRG_TM, D_MODEL), lambda i: (i, 0)),
        out_shape=jax.ShapeDtypeStruct((rows, D_MODEL), F32),
        compiler_params=_cparams(("arbitrary",)),
        name="merge",
    )(o, f, p, x2, mod3, wa, wf, wo)


MLP_TM = 512
MLP_TF = 1024


def _mlp_kernel(x1_ref, mod_ref, n2g_ref, w1_ref, w2_ref, fg_ref, out_ref, h_ref, acc_ref):
    kf = pl.program_id(1)

    @pl.when(kf == 0)
    def _():
        xn = _rmsnorm_rows(x1_ref[...], n2g_ref[...])
        h = xn * (1.0 + mod_ref[0, 4:5, :]) + mod_ref[0, 3:4, :]
        h_ref[...] = h.astype(BF16)
        acc_ref[...] = jnp.zeros(acc_ref.shape, F32)

    hid = jnp.maximum(_dot(h_ref[...], w1_ref[...]), 0.0)
    acc_ref[...] += _dot((hid * hid).astype(BF16), w2_ref[...])

    @pl.when(kf == pl.num_programs(1) - 1)
    def _():
        x2 = x1_ref[...] + mod_ref[0, 5:6, :] * acc_ref[...]
        out_ref[...] = _rmsnorm_rows(x2, fg_ref[...])


def _mlp(x1, mod3, n2g, w1, w2, fg, n_lat):
    rows = x1.shape[0]
    tiles_per_batch = n_lat // MLP_TM
    return pl.pallas_call(
        _mlp_kernel,
        grid=(rows // MLP_TM, D_FF // MLP_TF),
        in_specs=[
            pl.BlockSpec((MLP_TM, D_MODEL), lambda i, kf: (i, 0)),
            pl.BlockSpec((1, N_MOD, D_MODEL), lambda i, kf: (i // tiles_per_batch, 0, 0)),
            pl.BlockSpec((1, D_MODEL), lambda i, kf: (0, 0)),
            pl.BlockSpec((D_MODEL, MLP_TF), lambda i, kf: (0, kf)),
            pl.BlockSpec((MLP_TF, D_MODEL), lambda i, kf: (kf, 0)),
            pl.BlockSpec((1, D_MODEL), lambda i, kf: (0, 0)),
        ],
        out_specs=pl.BlockSpec((MLP_TM, D_MODEL), lambda i, kf: (i, 0)),
        out_shape=jax.ShapeDtypeStruct((rows, D_MODEL), F32),
        scratch_shapes=[pltpu.VMEM((MLP_TM, D_MODEL), BF16), pltpu.VMEM((MLP_TM, D_MODEL), F32)],
        compiler_params=_cparams(("arbitrary", "arbitrary")),
        name="mlp",
    )(x1, mod3, n2g, w1, w2, fg)


def _rope_tables(rows):
    inv = 1.0 / (ROPE_THETA ** (jnp.arange(ROPE_AXIS_PAIRS, dtype=F32) / ROPE_AXIS_PAIRS))
    row_ang = jnp.arange(rows, dtype=F32)[:, None] * inv
    col_ang = jnp.arange(GRID_W, dtype=F32)[:, None] * inv
    ang = jnp.concatenate([
        jnp.broadcast_to(row_ang[:, None, :], (rows, GRID_W, ROPE_AXIS_PAIRS)),
        jnp.broadcast_to(col_ang[None, :, :], (rows, GRID_W, ROPE_AXIS_PAIRS))], axis=-1)
    ang = ang.reshape(rows * GRID_W, 2 * ROPE_AXIS_PAIRS)
    cos, sin = jnp.cos(ang), jnp.sin(ang)
    zero = jnp.zeros_like(sin)
    cos_e = jnp.stack([cos, cos], axis=-1).reshape(-1, HEAD_DIM)
    sin_nxt = jnp.stack([-sin, zero], axis=-1).reshape(-1, HEAD_DIM)
    sin_prv = jnp.stack([zero, sin], axis=-1).reshape(-1, HEAD_DIM)
    return jnp.stack([cos_e, sin_nxt, sin_prv], axis=0)


def _dft_tables(n, scale):
    r = math.isqrt(n)
    assert r * r == n
    idx = jnp.arange(n, dtype=jnp.int32)
    sub = jnp.arange(r, dtype=jnp.int32)

    def cos_sin(mult):
        ang = ((mult[:, None] * idx[None, :]) % n).astype(F32) * (2.0 * math.pi / n)
        return jnp.cos(ang), jnp.sin(ang)

    c_hi, s_hi = cos_sin(sub * r)
    c_lo, s_lo = cos_sin(sub)
    c_hi, s_hi = c_hi[:, None, :] * scale, s_hi[:, None, :] * scale
    c_lo, s_lo = c_lo[None, :, :], s_lo[None, :, :]
    cos = (c_hi * c_lo - s_hi * s_lo).reshape(n, n)
    sin = (s_hi * c_lo + c_hi * s_lo).reshape(n, n)
    return jnp.stack([cos, sin], axis=0).astype(BF16)


def kernel(x, c, ctx, c_ctx, w_ada, b_ada, norm1_g, w_in, q_norm_g, k_norm_g, w_attn_o, w_fourier,
           w_out, norm2_g, w1, w2, final_norm_g):
    batch, n_lat, _ = x.shape
    n_ctx = ctx.shape[1]
    assert w_ada.shape[0] == 1, "single-layer block"
    assert n_lat % INP_TM == 0 and n_ctx % SUBLANES == 0

    cc = jnp.concatenate([c, c_ctx[None, :]], axis=0)
    cb = jnp.broadcast_to(cc[:, :, None], (batch + 1, D_MODEL, LANES))
    mod = _adaln(cb, w_ada[0], b_ada)
    mod3 = mod.reshape(SUBLANES, N_MOD, D_MODEL)

    w_in_bf = w_in[0].astype(BF16)
    n1g = norm1_g
    qg, kg = q_norm_g, k_norm_g
    rope = _rope_tables(n_lat // GRID_W)

    x2 = x.reshape(batch * n_lat, D_MODEL)
    tiles_per_batch = n_lat // INP_TM
    p = _inproj(x2, mod3, lambda i: i // tiles_per_batch, n1g, w_in_bf, rope, qg, kg,
                tm=INP_TM, col_block0=0, n_col_blocks=IN_WIDTH // INP_TN, use_rope=True,
                rope_tiles=tiles_per_batch, name="inproj")
    ctx2 = ctx.reshape(batch * n_ctx, D_MODEL)
    pc = _inproj(ctx2, mod3, lambda i: batch, n1g, w_in_bf, rope, qg, kg,
                 tm=batch * n_ctx, col_block0=Q_END // INP_TN, n_col_blocks=(V_END - Q_END) // INP_TN,
                 use_rope=False, rope_tiles=1, name="ctxproj")

    o = _attention(p, pc, batch, n_lat, n_ctx)

    cs_pos = _dft_tables(n_lat, n_lat ** -0.5)
    cs_chan = _dft_tables(FOURIER_GROUP_DIM, FOURIER_GROUP_DIM ** -0.5)
    f = _fourier(cs_pos, p, cs_chan, batch, n_lat)

    x1 = _merge(o, f, p, x2, mod3, w_attn_o[0].astype(BF16), w_fourier[0].astype(BF16),
                w_out[0].astype(BF16), n_lat)
    out = _mlp(x1, mod3, norm2_g, w1[0].astype(BF16), w2[0].astype(BF16),
               final_norm_g[None, :], n_lat)
    return out.reshape(batch, n_lat, D_MODEL)
```

```python
import functools
import math

import jax
import jax.numpy as jnp
import numpy as np
from jax import lax
from jax.experimental import pallas as pl
from jax.experimental.pallas import tpu as pltpu

D_MODEL = 2048
GRID_W = 64
HEAD_DIM = 128
N_Q_HEADS = 16
N_KV_HEADS = 4
Q_PER_KV = N_Q_HEADS // N_KV_HEADS
ATTN_WIDTH = N_Q_HEADS * HEAD_DIM
KV_WIDTH = N_KV_HEADS * HEAD_DIM
N_FOURIER_GROUPS = 4
FOURIER_GROUP_DIM = 256
FOURIER_WIDTH = N_FOURIER_GROUPS * FOURIER_GROUP_DIM
Q_END = ATTN_WIDTH
K_END = Q_END + KV_WIDTH
V_END = K_END + KV_WIDTH
F_END = V_END + FOURIER_WIDTH
IN_WIDTH = F_END + 2 * D_MODEL
D_FF = 4 * D_MODEL
N_MOD = 6
ROPE_THETA = 10000.0
ROPE_AXIS_PAIRS = HEAD_DIM // 4
EPS = 1e-6

LANES = 128
SUBLANES = 8
VMEM_LIMIT = 56 * 1024 * 1024

BF16 = jnp.bfloat16
F32 = jnp.float32


def _cparams(sem):
    return pltpu.CompilerParams(dimension_semantics=sem, vmem_limit_bytes=VMEM_LIMIT)


def _dot(a, b):
    return jnp.dot(a, b, preferred_element_type=F32)


def _rmsnorm_rows(x, g):
    ms = jnp.mean(x * x, axis=-1, keepdims=True)
    return x * lax.rsqrt(ms + EPS) * g


ADA_TN = 512


def _adaln_kernel(cb_ref, w_ref, b_ref, out_ref):
    n_rows = cb_ref.shape[0]
    acts = []
    for r in range(n_rows):
        cr = cb_ref[r]
        acts.append(cr * jax.nn.sigmoid(cr))
    out_ref[...] = jnp.zeros(out_ref.shape, F32)
    for t in range(ADA_TN // LANES):
        wt = w_ref[:, t * LANES:(t + 1) * LANES]
        for r in range(n_rows):
            prod = wt * acts[r]
            red = prod.reshape(D_MODEL // SUBLANES, SUBLANES, LANES).sum(axis=0)
            tot = red.sum(axis=0, keepdims=True) + b_ref[:, t * LANES:(t + 1) * LANES]
            out_ref[r:r + 1, t * LANES:(t + 1) * LANES] = tot


def _adaln(cb, w_ada, b_ada):
    n_out = w_ada.shape[1]
    return pl.pallas_call(
        _adaln_kernel,
        grid=(n_out // ADA_TN,),
        in_specs=[
            pl.BlockSpec(cb.shape, lambda j: (0, 0, 0)),
            pl.BlockSpec((D_MODEL, ADA_TN), lambda j: (0, j)),
            pl.BlockSpec((1, ADA_TN), lambda j: (0, j)),
        ],
        out_specs=pl.BlockSpec((SUBLANES, ADA_TN), lambda j: (0, j)),
        out_shape=jax.ShapeDtypeStruct((SUBLANES, n_out), F32),
        compiler_params=_cparams(("arbitrary",)),
        name="adaln",
    )(cb, w_ada, b_ada)


INP_TM = 1024
INP_TN = 1024
INP_SUB = 256
Q_SCALE = HEAD_DIM ** -0.5 * math.log2(math.e)


def _head_norm_rope(acc, g, sel_ref, rope_ref, mult):
    xg = acc * g
    ssq = _dot((acc * acc).astype(BF16), sel_ref[0])
    r = lax.rsqrt(ssq * (1.0 / HEAD_DIM) + EPS) * mult
    if rope_ref is None:
        return xg * r
    swapped = _dot(xg.astype(BF16), sel_ref[1])
    return (xg * rope_ref[0] + swapped * rope_ref[1]) * r


def _col_kind(col):
    if col < Q_END:
        return "q"
    if col < K_END:
        return "k"
    if col < F_END:
        return "plain"
    return "gate"


def _inproj_kernel(x_ref, mod_ref, n1g_ref, w_ref, rope_ref, qg_ref, kg_ref, sel_ref, out_ref, h_ref,
                   *, col_block0, n_col_blocks, use_rope):
    j = pl.program_id(1)

    @pl.when(j == 0)
    def _():
        xn = _rmsnorm_rows(x_ref[...], n1g_ref[...])
        h = xn * (1.0 + mod_ref[0, 1:2, :]) + mod_ref[0, 0:1, :]
        h_ref[...] = h.astype(BF16)

    def epilogue(acc, kind, c0):
        if kind in ("q", "k"):
            g_ref, mult = (qg_ref, Q_SCALE) if kind == "q" else (kg_ref, 1.0)
            res = _head_norm_rope(acc, g_ref[...], sel_ref, rope_ref if use_rope else None, mult)
            out_ref[:, c0:c0 + INP_SUB] = res.astype(BF16)
        elif kind == "plain":
            out_ref[:, c0:c0 + INP_SUB] = acc.astype(BF16)
        else:
            out_ref[:, c0:c0 + INP_SUB] = jax.nn.sigmoid(acc).astype(BF16)

    def tile(col_block):
        subs = range(0, INP_TN, INP_SUB)
        pending = None
        for c0 in subs:
            acc = _dot(h_ref[...], w_ref[:, c0:c0 + INP_SUB])
            if pending is not None:
                epilogue(*pending)
            pending = (acc, _col_kind(col_block * INP_TN + c0), c0)
        epilogue(*pending)

    def kinds_of(cb):
        return tuple(_col_kind(cb * INP_TN + c0) for c0 in range(0, INP_TN, INP_SUB))

    lo = 0
    while lo < n_col_blocks:
        hi = lo
        while hi + 1 < n_col_blocks and kinds_of(col_block0 + hi + 1) == kinds_of(col_block0 + lo):
            hi += 1
        pl.when(jnp.logical_and(j >= lo, j <= hi))(functools.partial(tile, col_block0 + lo))
        lo = hi + 1


def _head_select_matrices():
    idx = np.arange(INP_SUB)
    ones_bd = (idx[:, None] // HEAD_DIM == idx[None, :] // HEAD_DIM).astype(np.float32)
    swap = np.zeros((INP_SUB, INP_SUB), np.float32)
    swap[idx[1::2], idx[0::2]] = -1.0
    swap[idx[0::2], idx[1::2]] = 1.0
    return jnp.asarray(np.stack([ones_bd, swap]), dtype=BF16)


def _inproj(x2, mod3, mod_row_of_tile, n1g, w_bf, rope, qg, kg, *, tm, col_block0, n_col_blocks,
            use_rope, rope_tiles, name):
    rows = x2.shape[0]
    sel = _head_select_matrices()
    kern = functools.partial(_inproj_kernel, col_block0=col_block0, n_col_blocks=n_col_blocks,
                             use_rope=use_rope)
    return pl.pallas_call(
        kern,
        grid=(rows // tm, n_col_blocks),
        in_specs=[
            pl.BlockSpec((tm, D_MODEL), lambda i, j: (i, 0)),
            pl.BlockSpec((1, N_MOD, D_MODEL), lambda i, j: (mod_row_of_tile(i), 0, 0)),
            pl.BlockSpec((1, D_MODEL), lambda i, j: (0, 0)),
            pl.BlockSpec((D_MODEL, INP_TN), lambda i, j: (0, j + col_block0)),
            pl.BlockSpec((2, tm, INP_SUB), lambda i, j: (0, i % rope_tiles, 0)),
            pl.BlockSpec((1, INP_SUB), lambda i, j: (0, 0)),
            pl.BlockSpec((1, INP_SUB), lambda i, j: (0, 0)),
            pl.BlockSpec(sel.shape, lambda i, j: (0, 0, 0)),
        ],
        out_specs=pl.BlockSpec((tm, INP_TN), lambda i, j: (i, j)),
        out_shape=jax.ShapeDtypeStruct((rows, n_col_blocks * INP_TN), BF16),
        scratch_shapes=[pltpu.VMEM((tm, D_MODEL), BF16)],
        compiler_params=_cparams(("arbitrary", "arbitrary")),
        name=name,
    )(x2, mod3, n1g, w_bf, rope, qg, kg, sel)


ATT_SUB = 256
ATT_NSUB = 2
ATT_TQ = ATT_SUB * ATT_NSUB
ATT_CHUNK = 1024


def _key_chunks(n_keys):
    chunks = [(c0, ATT_CHUNK) for c0 in range(0, n_keys - ATT_CHUNK + 1, ATT_CHUNK)]
    done = len(chunks) * ATT_CHUNK
    if done < n_keys:
        chunks.append((done, n_keys - done))
    return chunks


def _attn_kernel(q_ref, k_ref, v_ref, kc_ref, vc_ref, o_ref, kall_ref, vt_ref, s_ref):
    qi = pl.program_id(2)
    n_lat = k_ref.shape[0]
    n_keys = kall_ref.shape[0]

    @pl.when(qi == 0)
    def _():
        kall_ref[0:n_lat, :] = k_ref[...]
        kall_ref[n_lat:, :] = kc_ref[...]
        vt_ref[:, 0:n_lat] = v_ref[...].astype(F32).T.astype(BF16)
        vt_ref[:, n_lat:] = vc_ref[...].astype(F32).T.astype(BF16)

    units = [(sub, g) for sub in range(ATT_NSUB) for g in range(Q_PER_KV)]
    chunks = _key_chunks(n_keys)

    def q_of(u):
        sub, g = units[u]
        return q_ref[sub * ATT_SUB:(sub + 1) * ATT_SUB, g * HEAD_DIM:(g + 1) * HEAD_DIM]

    m_prev = None
    for t in range(len(units) + 1):
        m_new = l = o_acc = None
        for c0, cs in chunks:
            if t < len(units):
                s_c = lax.dot_general(kall_ref[c0:c0 + cs, :], q_of(t), (((1,), (1,)), ((), ())),
                                      preferred_element_type=F32)
                s_ref[t % 2, c0:c0 + cs, :] = s_c
                mc = jnp.max(s_c, axis=0, keepdims=True)
                m_new = mc if m_new is None else jnp.maximum(m_new, mc)
            if t >= 1:
                p_c = jnp.exp2(s_ref[(t - 1) % 2, c0:c0 + cs, :] - m_prev)
                lc = jnp.sum(p_c, axis=0, keepdims=True)
                l = lc if l is None else l + lc
                pv = _dot(vt_ref[:, c0:c0 + cs], p_c.astype(BF16))
                o_acc = pv if o_acc is None else o_acc + pv
        if t >= 1:
            sub, g = units[t - 1]
            o_ref[sub * ATT_SUB:(sub + 1) * ATT_SUB, g * HEAD_DIM:(g + 1) * HEAD_DIM] = (
                (o_acc * (1.0 / l)).T.astype(BF16))
        m_prev = m_new


def _attention(p, pc, batch, n_lat, n_ctx):
    q_tiles = n_lat // ATT_TQ
    kcol = Q_END // HEAD_DIM
    vcol = K_END // HEAD_DIM
    n_keys = n_lat + n_ctx
    return pl.pallas_call(
        _attn_kernel,
        grid=(batch, N_KV_HEADS, q_tiles),
        in_specs=[
            pl.BlockSpec((ATT_TQ, Q_PER_KV * HEAD_DIM), lambda b, h, qi: (b * q_tiles + qi, h)),
            pl.BlockSpec((n_lat, HEAD_DIM), lambda b, h, qi: (b, kcol + h)),
            pl.BlockSpec((n_lat, HEAD_DIM), lambda b, h, qi: (b, vcol + h)),
            pl.BlockSpec((n_ctx, HEAD_DIM), lambda b, h, qi: (b, h)),
            pl.BlockSpec((n_ctx, HEAD_DIM), lambda b, h, qi: (b, N_KV_HEADS + h)),
        ],
        out_specs=pl.BlockSpec((ATT_TQ, Q_PER_KV * HEAD_DIM), lambda b, h, qi: (b * q_tiles + qi, h)),
        out_shape=jax.ShapeDtypeStruct((batch * n_lat, ATTN_WIDTH), BF16),
        scratch_shapes=[pltpu.VMEM((n_keys, HEAD_DIM), BF16), pltpu.VMEM((HEAD_DIM, n_keys), BF16),
                        pltpu.VMEM((2, n_keys, ATT_SUB), F32)],
        compiler_params=_cparams(("arbitrary", "arbitrary", "arbitrary")),
        name="attn",
    )(p, p, p, pc, pc)


FOU_TM = 512
FOU_TK = 1024


def _fourier_kernel(cs_ref, u_ref, chan_ref, f_ref, acc_ref):
    kk = pl.program_id(2)

    @pl.when(kk == 0)
    def _():
        acc_ref[...] = jnp.zeros(acc_ref.shape, F32)

    lhs = cs_ref[...].reshape(2 * FOU_TM, FOU_TK)
    acc_ref[...] += _dot(lhs, u_ref[...])

    @pl.when(kk == pl.num_programs(2) - 1)
    def _():
        for g in range(N_FOURIER_GROUPS):
            sl = slice(g * FOURIER_GROUP_DIM, (g + 1) * FOURIER_GROUP_DIM)
            gc = acc_ref[0:FOU_TM, sl].astype(BF16)
            gs = acc_ref[FOU_TM:, sl].astype(BF16)
            f_ref[:, sl] = (_dot(gc, chan_ref[0]) - _dot(gs, chan_ref[1])).astype(BF16)


def _fourier(cs_pos, p, cs_chan, batch, n_lat):
    ucol = V_END // FOURIER_WIDTH
    m_tiles = n_lat // FOU_TM
    k_tiles = n_lat // FOU_TK
    return pl.pallas_call(
        _fourier_kernel,
        grid=(batch, m_tiles, k_tiles),
        in_specs=[
            pl.BlockSpec((2, FOU_TM, FOU_TK), lambda b, i, kk: (0, i, kk)),
            pl.BlockSpec((FOU_TK, FOURIER_WIDTH), lambda b, i, kk: (b * k_tiles + kk, ucol)),
            pl.BlockSpec(cs_chan.shape, lambda b, i, kk: (0, 0, 0)),
        ],
        out_specs=pl.BlockSpec((FOU_TM, FOURIER_WIDTH), lambda b, i, kk: (b * m_tiles + i, 0)),
        out_shape=jax.ShapeDtypeStruct((batch * n_lat, FOURIER_WIDTH), BF16),
        scratch_shapes=[pltpu.VMEM((2 * FOU_TM, FOURIER_WIDTH), F32)],
        compiler_params=_cparams(("arbitrary", "arbitrary", "arbitrary")),
        name="fourier",
    )(cs_pos, p, cs_chan)


MRG_TM = 256


def _merge_kernel(o_ref, f_ref, gates_ref, x_ref, mod_ref, wa_ref, wf_ref, wo_ref, x1_ref):
    a = _dot(o_ref[...], wa_ref[...])
    fm = _dot(f_ref[...], wf_ref[...])
    z = gates_ref[:, 0:D_MODEL].astype(F32) * a + gates_ref[:, D_MODEL:].astype(F32) * fm
    y = _dot(z.astype(BF16), wo_ref[...])
    x1_ref[...] = x_ref[...] + mod_ref[0, 2:3, :] * y


def _merge(o, f, p, x2, mod3, wa, wf, wo, n_lat):
    rows = x2.shape[0]
    tiles_per_batch = n_lat // MRG_TM
    const = lambda i: (0, 0)
    single = pl.Buffered(1)
    return pl.pallas_call(
        _merge_kernel,
        grid=(rows // MRG_TM,),
        in_specs=[
            pl.BlockSpec((MRG_TM, ATTN_WIDTH), lambda i: (i, 0)),
            pl.BlockSpec((MRG_TM, FOURIER_WIDTH), lambda i: (i, 0)),
            pl.BlockSpec((MRG_TM, 2 * D_MODEL), lambda i: (i, F_END // (2 * D_MODEL))),
            pl.BlockSpec((MRG_TM, D_MODEL), lambda i: (i, 0)),
            pl.BlockSpec((1, N_MOD, D_MODEL), lambda i: (i // tiles_per_batch, 0, 0)),
            pl.BlockSpec(wa.shape, const, pipeline_mode=single),
            pl.BlockSpec(wf.shape, const, pipeline_mode=single),
            pl.BlockSpec(wo.shape, const, pipeline_mode=single),
        ],
        out_specs=pl.BlockSpec((MRG_TM, D_MODEL), lambda i: (i, 0)),
        out_shape=jax.ShapeDtypeStruct((rows, D_MODEL), F32),
        compiler_params=_cparams(("arbitrary",)),
        name="merge",
    )(o, f, p, x2, mod3, wa, wf, wo)


MLP_TM = 512
MLP_TF = 1024


def _mlp_kernel(x1_ref, mod_ref, n2g_ref, w1_ref, w2_ref, fg_ref, out_ref, h_ref, acc_ref):
    kf = pl.program_id(1)

    @pl.when(kf == 0)
    def _():
        xn = _rmsnorm_rows(x1_ref[...], n2g_ref[...])
        h = xn * (1.0 + mod_ref[0, 4:5, :]) + mod_ref[0, 3:4, :]
        h_ref[...] = h.astype(BF16)
        acc_ref[...] = jnp.zeros(acc_ref.shape, F32)

    hid = jnp.maximum(_dot(h_ref[...], w1_ref[...]), 0.0)
    acc_ref[...] += _dot((hid * hid).astype(BF16), w2_ref[...])

    @pl.when(kf == pl.num_programs(1) - 1)
    def _():
        x2 = x1_ref[...] + mod_ref[0, 5:6, :] * acc_ref[...]
        out_ref[...] = _rmsnorm_rows(x2, fg_ref[...])


def _mlp(x1, mod3, n2g, w1, w2, fg, n_lat):
    rows = x1.shape[0]
    tiles_per_batch = n_lat // MLP_TM
    return pl.pallas_call(
        _mlp_kernel,
        grid=(rows // MLP_TM, D_FF // MLP_TF),
        in_specs=[
            pl.BlockSpec((MLP_TM, D_MODEL), lambda i, kf: (i, 0)),
            pl.BlockSpec((1, N_MOD, D_MODEL), lambda i, kf: (i // tiles_per_batch, 0, 0)),
            pl.BlockSpec((1, D_MODEL), lambda i, kf: (0, 0)),
            pl.BlockSpec((D_MODEL, MLP_TF), lambda i, kf: (0, kf)),
            pl.BlockSpec((MLP_TF, D_MODEL), lambda i, kf: (kf, 0)),
            pl.BlockSpec((1, D_MODEL), lambda i, kf: (0, 0)),
        ],
        out_specs=pl.BlockSpec((MLP_TM, D_MODEL), lambda i, kf: (i, 0)),
        out_shape=jax.ShapeDtypeStruct((rows, D_MODEL), F32),
        scratch_shapes=[pltpu.VMEM((MLP_TM, D_MODEL), BF16), pltpu.VMEM((MLP_TM, D_MODEL), F32)],
        compiler_params=_cparams(("arbitrary", "arbitrary")),
        name="mlp",
    )(x1, mod3, n2g, w1, w2, fg)


def _rope_tables(rows):
    lane = jnp.arange(INP_SUB, dtype=jnp.int32)[None, :] % HEAD_DIM
    pos = jnp.arange(rows * GRID_W, dtype=jnp.int32)[:, None]
    pair = lane // 2
    inv = 1.0 / (ROPE_THETA ** ((pair % ROPE_AXIS_PAIRS).astype(F32) / ROPE_AXIS_PAIRS))
    coord = jnp.where(pair < ROPE_AXIS_PAIRS, pos // GRID_W, pos % GRID_W).astype(F32)
    ang = coord * inv
    return jnp.stack([jnp.cos(ang), jnp.sin(ang)], axis=0)


def _dft_tables(n, scale):
    r = math.isqrt(n)
    assert r * r == n
    idx = jnp.arange(n, dtype=jnp.int32)
    sub = jnp.arange(r, dtype=jnp.int32)

    def cos_sin(mult):
        ang = ((mult[:, None] * idx[None, :]) % n).astype(F32) * (2.0 * math.pi / n)
        return jnp.cos(ang), jnp.sin(ang)

    c_hi, s_hi = cos_sin(sub * r)
    c_lo, s_lo = cos_sin(sub)
    c_hi, s_hi = c_hi[:, None, :] * scale, s_hi[:, None, :] * scale
    c_lo, s_lo = c_lo[None, :, :], s_lo[None, :, :]
    cos = (c_hi * c_lo - s_hi * s_lo).reshape(n, n)
    sin = (s_hi * c_lo + c_hi * s_lo).reshape(n, n)
    return jnp.stack([cos, sin], axis=0).astype(BF16)


def kernel(x, c, ctx, c_ctx, w_ada, b_ada, norm1_g, w_in, q_norm_g, k_norm_g, w_attn_o, w_fourier,
           w_out, norm2_g, w1, w2, final_norm_g):
    batch, n_lat, _ = x.shape
    n_ctx = ctx.shape[1]
    assert w_ada.shape[0] == 1, "single-layer block"
    assert n_lat % INP_TM == 0 and n_ctx % SUBLANES == 0

    cc = jnp.concatenate([c, c_ctx[None, :]], axis=0)
    cb = jnp.broadcast_to(cc[:, :, None], (batch + 1, D_MODEL, LANES))
    mod = _adaln(cb, w_ada[0], b_ada)
    mod3 = mod.reshape(SUBLANES, N_MOD, D_MODEL)

    w_in_bf = w_in[0].astype(BF16)
    n1g = norm1_g
    heads_per_sub = INP_SUB // HEAD_DIM
    qg, kg = jnp.tile(q_norm_g, (1, heads_per_sub)), jnp.tile(k_norm_g, (1, heads_per_sub))
    rope = _rope_tables(n_lat // GRID_W)

    x2 = x.reshape(batch * n_lat, D_MODEL)
    tiles_per_batch = n_lat // INP_TM
    p = _inproj(x2, mod3, lambda i: i // tiles_per_batch, n1g, w_in_bf, rope, qg, kg,
                tm=INP_TM, col_block0=0, n_col_blocks=IN_WIDTH // INP_TN, use_rope=True,
                rope_tiles=tiles_per_batch, name="inproj")
    ctx2 = ctx.reshape(batch * n_ctx, D_MODEL)
    pc = _inproj(ctx2, mod3, lambda i: batch, n1g, w_in_bf, rope, qg, kg,
                 tm=batch * n_ctx, col_block0=Q_END // INP_TN, n_col_blocks=(V_END - Q_END) // INP_TN,
                 use_rope=False, rope_tiles=1, name="ctxproj")

    o = _attention(p, pc, batch, n_lat, n_ctx)

    cs_pos = _dft_tables(n_lat, n_lat ** -0.5)
    cs_chan = _dft_tables(FOURIER_GROUP_DIM, FOURIER_GROUP_DIM ** -0.5)
    f = _fourier(cs_pos, p, cs_chan, batch, n_lat)

    x1 = _merge(o, f, p, x2, mod3, w_attn_o[0].astype(BF16), w_fourier[0].astype(BF16),
                w_out[0].astype(BF16), n_lat)
    out = _mlp(x1, mod3, norm2_g, w1[0].astype(BF16), w2[0].astype(BF16),
               final_norm_g[None, :], n_lat)
    return out.reshape(batch, n_lat, D_MODEL)
```

```python
import functools
import math

import jax
import jax.numpy as jnp
import numpy as np
from jax import lax
from jax.experimental import pallas as pl
from jax.experimental.pallas import tpu as pltpu

D_MODEL = 2048
GRID_W = 64
HEAD_DIM = 128
N_Q_HEADS = 16
N_KV_HEADS = 4
Q_PER_KV = N_Q_HEADS // N_KV_HEADS
ATTN_WIDTH = N_Q_HEADS * HEAD_DIM
KV_WIDTH = N_KV_HEADS * HEAD_DIM
N_FOURIER_GROUPS = 4
FOURIER_GROUP_DIM = 256
FOURIER_WIDTH = N_FOURIER_GROUPS * FOURIER_GROUP_DIM
Q_END = ATTN_WIDTH
K_END = Q_END + KV_WIDTH
V_END = K_END + KV_WIDTH
F_END = V_END + FOURIER_WIDTH
IN_WIDTH = F_END + 2 * D_MODEL
D_FF = 4 * D_MODEL
N_MOD = 6
ROPE_THETA = 10000.0
ROPE_AXIS_PAIRS = HEAD_DIM // 4
EPS = 1e-6

LANES = 128
SUBLANES = 8
VMEM_LIMIT = 56 * 1024 * 1024

BF16 = jnp.bfloat16
F32 = jnp.float32


def _cparams(sem):
    return pltpu.CompilerParams(dimension_semantics=sem, vmem_limit_bytes=VMEM_LIMIT)


def _dot(a, b):
    return jnp.dot(a, b, preferred_element_type=F32)


def _rmsnorm_rows(x, g):
    ms = jnp.mean(x * x, axis=-1, keepdims=True)
    return x * lax.rsqrt(ms + EPS) * g


ADA_TN = 512


def _adaln_kernel(cb_ref, w_ref, b_ref, out_ref):
    n_rows = cb_ref.shape[0]
    acts = []
    for r in range(n_rows):
        cr = cb_ref[r]
        acts.append(cr * jax.nn.sigmoid(cr))
    out_ref[...] = jnp.zeros(out_ref.shape, F32)
    for t in range(ADA_TN // LANES):
        wt = w_ref[:, t * LANES:(t + 1) * LANES]
        for r in range(n_rows):
            prod = wt * acts[r]
            red = prod.reshape(D_MODEL // SUBLANES, SUBLANES, LANES).sum(axis=0)
            tot = red.sum(axis=0, keepdims=True) + b_ref[:, t * LANES:(t + 1) * LANES]
            out_ref[r:r + 1, t * LANES:(t + 1) * LANES] = tot


def _adaln(cb, w_ada, b_ada):
    n_out = w_ada.shape[1]
    return pl.pallas_call(
        _adaln_kernel,
        grid=(n_out // ADA_TN,),
        in_specs=[
            pl.BlockSpec(cb.shape, lambda j: (0, 0, 0)),
            pl.BlockSpec((D_MODEL, ADA_TN), lambda j: (0, j)),
            pl.BlockSpec((1, ADA_TN), lambda j: (0, j)),
        ],
        out_specs=pl.BlockSpec((SUBLANES, ADA_TN), lambda j: (0, j)),
        out_shape=jax.ShapeDtypeStruct((SUBLANES, n_out), F32),
        compiler_params=_cparams(("arbitrary",)),
        name="adaln",
    )(cb, w_ada, b_ada)


INP_TM = 1024
INP_TN = 1024
INP_SUB = 256
Q_SCALE = HEAD_DIM ** -0.5 * math.log2(math.e)


def _head_norm_rope(acc, g, sel_ref, rope_ref, mult):
    xg = acc * g
    ssq = _dot((acc * acc).astype(BF16), sel_ref[0])
    r = lax.rsqrt(ssq * (1.0 / HEAD_DIM) + EPS) * mult
    if rope_ref is None:
        return xg * r
    swapped = _dot(xg.astype(BF16), sel_ref[1])
    return (xg * rope_ref[0] + swapped * rope_ref[1]) * r


def _col_kind(col):
    if col < Q_END:
        return "q"
    if col < K_END:
        return "k"
    if col < F_END:
        return "plain"
    return "gate"


def _inproj_kernel(x_ref, mod_ref, n1g_ref, w_ref, rope_ref, qg_ref, kg_ref, sel_ref, out_ref, h_ref,
                   *, col_block0, n_col_blocks, use_rope):
    j = pl.program_id(1)

    @pl.when(j == 0)
    def _():
        xn = _rmsnorm_rows(x_ref[...], n1g_ref[...])
        h = xn * (1.0 + mod_ref[0, 1:2, :]) + mod_ref[0, 0:1, :]
        h_ref[...] = h.astype(BF16)

    def epilogue(acc, kind, c0):
        if kind in ("q", "k"):
            g_ref, mult = (qg_ref, Q_SCALE) if kind == "q" else (kg_ref, 1.0)
            res = _head_norm_rope(acc, g_ref[...], sel_ref, rope_ref if use_rope else None, mult)
            out_ref[:, c0:c0 + INP_SUB] = res.astype(BF16)
        elif kind == "plain":
            out_ref[:, c0:c0 + INP_SUB] = acc.astype(BF16)
        else:
            out_ref[:, c0:c0 + INP_SUB] = jax.nn.sigmoid(acc).astype(BF16)

    def tile(col_block):
        subs = range(0, INP_TN, INP_SUB)
        pending = None
        for c0 in subs:
            acc = _dot(h_ref[...], w_ref[:, c0:c0 + INP_SUB])
            if pending is not None:
                epilogue(*pending)
            pending = (acc, _col_kind(col_block * INP_TN + c0), c0)
        epilogue(*pending)

    def kinds_of(cb):
        return tuple(_col_kind(cb * INP_TN + c0) for c0 in range(0, INP_TN, INP_SUB))

    lo = 0
    while lo < n_col_blocks:
        hi = lo
        while hi + 1 < n_col_blocks and kinds_of(col_block0 + hi + 1) == kinds_of(col_block0 + lo):
            hi += 1
        pl.when(jnp.logical_and(j >= lo, j <= hi))(functools.partial(tile, col_block0 + lo))
        lo = hi + 1


def _head_select_matrices():
    idx = np.arange(INP_SUB)
    ones_bd = (idx[:, None] // HEAD_DIM == idx[None, :] // HEAD_DIM).astype(np.float32)
    swap = np.zeros((INP_SUB, INP_SUB), np.float32)
    swap[idx[1::2], idx[0::2]] = -1.0
    swap[idx[0::2], idx[1::2]] = 1.0
    return jnp.asarray(np.stack([ones_bd, swap]), dtype=BF16)


def _inproj(x2, mod3, mod_row_of_tile, n1g, w_bf, rope, qg, kg, *, tm, col_block0, n_col_blocks,
            use_rope, rope_tiles, name):
    rows = x2.shape[0]
    sel = _head_select_matrices()
    kern = functools.partial(_inproj_kernel, col_block0=col_block0, n_col_blocks=n_col_blocks,
                             use_rope=use_rope)
    return pl.pallas_call(
        kern,
        grid=(rows // tm, n_col_blocks),
        in_specs=[
            pl.BlockSpec((tm, D_MODEL), lambda i, j: (i, 0)),
            pl.BlockSpec((1, N_MOD, D_MODEL), lambda i, j: (mod_row_of_tile(i), 0, 0)),
            pl.BlockSpec((1, D_MODEL), lambda i, j: (0, 0)),
            pl.BlockSpec((D_MODEL, INP_TN), lambda i, j: (0, j + col_block0)),
            pl.BlockSpec((2, tm, INP_SUB), lambda i, j: (0, i % rope_tiles, 0)),
            pl.BlockSpec((1, INP_SUB), lambda i, j: (0, 0)),
            pl.BlockSpec((1, INP_SUB), lambda i, j: (0, 0)),
            pl.BlockSpec(sel.shape, lambda i, j: (0, 0, 0)),
        ],
        out_specs=pl.BlockSpec((tm, INP_TN), lambda i, j: (i, j)),
        out_shape=jax.ShapeDtypeStruct((rows, n_col_blocks * INP_TN), BF16),
        scratch_shapes=[pltpu.VMEM((tm, D_MODEL), BF16)],
        compiler_params=_cparams(("arbitrary", "arbitrary")),
        name=name,
    )(x2, mod3, n1g, w_bf, rope, qg, kg, sel)


ATT_SUB = 256
ATT_NSUB = 2
ATT_TQ = ATT_SUB * ATT_NSUB
ATT_CHUNK = 1024


def _key_chunks(n_keys):
    chunks = [(c0, ATT_CHUNK) for c0 in range(0, n_keys - ATT_CHUNK + 1, ATT_CHUNK)]
    done = len(chunks) * ATT_CHUNK
    if done < n_keys:
        chunks.append((done, n_keys - done))
    return chunks


def _attn_kernel(q_ref, qn_ref, k_ref, v_ref, kc_ref, vc_ref, o_ref, kall_ref, vt_ref, s_ref, m_ref):
    qi = pl.program_id(2)
    n_lat = k_ref.shape[0]
    n_keys = kall_ref.shape[0]

    units = [(sub, g) for sub in range(ATT_NSUB) for g in range(Q_PER_KV)]
    assert len(units) % 2 == 0
    chunks = _key_chunks(n_keys)

    def q_of(ref, u):
        sub, g = units[u]
        return ref[sub * ATT_SUB:(sub + 1) * ATT_SUB, g * HEAD_DIM:(g + 1) * HEAD_DIM]

    def scores_chunk(q_u, slot, c0, cs, m_run):
        s_c = lax.dot_general(kall_ref[c0:c0 + cs, :], q_u, (((1,), (1,)), ((), ())),
                              preferred_element_type=F32)
        s_ref[slot, c0:c0 + cs, :] = s_c
        mc = jnp.max(s_c, axis=0, keepdims=True)
        return mc if m_run is None else jnp.maximum(m_run, mc)

    @pl.when(qi == 0)
    def _():
        kall_ref[0:n_lat, :] = k_ref[...]
        kall_ref[n_lat:, :] = kc_ref[...]
        vt_ref[:, 0:n_lat] = v_ref[...].astype(F32).T.astype(BF16)
        vt_ref[:, n_lat:] = vc_ref[...].astype(F32).T.astype(BF16)
        m0 = None
        for c0, cs in chunks:
            m0 = scores_chunk(q_of(q_ref, 0), 0, c0, cs, m0)
        m_ref[...] = m0

    m_cur = m_ref[...]
    for t in range(len(units)):
        last = t + 1 == len(units)
        q_nxt = q_of(qn_ref, 0) if last else q_of(q_ref, t + 1)
        m_nxt = l = o_acc = None
        for c0, cs in chunks:
            m_nxt = scores_chunk(q_nxt, (t + 1) % 2, c0, cs, m_nxt)
            p_c = jnp.exp2(s_ref[t % 2, c0:c0 + cs, :] - m_cur)
            lc = jnp.sum(p_c, axis=0, keepdims=True)
            l = lc if l is None else l + lc
            pv = _dot(vt_ref[:, c0:c0 + cs], p_c.astype(BF16))
            o_acc = pv if o_acc is None else o_acc + pv
        sub, g = units[t]
        o_ref[sub * ATT_SUB:(sub + 1) * ATT_SUB, g * HEAD_DIM:(g + 1) * HEAD_DIM] = (
            (o_acc * (1.0 / l)).T.astype(BF16))
        m_cur = m_nxt
    m_ref[...] = m_cur


def _attention(p, pc, batch, n_lat, n_ctx):
    q_tiles = n_lat // ATT_TQ
    kcol = Q_END // HEAD_DIM
    vcol = K_END // HEAD_DIM
    n_keys = n_lat + n_ctx
    return pl.pallas_call(
        _attn_kernel,
        grid=(batch, N_KV_HEADS, q_tiles),
        in_specs=[
            pl.BlockSpec((ATT_TQ, Q_PER_KV * HEAD_DIM), lambda b, h, qi: (b * q_tiles + qi, h)),
            pl.BlockSpec((ATT_TQ, Q_PER_KV * HEAD_DIM),
                         lambda b, h, qi: (b * q_tiles + jnp.minimum(qi + 1, q_tiles - 1), h)),
            pl.BlockSpec((n_lat, HEAD_DIM), lambda b, h, qi: (b, kcol + h)),
            pl.BlockSpec((n_lat, HEAD_DIM), lambda b, h, qi: (b, vcol + h)),
            pl.BlockSpec((n_ctx, HEAD_DIM), lambda b, h, qi: (b, h)),
            pl.BlockSpec((n_ctx, HEAD_DIM), lambda b, h, qi: (b, N_KV_HEADS + h)),
        ],
        out_specs=pl.BlockSpec((ATT_TQ, Q_PER_KV * HEAD_DIM), lambda b, h, qi: (b * q_tiles + qi, h)),
        out_shape=jax.ShapeDtypeStruct((batch * n_lat, ATTN_WIDTH), BF16),
        scratch_shapes=[pltpu.VMEM((n_keys, HEAD_DIM), BF16), pltpu.VMEM((HEAD_DIM, n_keys), BF16),
                        pltpu.VMEM((2, n_keys, ATT_SUB), F32), pltpu.VMEM((1, ATT_SUB), F32)],
        compiler_params=_cparams(("arbitrary", "arbitrary", "arbitrary")),
        name="attn",
    )(p, p, p, p, pc, pc)


FOU_TM = 512
FOU_TK = 1024


def _fourier_kernel(cs_ref, u_ref, chan_ref, f_ref, acc_ref):
    kk = pl.program_id(2)

    @pl.when(kk == 0)
    def _():
        acc_ref[...] = jnp.zeros(acc_ref.shape, F32)

    lhs = cs_ref[...].reshape(2 * FOU_TM, FOU_TK)
    acc_ref[...] += _dot(lhs, u_ref[...])

    @pl.when(kk == pl.num_programs(2) - 1)
    def _():
        for g in range(N_FOURIER_GROUPS):
            sl = slice(g * FOURIER_GROUP_DIM, (g + 1) * FOURIER_GROUP_DIM)
            gc = acc_ref[0:FOU_TM, sl].astype(BF16)
            gs = acc_ref[FOU_TM:, sl].astype(BF16)
            f_ref[:, sl] = (_dot(gc, chan_ref[0]) - _dot(gs, chan_ref[1])).astype(BF16)


def _fourier(cs_pos, p, cs_chan, batch, n_lat):
    ucol = V_END // FOURIER_WIDTH
    m_tiles = n_lat // FOU_TM
    k_tiles = n_lat // FOU_TK
    return pl.pallas_call(
        _fourier_kernel,
        grid=(batch, m_tiles, k_tiles),
        in_specs=[
            pl.BlockSpec((2, FOU_TM, FOU_TK), lambda b, i, kk: (0, i, kk)),
            pl.BlockSpec((FOU_TK, FOURIER_WIDTH), lambda b, i, kk: (b * k_tiles + kk, ucol)),
            pl.BlockSpec(cs_chan.shape, lambda b, i, kk: (0, 0, 0)),
        ],
        out_specs=pl.BlockSpec((FOU_TM, FOURIER_WIDTH), lambda b, i, kk: (b * m_tiles + i, 0)),
        out_shape=jax.ShapeDtypeStruct((batch * n_lat, FOURIER_WIDTH), BF16),
        scratch_shapes=[pltpu.VMEM((2 * FOU_TM, FOURIER_WIDTH), F32)],
        compiler_params=_cparams(("arbitrary", "arbitrary", "arbitrary")),
        name="fourier",
    )(cs_pos, p, cs_chan)


MRG_TM = 256


def _merge_kernel(o_ref, f_ref, gates_ref, x_ref, mod_ref, wa_ref, wf_ref, wo_ref, x1_ref):
    a = _dot(o_ref[...], wa_ref[...])
    fm = _dot(f_ref[...], wf_ref[...])
    z = gates_ref[:, 0:D_MODEL].astype(F32) * a + gates_ref[:, D_MODEL:].astype(F32) * fm
    y = _dot(z.astype(BF16), wo_ref[...])
    x1_ref[...] = x_ref[...] + mod_ref[0, 2:3, :] * y


def _merge(o, f, p, x2, mod3, wa, wf, wo, n_lat):
    rows = x2.shape[0]
    tiles_per_batch = n_lat // MRG_TM
    const = lambda i: (0, 0)
    single = pl.Buffered(1)
    return pl.pallas_call(
        _merge_kernel,
        grid=(rows // MRG_TM,),
        in_specs=[
            pl.BlockSpec((MRG_TM, ATTN_WIDTH), lambda i: (i, 0)),
            pl.BlockSpec((MRG_TM, FOURIER_WIDTH), lambda i: (i, 0)),
            pl.BlockSpec((MRG_TM, 2 * D_MODEL), lambda i: (i, F_END // (2 * D_MODEL))),
            pl.BlockSpec((MRG_TM, D_MODEL), lambda i: (i, 0)),
            pl.BlockSpec((1, N_MOD, D_MODEL), lambda i: (i // tiles_per_batch, 0, 0)),
            pl.BlockSpec(wa.shape, const, pipeline_mode=single),
            pl.BlockSpec(wf.shape, const, pipeline_mode=single),
            pl.BlockSpec(wo.shape, const, pipeline_mode=single),
        ],
        out_specs=pl.BlockSpec((MRG_TM, D_MODEL), lambda i: (i, 0)),
        out_shape=jax.ShapeDtypeStruct((rows, D_MODEL), F32),
        compiler_params=_cparams(("arbitrary",)),
        name="merge",
    )(o, f, p, x2, mod3, wa, wf, wo)


MLP_TM = 512
MLP_TF = 1024


def _mlp_kernel(x1_ref, mod_ref, n2g_ref, w1_ref, w2_ref, fg_ref, out_ref, h_ref, acc_ref):
    kf = pl.program_id(1)

    @pl.when(kf == 0)
    def _():
        xn = _rmsnorm_rows(x1_ref[...], n2g_ref[...])
        h = xn * (1.0 + mod_ref[0, 4:5, :]) + mod_ref[0, 3:4, :]
        h_ref[...] = h.astype(BF16)
        acc_ref[...] = jnp.zeros(acc_ref.shape, F32)

    hid = jnp.maximum(_dot(h_ref[...], w1_ref[...]), 0.0)
    acc_ref[...] += _dot((hid * hid).astype(BF16), w2_ref[...])

    @pl.when(kf == pl.num_programs(1) - 1)
    def _():
        x2 = x1_ref[...] + mod_ref[0, 5:6, :] * acc_ref[...]
        out_ref[...] = _rmsnorm_rows(x2, fg_ref[...])


def _mlp(x1, mod3, n2g, w1, w2, fg, n_lat):
    rows = x1.shape[0]
    tiles_per_batch = n_lat // MLP_TM
    return pl.pallas_call(
        _mlp_kernel,
        grid=(rows // MLP_TM, D_FF // MLP_TF),
        in_specs=[
            pl.BlockSpec((MLP_TM, D_MODEL), lambda i, kf: (i, 0)),
            pl.BlockSpec((1, N_MOD, D_MODEL), lambda i, kf: (i // tiles_per_batch, 0, 0)),
            pl.BlockSpec((1, D_MODEL), lambda i, kf: (0, 0)),
            pl.BlockSpec((D_MODEL, MLP_TF), lambda i, kf: (0, kf)),
            pl.BlockSpec((MLP_TF, D_MODEL), lambda i, kf: (kf, 0)),
            pl.BlockSpec((1, D_MODEL), lambda i, kf: (0, 0)),
        ],
        out_specs=pl.BlockSpec((MLP_TM, D_MODEL), lambda i, kf: (i, 0)),
        out_shape=jax.ShapeDtypeStruct((rows, D_MODEL), F32),
        scratch_shapes=[pltpu.VMEM((MLP_TM, D_MODEL), BF16), pltpu.VMEM((MLP_TM, D_MODEL), F32)],
        compiler_params=_cparams(("arbitrary", "arbitrary")),
        name="mlp",
    )(x1, mod3, n2g, w1, w2, fg)


def _rope_tables(rows):
    lane = jnp.arange(INP_SUB, dtype=jnp.int32)[None, :] % HEAD_DIM
    pos = jnp.arange(rows * GRID_W, dtype=jnp.int32)[:, None]
    pair = lane // 2
    inv = 1.0 / (ROPE_THETA ** ((pair % ROPE_AXIS_PAIRS).astype(F32) / ROPE_AXIS_PAIRS))
    coord = jnp.where(pair < ROPE_AXIS_PAIRS, pos // GRID_W, pos % GRID_W).astype(F32)
    ang = coord * inv
    return jnp.stack([jnp.cos(ang), jnp.sin(ang)], axis=0)


def _dft_tables(n, scale):
    r = math.isqrt(n)
    assert r * r == n
    idx = jnp.arange(n, dtype=jnp.int32)
    sub = jnp.arange(r, dtype=jnp.int32)

    def cos_sin(mult):
        ang = ((mult[:, None] * idx[None, :]) % n).astype(F32) * (2.0 * math.pi / n)
        return jnp.cos(ang), jnp.sin(ang)

    c_hi, s_hi = cos_sin(sub * r)
    c_lo, s_lo = cos_sin(sub)
    c_hi, s_hi = c_hi[:, None, :] * scale, s_hi[:, None, :] * scale
    c_lo, s_lo = c_lo[None, :, :], s_lo[None, :, :]
    cos = (c_hi * c_lo - s_hi * s_lo).reshape(n, n)
    sin = (s_hi * c_lo + c_hi * s_lo).reshape(n, n)
    return jnp.stack([cos, sin], axis=0).astype(BF16)


def kernel(x, c, ctx, c_ctx, w_ada, b_ada, norm1_g, w_in, q_norm_g, k_norm_g, w_attn_o, w_fourier,
           w_out, norm2_g, w1, w2, final_norm_g):
    batch, n_lat, _ = x.shape
    n_ctx = ctx.shape[1]
    assert w_ada.shape[0] == 1, "single-layer block"
    assert n_lat % INP_TM == 0 and n_ctx % SUBLANES == 0

    cc = jnp.concatenate([c, c_ctx[None, :]], axis=0)
    cb = jnp.broadcast_to(cc[:, :, None], (batch + 1, D_MODEL, LANES))
    mod = _adaln(cb, w_ada[0], b_ada)
    mod3 = mod.reshape(SUBLANES, N_MOD, D_MODEL)

    w_in_bf = w_in[0].astype(BF16)
    n1g = norm1_g
    heads_per_sub = INP_SUB // HEAD_DIM
    qg, kg = jnp.tile(q_norm_g, (1, heads_per_sub)), jnp.tile(k_norm_g, (1, heads_per_sub))
    rope = _rope_tables(n_lat // GRID_W)

    x2 = x.reshape(batch * n_lat, D_MODEL)
    tiles_per_batch = n_lat // INP_TM
    p = _inproj(x2, mod3, lambda i: i // tiles_per_batch, n1g, w_in_bf, rope, qg, kg,
                tm=INP_TM, col_block0=0, n_col_blocks=IN_WIDTH // INP_TN, use_rope=True,
                rope_tiles=tiles_per_batch, name="inproj")
    ctx2 = ctx.reshape(batch * n_ctx, D_MODEL)
    pc = _inproj(ctx2, mod3, lambda i: batch, n1g, w_in_bf, rope, qg, kg,
                 tm=batch * n_ctx, col_block0=Q_END // INP_TN, n_col_blocks=(V_END - Q_END) // INP_TN,
                 use_rope=False, rope_tiles=1, name="ctxproj")

    o = _attention(p, pc, batch, n_lat, n_ctx)

    cs_pos = _dft_tables(n_lat, n_lat ** -0.5)
    cs_chan = _dft_tables(FOURIER_GROUP_DIM, FOURIER_GROUP_DIM ** -0.5)
    f = _fourier(cs_pos, p, cs_chan, batch, n_lat)

    x1 = _merge(o, f, p, x2, mod3, w_attn_o[0].astype(BF16), w_fourier[0].astype(BF16),
                w_out[0].astype(BF16), n_lat)
    out = _mlp(x1, mod3, norm2_g, w1[0].astype(BF16), w2[0].astype(BF16),
               final_norm_g[None, :], n_lat)
    return out.reshape(batch, n_lat, D_MODEL)
```

```python
import functools
import math

import jax
import jax.numpy as jnp
import numpy as np
from jax import lax
from jax.experimental import pallas as pl
from jax.experimental.pallas import tpu as pltpu

D_MODEL = 2048
GRID_W = 64
HEAD_DIM = 128
N_Q_HEADS = 16
N_KV_HEADS = 4
Q_PER_KV = N_Q_HEADS // N_KV_HEADS
ATTN_WIDTH = N_Q_HEADS * HEAD_DIM
KV_WIDTH = N_KV_HEADS * HEAD_DIM
N_FOURIER_GROUPS = 4
FOURIER_GROUP_DIM = 256
FOURIER_WIDTH = N_FOURIER_GROUPS * FOURIER_GROUP_DIM
Q_END = ATTN_WIDTH
K_END = Q_END + KV_WIDTH
V_END = K_END + KV_WIDTH
F_END = V_END + FOURIER_WIDTH
IN_WIDTH = F_END + 2 * D_MODEL
D_FF = 4 * D_MODEL
N_MOD = 6
ROPE_THETA = 10000.0
ROPE_AXIS_PAIRS = HEAD_DIM // 4
EPS = 1e-6

LANES = 128
SUBLANES = 8
VMEM_LIMIT = 56 * 1024 * 1024

BF16 = jnp.bfloat16
F32 = jnp.float32


def _cparams(sem):
    return pltpu.CompilerParams(dimension_semantics=sem, vmem_limit_bytes=VMEM_LIMIT)


def _dot(a, b):
    return jnp.dot(a, b, preferred_element_type=F32)


def _rmsnorm_rows(x, g):
    ms = jnp.mean(x * x, axis=-1, keepdims=True)
    return x * lax.rsqrt(ms + EPS) * g


ADA_TN = 512


def _adaln_kernel(cb_ref, w_ref, b_ref, out_ref):
    n_rows = cb_ref.shape[0]
    acts = []
    for r in range(n_rows):
        cr = cb_ref[r]
        acts.append(cr * jax.nn.sigmoid(cr))
    out_ref[...] = jnp.zeros(out_ref.shape, F32)
    for t in range(ADA_TN // LANES):
        wt = w_ref[:, t * LANES:(t + 1) * LANES]
        for r in range(n_rows):
            prod = wt * acts[r]
            red = prod.reshape(D_MODEL // SUBLANES, SUBLANES, LANES).sum(axis=0)
            tot = red.sum(axis=0, keepdims=True) + b_ref[:, t * LANES:(t + 1) * LANES]
            out_ref[r:r + 1, t * LANES:(t + 1) * LANES] = tot


def _adaln(cb, w_ada, b_ada):
    n_out = w_ada.shape[1]
    return pl.pallas_call(
        _adaln_kernel,
        grid=(n_out // ADA_TN,),
        in_specs=[
            pl.BlockSpec(cb.shape, lambda j: (0, 0, 0)),
            pl.BlockSpec((D_MODEL, ADA_TN), lambda j: (0, j)),
            pl.BlockSpec((1, ADA_TN), lambda j: (0, j)),
        ],
        out_specs=pl.BlockSpec((SUBLANES, ADA_TN), lambda j: (0, j)),
        out_shape=jax.ShapeDtypeStruct((SUBLANES, n_out), F32),
        compiler_params=_cparams(("arbitrary",)),
        name="adaln",
    )(cb, w_ada, b_ada)


INP_TM = 1024
INP_TN = 1024
INP_SUB = 256
Q_SCALE = HEAD_DIM ** -0.5 * math.log2(math.e)


def _head_norm_rope(acc, g, sel_ref, rope_ref, mult):
    xg = acc * g
    ssq = _dot((acc * acc).astype(BF16), sel_ref[0])
    r = lax.rsqrt(ssq * (1.0 / HEAD_DIM) + EPS) * mult
    if rope_ref is None:
        return xg * r
    swapped = _dot(xg.astype(BF16), sel_ref[1])
    return (xg * rope_ref[0] + swapped * rope_ref[1]) * r


def _col_kind(col):
    if col < Q_END:
        return "q"
    if col < K_END:
        return "k"
    if col < F_END:
        return "plain"
    return "gate"


def _inproj_kernel(x_ref, mod_ref, n1g_ref, w_ref, rope_ref, qg_ref, kg_ref, sel_ref, out_ref, h_ref,
                   *, col_block0, n_col_blocks, use_rope):
    j = pl.program_id(1)

    @pl.when(j == 0)
    def _():
        xn = _rmsnorm_rows(x_ref[...], n1g_ref[...])
        h = xn * (1.0 + mod_ref[0, 1:2, :]) + mod_ref[0, 0:1, :]
        h_ref[...] = h.astype(BF16)

    def epilogue(acc, kind, c0):
        if kind in ("q", "k"):
            g_ref, mult = (qg_ref, Q_SCALE) if kind == "q" else (kg_ref, 1.0)
            res = _head_norm_rope(acc, g_ref[...], sel_ref, rope_ref if use_rope else None, mult)
            out_ref[:, c0:c0 + INP_SUB] = res.astype(BF16)
        elif kind == "plain":
            out_ref[:, c0:c0 + INP_SUB] = acc.astype(BF16)
        else:
            out_ref[:, c0:c0 + INP_SUB] = jax.nn.sigmoid(acc).astype(BF16)

    def tile(col_block):
        subs = range(0, INP_TN, INP_SUB)
        pending = None
        for c0 in subs:
            acc = _dot(h_ref[...], w_ref[:, c0:c0 + INP_SUB])
            if pending is not None:
                epilogue(*pending)
            pending = (acc, _col_kind(col_block * INP_TN + c0), c0)
        epilogue(*pending)

    def kinds_of(cb):
        return tuple(_col_kind(cb * INP_TN + c0) for c0 in range(0, INP_TN, INP_SUB))

    lo = 0
    while lo < n_col_blocks:
        hi = lo
        while hi + 1 < n_col_blocks and kinds_of(col_block0 + hi + 1) == kinds_of(col_block0 + lo):
            hi += 1
        pl.when(jnp.logical_and(j >= lo, j <= hi))(functools.partial(tile, col_block0 + lo))
        lo = hi + 1


def _head_select_matrices():
    idx = np.arange(INP_SUB)
    ones_bd = (idx[:, None] // HEAD_DIM == idx[None, :] // HEAD_DIM).astype(np.float32)
    swap = np.zeros((INP_SUB, INP_SUB), np.float32)
    swap[idx[1::2], idx[0::2]] = -1.0
    swap[idx[0::2], idx[1::2]] = 1.0
    return jnp.asarray(np.stack([ones_bd, swap]), dtype=BF16)


def _inproj(x2, mod3, mod_row_of_tile, n1g, w_bf, rope, qg, kg, *, tm, col_block0, n_col_blocks,
            use_rope, rope_tiles, name):
    rows = x2.shape[0]
    sel = _head_select_matrices()
    kern = functools.partial(_inproj_kernel, col_block0=col_block0, n_col_blocks=n_col_blocks,
                             use_rope=use_rope)
    return pl.pallas_call(
        kern,
        grid=(rows // tm, n_col_blocks),
        in_specs=[
            pl.BlockSpec((tm, D_MODEL), lambda i, j: (i, 0)),
            pl.BlockSpec((1, N_MOD, D_MODEL), lambda i, j: (mod_row_of_tile(i), 0, 0)),
            pl.BlockSpec((1, D_MODEL), lambda i, j: (0, 0)),
            pl.BlockSpec((D_MODEL, INP_TN), lambda i, j: (0, j + col_block0)),
            pl.BlockSpec((2, tm, INP_SUB), lambda i, j: (0, i % rope_tiles, 0)),
            pl.BlockSpec((1, INP_SUB), lambda i, j: (0, 0)),
            pl.BlockSpec((1, INP_SUB), lambda i, j: (0, 0)),
            pl.BlockSpec(sel.shape, lambda i, j: (0, 0, 0)),
        ],
        out_specs=pl.BlockSpec((tm, INP_TN), lambda i, j: (i, j)),
        out_shape=jax.ShapeDtypeStruct((rows, n_col_blocks * INP_TN), BF16),
        scratch_shapes=[pltpu.VMEM((tm, D_MODEL), BF16)],
        compiler_params=_cparams(("arbitrary", "arbitrary")),
        name=name,
    )(x2, mod3, n1g, w_bf, rope, qg, kg, sel)


ATT_SUB = 256
ATT_NSUB = 2
ATT_TQ = ATT_SUB * ATT_NSUB
ATT_CHUNK = 1024


def _key_chunks(n_keys):
    chunks = [(c0, ATT_CHUNK) for c0 in range(0, n_keys - ATT_CHUNK + 1, ATT_CHUNK)]
    done = len(chunks) * ATT_CHUNK
    if done < n_keys:
        chunks.append((done, n_keys - done))
    return chunks


def _attn_kernel(q_ref, qn_ref, k_ref, v_ref, kc_ref, vc_ref, o_ref, kall_ref, vt_ref, s_ref, m_ref):
    qi = pl.program_id(2)
    n_lat = k_ref.shape[0]
    n_keys = kall_ref.shape[0]

    units = [(sub, g) for sub in range(ATT_NSUB) for g in range(Q_PER_KV)]
    assert len(units) % 2 == 0
    chunks = _key_chunks(n_keys)

    def q_of(ref, u):
        sub, g = units[u]
        return ref[sub * ATT_SUB:(sub + 1) * ATT_SUB, g * HEAD_DIM:(g + 1) * HEAD_DIM]

    def scores_chunk(q_u, slot, c0, cs, m_run):
        s_c = lax.dot_general(kall_ref[c0:c0 + cs, :], q_u, (((1,), (1,)), ((), ())),
                              preferred_element_type=F32)
        s_ref[slot, c0:c0 + cs, :] = s_c
        mc = jnp.max(s_c, axis=0, keepdims=True)
        return mc if m_run is None else jnp.maximum(m_run, mc)

    @pl.when(qi == 0)
    def _():
        kall_ref[0:n_lat, :] = k_ref[...]
        kall_ref[n_lat:, :] = kc_ref[...]
        vt_ref[:, 0:n_lat] = v_ref[...].astype(F32).T.astype(BF16)
        vt_ref[:, n_lat:] = vc_ref[...].astype(F32).T.astype(BF16)
        m0 = None
        for c0, cs in chunks:
            m0 = scores_chunk(q_of(q_ref, 0), 0, c0, cs, m0)
        m_ref[...] = m0

    m_cur = m_ref[...]
    for t in range(len(units)):
        last = t + 1 == len(units)
        q_nxt = q_of(qn_ref, 0) if last else q_of(q_ref, t + 1)
        m_nxt = l = o_acc = None
        for c0, cs in chunks:
            m_nxt = scores_chunk(q_nxt, (t + 1) % 2, c0, cs, m_nxt)
            p_c = jnp.exp2(s_ref[t % 2, c0:c0 + cs, :] - m_cur)
            lc = jnp.sum(p_c, axis=0, keepdims=True)
            l = lc if l is None else l + lc
            pv = _dot(vt_ref[:, c0:c0 + cs], p_c.astype(BF16))
            o_acc = pv if o_acc is None else o_acc + pv
        sub, g = units[t]
        o_ref[sub * ATT_SUB:(sub + 1) * ATT_SUB, g * HEAD_DIM:(g + 1) * HEAD_DIM] = (
            (o_acc * (1.0 / l)).T.astype(BF16))
        m_cur = m_nxt
    m_ref[...] = m_cur


def _attention(p, pc, batch, n_lat, n_ctx):
    q_tiles = n_lat // ATT_TQ
    kcol = Q_END // HEAD_DIM
    vcol = K_END // HEAD_DIM
    n_keys = n_lat + n_ctx
    return pl.pallas_call(
        _attn_kernel,
        grid=(batch, N_KV_HEADS, q_tiles),
        in_specs=[
            pl.BlockSpec((ATT_TQ, Q_PER_KV * HEAD_DIM), lambda b, h, qi: (b * q_tiles + qi, h)),
            pl.BlockSpec((ATT_TQ, Q_PER_KV * HEAD_DIM),
                         lambda b, h, qi: (b * q_tiles + jnp.minimum(qi + 1, q_tiles - 1), h)),
            pl.BlockSpec((n_lat, HEAD_DIM), lambda b, h, qi: (b, kcol + h)),
            pl.BlockSpec((n_lat, HEAD_DIM), lambda b, h, qi: (b, vcol + h)),
            pl.BlockSpec((n_ctx, HEAD_DIM), lambda b, h, qi: (b, h)),
            pl.BlockSpec((n_ctx, HEAD_DIM), lambda b, h, qi: (b, N_KV_HEADS + h)),
        ],
        out_specs=pl.BlockSpec((ATT_TQ, Q_PER_KV * HEAD_DIM), lambda b, h, qi: (b * q_tiles + qi, h)),
        out_shape=jax.ShapeDtypeStruct((batch * n_lat, ATTN_WIDTH), BF16),
        scratch_shapes=[pltpu.VMEM((n_keys, HEAD_DIM), BF16), pltpu.VMEM((HEAD_DIM, n_keys), BF16),
                        pltpu.VMEM((2, n_keys, ATT_SUB), F32), pltpu.VMEM((1, ATT_SUB), F32)],
        compiler_params=_cparams(("arbitrary", "arbitrary", "arbitrary")),
        name="attn",
    )(p, p, p, p, pc, pc)


FOU_T = 256


def _fourier_kernel(u_ref, ct_ref, st_ref, cc_ref, sc_ref, jrev_ref, alt_ref, f_ref,
                    e_ref, o_ref, carry_ref, *, n_lat):
    s = pl.program_id(1)
    half = n_lat // 2
    tiles = half // FOU_T
    row = lax.broadcasted_iota(jnp.int32, (FOU_T, FOURIER_WIDTH), 0)

    def chan_dft(g_mat, h_mat):
        fd, fm = [], []
        for g in range(N_FOURIER_GROUPS):
            sl = slice(g * FOURIER_GROUP_DIM, (g + 1) * FOURIER_GROUP_DIM)
            gc = _dot(g_mat[:, sl].astype(BF16), cc_ref[...])
            if h_mat is None:
                fd.append(gc)
                continue
            hs = _dot(h_mat[:, sl].astype(BF16), sc_ref[...])
            fd.append(gc - hs)
            fm.append(gc + hs)
        return jnp.concatenate(fd, axis=1), (jnp.concatenate(fm, axis=1) if fm else None)

    @pl.when(s == 0)
    def _():
        for t in range(tiles):
            lo = u_ref[t * FOU_T:(t + 1) * FOU_T, :].astype(F32)
            hi_tile = u_ref[(2 * tiles - 1 - t) * FOU_T:(2 * tiles - t) * FOU_T, :]
            part = _dot(jrev_ref[...], hi_tile)
            if t == 0:
                first = jnp.zeros((1, FOURIER_WIDTH), F32)
            else:
                r0 = (2 * tiles - t) * FOU_T
                first = u_ref[r0:r0 + 1, :].astype(F32)
            part = jnp.where(row == 0, first, part)
            e_ref[t * FOU_T:(t + 1) * FOU_T, :] = (lo + part).astype(BF16)
            o_ref[t * FOU_T:(t + 1) * FOU_T, :] = (lo - part).astype(BF16)
        g_nyq = _dot(alt_ref[...], u_ref[...])
        f_nyq, _ = chan_dft(g_nyq, None)
        carry_ref[...] = f_nyq

    scale = n_lat ** -0.5
    u_half = u_ref[half:half + 1, :].astype(F32) * scale
    g_mat = _dot(ct_ref[...], e_ref[...]) + jnp.where(row % 2 == 0, u_half, -u_half)
    h_mat = _dot(st_ref[...], o_ref[...])
    fd, fm = chan_dft(g_mat, h_mat)
    f_ref[0, 0, 0] = fd.astype(BF16)
    fm_bf = fm.astype(BF16)
    mirrored = jnp.where(row == 0, carry_ref[0:1, :], _dot(jrev_ref[...], fm_bf))
    f_ref[0, 1, 0] = mirrored.astype(BF16)
    carry_ref[0:1, :] = fm_bf[0:1, :].astype(F32)


def _fourier(p, tables, batch, n_lat):
    ct, st, cc, sc, jrev, alt = tables
    ucol = V_END // FOURIER_WIDTH
    tiles = n_lat // 2 // FOU_T
    const2 = lambda b, s: (0, 0)
    return pl.pallas_call(
        functools.partial(_fourier_kernel, n_lat=n_lat),
        grid=(batch, tiles),
        in_specs=[
            pl.BlockSpec((n_lat, FOURIER_WIDTH), lambda b, s: (b, ucol)),
            pl.BlockSpec((FOU_T, n_lat // 2), lambda b, s: (tiles - 1 - s, 0)),
            pl.BlockSpec((FOU_T, n_lat // 2), lambda b, s: (tiles - 1 - s, 0)),
            pl.BlockSpec(cc.shape, const2),
            pl.BlockSpec(sc.shape, const2),
            pl.BlockSpec(jrev.shape, const2),
            pl.BlockSpec(alt.shape, const2),
        ],
        out_specs=pl.BlockSpec((1, 2, 1, FOU_T, FOURIER_WIDTH), lambda b, s: (b, 0, tiles - 1 - s, 0, 0)),
        out_shape=jax.ShapeDtypeStruct((batch, 2, tiles, FOU_T, FOURIER_WIDTH), BF16),
        scratch_shapes=[pltpu.VMEM((n_lat // 2, FOURIER_WIDTH), BF16),
                        pltpu.VMEM((n_lat // 2, FOURIER_WIDTH), BF16),
                        pltpu.VMEM((SUBLANES, FOURIER_WIDTH), F32)],
        compiler_params=_cparams(("arbitrary", "arbitrary")),
        name="fourier",
    )(p, ct, st, cc, sc, jrev, alt)


MRG_TM = FOU_T


def _merge_kernel(o_ref, f_ref, gates_ref, x_ref, mod_ref, wa_ref, wf_ref, wo_ref, x1_ref):
    a = _dot(o_ref[...], wa_ref[...])
    fm = _dot(f_ref[0, 0, 0], wf_ref[...])
    z = gates_ref[:, 0:D_MODEL].astype(F32) * a + gates_ref[:, D_MODEL:].astype(F32) * fm
    y = _dot(z.astype(BF16), wo_ref[...])
    x1_ref[...] = x_ref[...] + mod_ref[0, 2:3, :] * y


def _merge(o, f, p, x2, mod3, wa, wf, wo, n_lat):
    rows = x2.shape[0]
    tiles_per_batch = n_lat // MRG_TM
    const = lambda i: (0, 0)
    single = pl.Buffered(1)

    def f_index(i):
        t = i % tiles_per_batch
        upper = t // (tiles_per_batch // 2)
        return (i // tiles_per_batch, upper, jnp.where(upper == 1, tiles_per_batch - 1 - t, t), 0, 0)

    return pl.pallas_call(
        _merge_kernel,
        grid=(rows // MRG_TM,),
        in_specs=[
            pl.BlockSpec((MRG_TM, ATTN_WIDTH), lambda i: (i, 0)),
            pl.BlockSpec((1, 1, 1, MRG_TM, FOURIER_WIDTH), f_index),
            pl.BlockSpec((MRG_TM, 2 * D_MODEL), lambda i: (i, F_END // (2 * D_MODEL))),
            pl.BlockSpec((MRG_TM, D_MODEL), lambda i: (i, 0)),
            pl.BlockSpec((1, N_MOD, D_MODEL), lambda i: (i // tiles_per_batch, 0, 0)),
            pl.BlockSpec(wa.shape, const, pipeline_mode=single),
            pl.BlockSpec(wf.shape, const, pipeline_mode=single),
            pl.BlockSpec(wo.shape, const, pipeline_mode=single),
        ],
        out_specs=pl.BlockSpec((MRG_TM, D_MODEL), lambda i: (i, 0)),
        out_shape=jax.ShapeDtypeStruct((rows, D_MODEL), F32),
        compiler_params=_cparams(("arbitrary",)),
        name="merge",
    )(o, f, p, x2, mod3, wa, wf, wo)


MLP_TM = 512
MLP_TF = 1024


def _mlp_kernel(x1_ref, mod_ref, n2g_ref, w1_ref, w2_ref, fg_ref, out_ref, h_ref, acc_ref):
    kf = pl.program_id(1)

    @pl.when(kf == 0)
    def _():
        xn = _rmsnorm_rows(x1_ref[...], n2g_ref[...])
        h = xn * (1.0 + mod_ref[0, 4:5, :]) + mod_ref[0, 3:4, :]
        h_ref[...] = h.astype(BF16)
        acc_ref[...] = jnp.zeros(acc_ref.shape, F32)

    hid = jnp.maximum(_dot(h_ref[...], w1_ref[...]), 0.0)
    acc_ref[...] += _dot((hid * hid).astype(BF16), w2_ref[...])

    @pl.when(kf == pl.num_programs(1) - 1)
    def _():
        x2 = x1_ref[...] + mod_ref[0, 5:6, :] * acc_ref[...]
        out_ref[...] = _rmsnorm_rows(x2, fg_ref[...])


def _mlp(x1, mod3, n2g, w1, w2, fg, n_lat):
    rows = x1.shape[0]
    tiles_per_batch = n_lat // MLP_TM
    return pl.pallas_call(
        _mlp_kernel,
        grid=(rows // MLP_TM, D_FF // MLP_TF),
        in_specs=[
            pl.BlockSpec((MLP_TM, D_MODEL), lambda i, kf: (i, 0)),
            pl.BlockSpec((1, N_MOD, D_MODEL), lambda i, kf: (i // tiles_per_batch, 0, 0)),
            pl.BlockSpec((1, D_MODEL), lambda i, kf: (0, 0)),
            pl.BlockSpec((D_MODEL, MLP_TF), lambda i, kf: (0, kf)),
            pl.BlockSpec((MLP_TF, D_MODEL), lambda i, kf: (kf, 0)),
            pl.BlockSpec((1, D_MODEL), lambda i, kf: (0, 0)),
        ],
        out_specs=pl.BlockSpec((MLP_TM, D_MODEL), lambda i, kf: (i, 0)),
        out_shape=jax.ShapeDtypeStruct((rows, D_MODEL), F32),
        scratch_shapes=[pltpu.VMEM((MLP_TM, D_MODEL), BF16), pltpu.VMEM((MLP_TM, D_MODEL), F32)],
        compiler_params=_cparams(("arbitrary", "arbitrary")),
        name="mlp",
    )(x1, mod3, n2g, w1, w2, fg)


def _rope_tables(rows):
    lane = jnp.arange(INP_SUB, dtype=jnp.int32)[None, :] % HEAD_DIM
    pos = jnp.arange(rows * GRID_W, dtype=jnp.int32)[:, None]
    pair = lane // 2
    inv = 1.0 / (ROPE_THETA ** ((pair % ROPE_AXIS_PAIRS).astype(F32) / ROPE_AXIS_PAIRS))
    coord = jnp.where(pair < ROPE_AXIS_PAIRS, pos // GRID_W, pos % GRID_W).astype(F32)
    ang = coord * inv
    return jnp.stack([jnp.cos(ang), jnp.sin(ang)], axis=0)


def _dft_tables(size, period, scale):
    n_lo = 1 << ((size.bit_length() - 1) // 2)
    n_hi = size // n_lo
    assert n_lo * n_hi == size
    idx = jnp.arange(size, dtype=jnp.int32)

    def cos_sin(mult):
        ang = ((mult[:, None] * idx[None, :]) % period).astype(F32) * (2.0 * math.pi / period)
        return jnp.cos(ang), jnp.sin(ang)

    c_hi, s_hi = cos_sin(jnp.arange(n_hi, dtype=jnp.int32) * n_lo)
    c_lo, s_lo = cos_sin(jnp.arange(n_lo, dtype=jnp.int32))
    c_hi, s_hi = c_hi[:, None, :] * scale, s_hi[:, None, :] * scale
    c_lo, s_lo = c_lo[None, :, :], s_lo[None, :, :]
    cos = (c_hi * c_lo - s_hi * s_lo).reshape(size, size)
    sin = (s_hi * c_lo + c_hi * s_lo).reshape(size, size)
    return cos.astype(BF16), sin.astype(BF16)


def _fourier_tables(n_lat):
    half = n_lat // 2
    ct, st = _dft_tables(half, n_lat, n_lat ** -0.5)
    cc, sc = _dft_tables(FOURIER_GROUP_DIM, FOURIER_GROUP_DIM, FOURIER_GROUP_DIM ** -0.5)
    jrev = np.zeros((FOU_T, FOU_T), np.float32)
    jrev[np.arange(1, FOU_T), FOU_T - np.arange(1, FOU_T)] = 1.0
    alt = np.zeros((SUBLANES, n_lat), np.float32)
    alt[0] = (1.0 - 2.0 * (np.arange(n_lat) % 2)) * n_lat ** -0.5
    return ct, st, cc, sc, jnp.asarray(jrev, dtype=BF16), jnp.asarray(alt, dtype=BF16)


def kernel(x, c, ctx, c_ctx, w_ada, b_ada, norm1_g, w_in, q_norm_g, k_norm_g, w_attn_o, w_fourier,
           w_out, norm2_g, w1, w2, final_norm_g):
    batch, n_lat, _ = x.shape
    n_ctx = ctx.shape[1]
    assert w_ada.shape[0] == 1, "single-layer block"
    assert n_lat % INP_TM == 0 and n_ctx % SUBLANES == 0

    cc = jnp.concatenate([c, c_ctx[None, :]], axis=0)
    cb = jnp.broadcast_to(cc[:, :, None], (batch + 1, D_MODEL, LANES))
    mod = _adaln(cb, w_ada[0], b_ada)
    mod3 = mod.reshape(SUBLANES, N_MOD, D_MODEL)

    w_in_bf = w_in[0].astype(BF16)
    n1g = norm1_g
    heads_per_sub = INP_SUB // HEAD_DIM
    qg, kg = jnp.tile(q_norm_g, (1, heads_per_sub)), jnp.tile(k_norm_g, (1, heads_per_sub))
    rope = _rope_tables(n_lat // GRID_W)

    x2 = x.reshape(batch * n_lat, D_MODEL)
    tiles_per_batch = n_lat // INP_TM
    p = _inproj(x2, mod3, lambda i: i // tiles_per_batch, n1g, w_in_bf, rope, qg, kg,
                tm=INP_TM, col_block0=0, n_col_blocks=IN_WIDTH // INP_TN, use_rope=True,
                rope_tiles=tiles_per_batch, name="inproj")
    ctx2 = ctx.reshape(batch * n_ctx, D_MODEL)
    pc = _inproj(ctx2, mod3, lambda i: batch, n1g, w_in_bf, rope, qg, kg,
                 tm=batch * n_ctx, col_block0=Q_END // INP_TN, n_col_blocks=(V_END - Q_END) // INP_TN,
                 use_rope=False, rope_tiles=1, name="ctxproj")

    o = _attention(p, pc, batch, n_lat, n_ctx)

    f = _fourier(p, _fourier_tables(n_lat), batch, n_lat)

    x1 = _merge(o, f, p, x2, mod3, w_attn_o[0].astype(BF16), w_fourier[0].astype(BF16),
                w_out[0].astype(BF16), n_lat)
    out = _mlp(x1, mod3, norm2_g, w1[0].astype(BF16), w2[0].astype(BF16),
               final_norm_g[None, :], n_lat)
    return out.reshape(batch, n_lat, D_MODEL)
```

```python
import functools
import math

import jax
import jax.numpy as jnp
import numpy as np
from jax import lax
from jax.experimental import pallas as pl
from jax.experimental.pallas import tpu as pltpu

D_MODEL = 2048
GRID_W = 64
HEAD_DIM = 128
N_Q_HEADS = 16
N_KV_HEADS = 4
Q_PER_KV = N_Q_HEADS // N_KV_HEADS
ATTN_WIDTH = N_Q_HEADS * HEAD_DIM
KV_WIDTH = N_KV_HEADS * HEAD_DIM
N_FOURIER_GROUPS = 4
FOURIER_GROUP_DIM = 256
FOURIER_WIDTH = N_FOURIER_GROUPS * FOURIER_GROUP_DIM
Q_END = ATTN_WIDTH
K_END = Q_END + KV_WIDTH
V_END = K_END + KV_WIDTH
F_END = V_END + FOURIER_WIDTH
IN_WIDTH = F_END + 2 * D_MODEL
D_FF = 4 * D_MODEL
N_MOD = 6
ROPE_THETA = 10000.0
ROPE_AXIS_PAIRS = HEAD_DIM // 4
EPS = 1e-6

LANES = 128
SUBLANES = 8
VMEM_LIMIT = 56 * 1024 * 1024

BF16 = jnp.bfloat16
F32 = jnp.float32


def _cparams(sem):
    return pltpu.CompilerParams(dimension_semantics=sem, vmem_limit_bytes=VMEM_LIMIT)


def _dot(a, b):
    return jnp.dot(a, b, preferred_element_type=F32)


def _rmsnorm_rows(x, g):
    ms = jnp.mean(x * x, axis=-1, keepdims=True)
    return x * lax.rsqrt(ms + EPS) * g


ADA_TN = 1024


def _adaln_kernel(cb_ref, w_ref, b_ref, out_ref, act_ref):
    n_rows = cb_ref.shape[0]

    @pl.when(pl.program_id(0) == 0)
    def _():
        for r in range(n_rows):
            cr = cb_ref[r]
            act_ref[r] = cr * jax.nn.sigmoid(cr)

    out_ref[...] = jnp.zeros(out_ref.shape, F32)
    for t in range(ADA_TN // LANES):
        wt = w_ref[:, t * LANES:(t + 1) * LANES]
        for r in range(n_rows):
            prod = wt * act_ref[r]
            red = prod.reshape(D_MODEL // SUBLANES, SUBLANES, LANES).sum(axis=0)
            tot = red.sum(axis=0, keepdims=True) + b_ref[:, t * LANES:(t + 1) * LANES]
            out_ref[r:r + 1, t * LANES:(t + 1) * LANES] = tot


def _adaln(cb, w_ada, b_ada):
    n_out = w_ada.shape[1]
    return pl.pallas_call(
        _adaln_kernel,
        grid=(n_out // ADA_TN,),
        in_specs=[
            pl.BlockSpec(cb.shape, lambda j: (0, 0, 0)),
            pl.BlockSpec((D_MODEL, ADA_TN), lambda j: (0, j)),
            pl.BlockSpec((1, ADA_TN), lambda j: (0, j)),
        ],
        out_specs=pl.BlockSpec((SUBLANES, ADA_TN), lambda j: (0, j)),
        out_shape=jax.ShapeDtypeStruct((SUBLANES, n_out), F32),
        scratch_shapes=[pltpu.VMEM(cb.shape, F32)],
        compiler_params=_cparams(("arbitrary",)),
        name="adaln",
    )(cb, w_ada, b_ada)


INP_TM = 1024
INP_TN = 1024
INP_SUB = 256
Q_SCALE = HEAD_DIM ** -0.5 * math.log2(math.e)


def _head_norm_rope(acc, g, sel_ref, rope_ref, mult):
    xg = acc * g
    ssq = _dot((acc * acc).astype(BF16), sel_ref[0])
    r = lax.rsqrt(ssq * (1.0 / HEAD_DIM) + EPS) * mult
    if rope_ref is None:
        return xg * r
    swapped = _dot(xg.astype(BF16), sel_ref[1])
    return (xg * rope_ref[0] + swapped * rope_ref[1]) * r


def _col_kind(col):
    if col < Q_END:
        return "q"
    if col < K_END:
        return "k"
    if col < F_END:
        return "plain"
    return "gate"


def _inproj_kernel(x_ref, mod_ref, n1g_ref, w_ref, rope_ref, qg_ref, kg_ref, sel_ref, out_ref, h_ref,
                   *, col_block0, n_col_blocks, use_rope):
    j = pl.program_id(1)

    @pl.when(j == 0)
    def _():
        xn = _rmsnorm_rows(x_ref[...], n1g_ref[...])
        h = xn * (1.0 + mod_ref[0, 1:2, :]) + mod_ref[0, 0:1, :]
        h_ref[...] = h.astype(BF16)

    def epilogue(acc, kind, c0):
        if kind in ("q", "k"):
            g_ref, mult = (qg_ref, Q_SCALE) if kind == "q" else (kg_ref, 1.0)
            res = _head_norm_rope(acc, g_ref[...], sel_ref, rope_ref if use_rope else None, mult)
            out_ref[:, c0:c0 + INP_SUB] = res.astype(BF16)
        elif kind == "plain":
            out_ref[:, c0:c0 + INP_SUB] = acc.astype(BF16)
        else:
            out_ref[:, c0:c0 + INP_SUB] = jax.nn.sigmoid(acc).astype(BF16)

    def tile(col_block):
        subs = range(0, INP_TN, INP_SUB)
        pending = None
        for c0 in subs:
            acc = _dot(h_ref[...], w_ref[:, c0:c0 + INP_SUB].astype(BF16))
            if pending is not None:
                epilogue(*pending)
            pending = (acc, _col_kind(col_block * INP_TN + c0), c0)
        epilogue(*pending)

    def kinds_of(cb):
        return tuple(_col_kind(cb * INP_TN + c0) for c0 in range(0, INP_TN, INP_SUB))

    lo = 0
    while lo < n_col_blocks:
        hi = lo
        while hi + 1 < n_col_blocks and kinds_of(col_block0 + hi + 1) == kinds_of(col_block0 + lo):
            hi += 1
        pl.when(jnp.logical_and(j >= lo, j <= hi))(functools.partial(tile, col_block0 + lo))
        lo = hi + 1


def _head_select_matrices():
    idx = np.arange(INP_SUB)
    ones_bd = (idx[:, None] // HEAD_DIM == idx[None, :] // HEAD_DIM).astype(np.float32)
    swap = np.zeros((INP_SUB, INP_SUB), np.float32)
    swap[idx[1::2], idx[0::2]] = -1.0
    swap[idx[0::2], idx[1::2]] = 1.0
    return jnp.asarray(np.stack([ones_bd, swap]), dtype=BF16)


def _inproj(x2, mod3, mod_row_of_tile, n1g, w_bf, rope, qg, kg, *, tm, col_block0, n_col_blocks,
            use_rope, rope_tiles, name):
    rows = x2.shape[0]
    sel = _head_select_matrices()
    kern = functools.partial(_inproj_kernel, col_block0=col_block0, n_col_blocks=n_col_blocks,
                             use_rope=use_rope)
    return pl.pallas_call(
        kern,
        grid=(rows // tm, n_col_blocks),
        in_specs=[
            pl.BlockSpec((tm, D_MODEL), lambda i, j: (i, 0)),
            pl.BlockSpec((1, N_MOD, D_MODEL), lambda i, j: (mod_row_of_tile(i), 0, 0)),
            pl.BlockSpec((1, D_MODEL), lambda i, j: (0, 0)),
            pl.BlockSpec((D_MODEL, INP_TN), lambda i, j: (0, j + col_block0)),
            pl.BlockSpec((2, tm, INP_SUB), lambda i, j: (0, i % rope_tiles, 0)),
            pl.BlockSpec((1, INP_SUB), lambda i, j: (0, 0)),
            pl.BlockSpec((1, INP_SUB), lambda i, j: (0, 0)),
            pl.BlockSpec(sel.shape, lambda i, j: (0, 0, 0)),
        ],
        out_specs=pl.BlockSpec((tm, INP_TN), lambda i, j: (i, j)),
        out_shape=jax.ShapeDtypeStruct((rows, n_col_blocks * INP_TN), BF16),
        scratch_shapes=[pltpu.VMEM((tm, D_MODEL), BF16)],
        compiler_params=_cparams(("arbitrary", "arbitrary")),
        name=name,
    )(x2, mod3, n1g, w_bf, rope, qg, kg, sel)


ATT_SUB = 256
ATT_NSUB = 2
ATT_TQ = ATT_SUB * ATT_NSUB
ATT_CHUNK = 1024


def _key_chunks(n_keys):
    chunks = [(c0, ATT_CHUNK) for c0 in range(0, n_keys - ATT_CHUNK + 1, ATT_CHUNK)]
    done = len(chunks) * ATT_CHUNK
    if done < n_keys:
        chunks.append((done, n_keys - done))
    return chunks


def _attn_kernel(q_ref, qn_ref, k_ref, v_ref, kc_ref, vc_ref, o_ref, kall_ref, vt_ref, s_ref, m_ref):
    qi = pl.program_id(2)
    n_lat = k_ref.shape[0]
    n_keys = kall_ref.shape[0]

    units = [(sub, g) for sub in range(ATT_NSUB) for g in range(Q_PER_KV)]
    assert len(units) % 2 == 0
    chunks = _key_chunks(n_keys)

    def q_of(ref, u):
        sub, g = units[u]
        return ref[sub * ATT_SUB:(sub + 1) * ATT_SUB, g * HEAD_DIM:(g + 1) * HEAD_DIM]

    def scores_chunk(q_u, slot, c0, cs, m_run):
        s_c = lax.dot_general(kall_ref[c0:c0 + cs, :], q_u, (((1,), (1,)), ((), ())),
                              preferred_element_type=F32)
        s_ref[slot, c0:c0 + cs, :] = s_c
        mc = jnp.max(s_c, axis=0, keepdims=True)
        return mc if m_run is None else jnp.maximum(m_run, mc)

    @pl.when(qi == 0)
    def _():
        kall_ref[0:n_lat, :] = k_ref[...]
        kall_ref[n_lat:, :] = kc_ref[...]
        vt_ref[:, 0:n_lat] = v_ref[...].astype(F32).T.astype(BF16)
        vt_ref[:, n_lat:] = vc_ref[...].astype(F32).T.astype(BF16)
        m0 = None
        for c0, cs in chunks:
            m0 = scores_chunk(q_of(q_ref, 0), 0, c0, cs, m0)
        m_ref[...] = m0

    m_cur = m_ref[...]
    for t in range(len(units)):
        last = t + 1 == len(units)
        q_nxt = q_of(qn_ref, 0) if last else q_of(q_ref, t + 1)
        m_nxt = l = o_acc = None
        for c0, cs in chunks:
            m_nxt = scores_chunk(q_nxt, (t + 1) % 2, c0, cs, m_nxt)
            p_c = jnp.exp2(s_ref[t % 2, c0:c0 + cs, :] - m_cur)
            lc = jnp.sum(p_c, axis=0, keepdims=True)
            l = lc if l is None else l + lc
            pv = _dot(vt_ref[:, c0:c0 + cs], p_c.astype(BF16))
            o_acc = pv if o_acc is None else o_acc + pv
        sub, g = units[t]
        o_ref[sub * ATT_SUB:(sub + 1) * ATT_SUB, g * HEAD_DIM:(g + 1) * HEAD_DIM] = (
            (o_acc * (1.0 / l)).T.astype(BF16))
        m_cur = m_nxt
    m_ref[...] = m_cur


def _attention(p, pc, batch, n_lat, n_ctx):
    q_tiles = n_lat // ATT_TQ
    kcol = Q_END // HEAD_DIM
    vcol = K_END // HEAD_DIM
    n_keys = n_lat + n_ctx
    return pl.pallas_call(
        _attn_kernel,
        grid=(batch, N_KV_HEADS, q_tiles),
        in_specs=[
            pl.BlockSpec((ATT_TQ, Q_PER_KV * HEAD_DIM), lambda b, h, qi: (b * q_tiles + qi, h)),
            pl.BlockSpec((ATT_TQ, Q_PER_KV * HEAD_DIM),
                         lambda b, h, qi: (b * q_tiles + jnp.minimum(qi + 1, q_tiles - 1), h)),
            pl.BlockSpec((n_lat, HEAD_DIM), lambda b, h, qi: (b, kcol + h)),
            pl.BlockSpec((n_lat, HEAD_DIM), lambda b, h, qi: (b, vcol + h)),
            pl.BlockSpec((n_ctx, HEAD_DIM), lambda b, h, qi: (b, h)),
            pl.BlockSpec((n_ctx, HEAD_DIM), lambda b, h, qi: (b, N_KV_HEADS + h)),
        ],
        out_specs=pl.BlockSpec((ATT_TQ, Q_PER_KV * HEAD_DIM), lambda b, h, qi: (b * q_tiles + qi, h)),
        out_shape=jax.ShapeDtypeStruct((batch * n_lat, ATTN_WIDTH), BF16),
        scratch_shapes=[pltpu.VMEM((n_keys, HEAD_DIM), BF16), pltpu.VMEM((HEAD_DIM, n_keys), BF16),
                        pltpu.VMEM((2, n_keys, ATT_SUB), F32), pltpu.VMEM((1, ATT_SUB), F32)],
        compiler_params=_cparams(("arbitrary", "arbitrary", "arbitrary")),
        name="attn",
    )(p, p, p, p, pc, pc)


FOU_T = 256


def _fourier_kernel(u_ref, ct_ref, st_ref, cc_ref, sc_ref, jrev_ref, alt_ref, f_ref,
                    e_ref, o_ref, carry_ref, *, n_lat):
    s = pl.program_id(1)
    half = n_lat // 2
    tiles = half // FOU_T
    row = lax.broadcasted_iota(jnp.int32, (FOU_T, FOURIER_WIDTH), 0)

    def chan_dft(g_mat, h_mat):
        fd, fm = [], []
        for g in range(N_FOURIER_GROUPS):
            sl = slice(g * FOURIER_GROUP_DIM, (g + 1) * FOURIER_GROUP_DIM)
            gc = _dot(g_mat[:, sl].astype(BF16), cc_ref[...])
            if h_mat is None:
                fd.append(gc)
                continue
            hs = _dot(h_mat[:, sl].astype(BF16), sc_ref[...])
            fd.append(gc - hs)
            fm.append(gc + hs)
        return jnp.concatenate(fd, axis=1), (jnp.concatenate(fm, axis=1) if fm else None)

    @pl.when(s == 0)
    def _():
        for t in range(tiles):
            lo = u_ref[t * FOU_T:(t + 1) * FOU_T, :].astype(F32)
            hi_tile = u_ref[(2 * tiles - 1 - t) * FOU_T:(2 * tiles - t) * FOU_T, :]
            part = _dot(jrev_ref[...], hi_tile)
            if t == 0:
                first = jnp.zeros((1, FOURIER_WIDTH), F32)
            else:
                r0 = (2 * tiles - t) * FOU_T
                first = u_ref[r0:r0 + 1, :].astype(F32)
            part = jnp.where(row == 0, first, part)
            e_ref[t * FOU_T:(t + 1) * FOU_T, :] = (lo + part).astype(BF16)
            o_ref[t * FOU_T:(t + 1) * FOU_T, :] = (lo - part).astype(BF16)
        g_nyq = _dot(alt_ref[...], u_ref[...])
        f_nyq, _ = chan_dft(g_nyq, None)
        carry_ref[...] = f_nyq

    scale = n_lat ** -0.5
    u_half = u_ref[half:half + 1, :].astype(F32) * scale
    g_mat = _dot(ct_ref[...], e_ref[...]) + jnp.where(row % 2 == 0, u_half, -u_half)
    h_mat = _dot(st_ref[...], o_ref[...])
    fd, fm = chan_dft(g_mat, h_mat)
    f_ref[0, 0, 0] = fd.astype(BF16)
    fm_bf = fm.astype(BF16)
    mirrored = jnp.where(row == 0, carry_ref[0:1, :], _dot(jrev_ref[...], fm_bf))
    f_ref[0, 1, 0] = mirrored.astype(BF16)
    carry_ref[0:1, :] = fm_bf[0:1, :].astype(F32)


def _fourier(p, tables, batch, n_lat):
    ct, st, cc, sc, jrev, alt = tables
    ucol = V_END // FOURIER_WIDTH
    tiles = n_lat // 2 // FOU_T
    const2 = lambda b, s: (0, 0)
    return pl.pallas_call(
        functools.partial(_fourier_kernel, n_lat=n_lat),
        grid=(batch, tiles),
        in_specs=[
            pl.BlockSpec((n_lat, FOURIER_WIDTH), lambda b, s: (b, ucol)),
            pl.BlockSpec((FOU_T, n_lat // 2), lambda b, s: (tiles - 1 - s, 0)),
            pl.BlockSpec((FOU_T, n_lat // 2), lambda b, s: (tiles - 1 - s, 0)),
            pl.BlockSpec(cc.shape, const2),
            pl.BlockSpec(sc.shape, const2),
            pl.BlockSpec(jrev.shape, const2),
            pl.BlockSpec(alt.shape, const2),
        ],
        out_specs=pl.BlockSpec((1, 2, 1, FOU_T, FOURIER_WIDTH), lambda b, s: (b, 0, tiles - 1 - s, 0, 0)),
        out_shape=jax.ShapeDtypeStruct((batch, 2, tiles, FOU_T, FOURIER_WIDTH), BF16),
        scratch_shapes=[pltpu.VMEM((n_lat // 2, FOURIER_WIDTH), BF16),
                        pltpu.VMEM((n_lat // 2, FOURIER_WIDTH), BF16),
                        pltpu.VMEM((SUBLANES, FOURIER_WIDTH), F32)],
        compiler_params=_cparams(("arbitrary", "arbitrary")),
        name="fourier",
    )(p, ct, st, cc, sc, jrev, alt)


MRG_TM = FOU_T


def _merge_kernel(o_ref, f_ref, gates_ref, x_ref, mod_ref, wa_ref, wf_ref, wo_ref, x1_ref):
    a = _dot(o_ref[...], wa_ref[...])
    fm = _dot(f_ref[0, 0, 0], wf_ref[...])
    z = gates_ref[:, 0:D_MODEL].astype(F32) * a + gates_ref[:, D_MODEL:].astype(F32) * fm
    y = _dot(z.astype(BF16), wo_ref[...])
    x1_ref[...] = x_ref[...] + mod_ref[0, 2:3, :] * y


def _merge(o, f, p, x2, mod3, wa, wf, wo, n_lat):
    rows = x2.shape[0]
    tiles_per_batch = n_lat // MRG_TM
    const = lambda i: (0, 0)
    single = pl.Buffered(1)

    def f_index(i):
        t = i % tiles_per_batch
        upper = t // (tiles_per_batch // 2)
        return (i // tiles_per_batch, upper, jnp.where(upper == 1, tiles_per_batch - 1 - t, t), 0, 0)

    return pl.pallas_call(
        _merge_kernel,
        grid=(rows // MRG_TM,),
        in_specs=[
            pl.BlockSpec((MRG_TM, ATTN_WIDTH), lambda i: (i, 0)),
            pl.BlockSpec((1, 1, 1, MRG_TM, FOURIER_WIDTH), f_index),
            pl.BlockSpec((MRG_TM, 2 * D_MODEL), lambda i: (i, F_END // (2 * D_MODEL))),
            pl.BlockSpec((MRG_TM, D_MODEL), lambda i: (i, 0)),
            pl.BlockSpec((1, N_MOD, D_MODEL), lambda i: (i // tiles_per_batch, 0, 0)),
            pl.BlockSpec(wa.shape, const, pipeline_mode=single),
            pl.BlockSpec(wf.shape, const, pipeline_mode=single),
            pl.BlockSpec(wo.shape, const, pipeline_mode=single),
        ],
        out_specs=pl.BlockSpec((MRG_TM, D_MODEL), lambda i: (i, 0)),
        out_shape=jax.ShapeDtypeStruct((rows, D_MODEL), F32),
        compiler_params=_cparams(("arbitrary",)),
        name="merge",
    )(o, f, p, x2, mod3, wa, wf, wo)


MLP_TM = 512
MLP_TF = 1024


def _mlp_kernel(x1_ref, mod_ref, n2g_ref, w1_ref, w2_ref, fg_ref, out_ref, h_ref, acc_ref):
    kf = pl.program_id(1)

    @pl.when(kf == 0)
    def _():
        xn = _rmsnorm_rows(x1_ref[...], n2g_ref[...])
        h = xn * (1.0 + mod_ref[0, 4:5, :]) + mod_ref[0, 3:4, :]
        h_ref[...] = h.astype(BF16)
        acc_ref[...] = jnp.zeros(acc_ref.shape, F32)

    hid = jnp.maximum(_dot(h_ref[...], w1_ref[...]), 0.0)
    acc_ref[...] += _dot((hid * hid).astype(BF16), w2_ref[...])

    @pl.when(kf == pl.num_programs(1) - 1)
    def _():
        x2 = x1_ref[...] + mod_ref[0, 5:6, :] * acc_ref[...]
        out_ref[...] = _rmsnorm_rows(x2, fg_ref[...])


def _mlp(x1, mod3, n2g, w1, w2, fg, n_lat):
    rows = x1.shape[0]
    tiles_per_batch = n_lat // MLP_TM
    return pl.pallas_call(
        _mlp_kernel,
        grid=(rows // MLP_TM, D_FF // MLP_TF),
        in_specs=[
            pl.BlockSpec((MLP_TM, D_MODEL), lambda i, kf: (i, 0)),
            pl.BlockSpec((1, N_MOD, D_MODEL), lambda i, kf: (i // tiles_per_batch, 0, 0)),
            pl.BlockSpec((1, D_MODEL), lambda i, kf: (0, 0)),
            pl.BlockSpec((D_MODEL, MLP_TF), lambda i, kf: (0, kf)),
            pl.BlockSpec((MLP_TF, D_MODEL), lambda i, kf: (kf, 0)),
            pl.BlockSpec((1, D_MODEL), lambda i, kf: (0, 0)),
        ],
        out_specs=pl.BlockSpec((MLP_TM, D_MODEL), lambda i, kf: (i, 0)),
        out_shape=jax.ShapeDtypeStruct((rows, D_MODEL), F32),
        scratch_shapes=[pltpu.VMEM((MLP_TM, D_MODEL), BF16), pltpu.VMEM((MLP_TM, D_MODEL), F32)],
        compiler_params=_cparams(("arbitrary", "arbitrary")),
        name="mlp",
    )(x1, mod3, n2g, w1, w2, fg)


def _rope_tables(rows):
    lane = jnp.arange(INP_SUB, dtype=jnp.int32)[None, :] % HEAD_DIM
    pos = jnp.arange(rows * GRID_W, dtype=jnp.int32)[:, None]
    pair = lane // 2
    inv = 1.0 / (ROPE_THETA ** ((pair % ROPE_AXIS_PAIRS).astype(F32) / ROPE_AXIS_PAIRS))
    coord = jnp.where(pair < ROPE_AXIS_PAIRS, pos // GRID_W, pos % GRID_W).astype(F32)
    ang = coord * inv
    return jnp.stack([jnp.cos(ang), jnp.sin(ang)], axis=0)


def _dft_tables(size, period, scale):
    n_lo = 1 << ((size.bit_length() - 1) // 2)
    n_hi = size // n_lo
    assert n_lo * n_hi == size
    idx = jnp.arange(size, dtype=jnp.int32)

    def cos_sin(mult):
        ang = ((mult[:, None] * idx[None, :]) % period).astype(F32) * (2.0 * math.pi / period)
        return jnp.cos(ang), jnp.sin(ang)

    c_hi, s_hi = cos_sin(jnp.arange(n_hi, dtype=jnp.int32) * n_lo)
    c_lo, s_lo = cos_sin(jnp.arange(n_lo, dtype=jnp.int32))
    c_hi, s_hi = c_hi[:, None, :] * scale, s_hi[:, None, :] * scale
    c_lo, s_lo = c_lo[None, :, :], s_lo[None, :, :]
    cos = (c_hi * c_lo - s_hi * s_lo).reshape(size, size)
    sin = (s_hi * c_lo + c_hi * s_lo).reshape(size, size)
    return cos.astype(BF16), sin.astype(BF16)


def _fourier_tables(n_lat):
    half = n_lat // 2
    ct, st = _dft_tables(half, n_lat, n_lat ** -0.5)
    cc, sc = _dft_tables(FOURIER_GROUP_DIM, FOURIER_GROUP_DIM, FOURIER_GROUP_DIM ** -0.5)
    jrev = np.zeros((FOU_T, FOU_T), np.float32)
    jrev[np.arange(1, FOU_T), FOU_T - np.arange(1, FOU_T)] = 1.0
    alt = np.zeros((SUBLANES, n_lat), np.float32)
    alt[0] = (1.0 - 2.0 * (np.arange(n_lat) % 2)) * n_lat ** -0.5
    return ct, st, cc, sc, jnp.asarray(jrev, dtype=BF16), jnp.asarray(alt, dtype=BF16)


def kernel(x, c, ctx, c_ctx, w_ada, b_ada, norm1_g, w_in, q_norm_g, k_norm_g, w_attn_o, w_fourier,
           w_out, norm2_g, w1, w2, final_norm_g):
    batch, n_lat, _ = x.shape
    n_ctx = ctx.shape[1]
    assert w_ada.shape[0] == 1, "single-layer block"
    assert n_lat % INP_TM == 0 and n_ctx % SUBLANES == 0

    cc = jnp.concatenate([c, c_ctx[None, :]], axis=0)
    cb = jnp.broadcast_to(cc[:, :, None], (batch + 1, D_MODEL, LANES))
    mod = _adaln(cb, w_ada[0], b_ada)
    mod3 = mod.reshape(SUBLANES, N_MOD, D_MODEL)

    w_in_bf = w_in[0]
    n1g = norm1_g
    heads_per_sub = INP_SUB // HEAD_DIM
    qg, kg = jnp.tile(q_norm_g, (1, heads_per_sub)), jnp.tile(k_norm_g, (1, heads_per_sub))
    rope = _rope_tables(n_lat // GRID_W)

    x2 = x.reshape(batch * n_lat, D_MODEL)
    tiles_per_batch = n_lat // INP_TM
    p = _inproj(x2, mod3, lambda i: i // tiles_per_batch, n1g, w_in_bf, rope, qg, kg,
                tm=INP_TM, col_block0=0, n_col_blocks=IN_WIDTH // INP_TN, use_rope=True,
                rope_tiles=tiles_per_batch, name="inproj")
    ctx2 = ctx.reshape(batch * n_ctx, D_MODEL)
    pc = _inproj(ctx2, mod3, lambda i: batch, n1g, w_in_bf, rope, qg, kg,
                 tm=batch * n_ctx, col_block0=Q_END // INP_TN, n_col_blocks=(V_END - Q_END) // INP_TN,
                 use_rope=False, rope_tiles=1, name="ctxproj")

    o = _attention(p, pc, batch, n_lat, n_ctx)

    f = _fourier(p, _fourier_tables(n_lat), batch, n_lat)

    x1 = _merge(o, f, p, x2, mod3, w_attn_o[0].astype(BF16), w_fourier[0].astype(BF16),
                w_out[0].astype(BF16), n_lat)
    out = _mlp(x1, mod3, norm2_g, w1[0].astype(BF16), w2[0].astype(BF16),
               final_norm_g[None, :], n_lat)
    return out.reshape(batch, n_lat, D_MODEL)
```

```python
import functools
import math

import jax
import jax.numpy as jnp
import numpy as np
from jax import lax
from jax.experimental import pallas as pl
from jax.experimental.pallas import tpu as pltpu

D_MODEL = 2048
GRID_W = 64
HEAD_DIM = 128
N_Q_HEADS = 16
N_KV_HEADS = 4
Q_PER_KV = N_Q_HEADS // N_KV_HEADS
ATTN_WIDTH = N_Q_HEADS * HEAD_DIM
KV_WIDTH = N_KV_HEADS * HEAD_DIM
N_FOURIER_GROUPS = 4
FOURIER_GROUP_DIM = 256
FOURIER_WIDTH = N_FOURIER_GROUPS * FOURIER_GROUP_DIM
Q_END = ATTN_WIDTH
K_END = Q_END + KV_WIDTH
V_END = K_END + KV_WIDTH
F_END = V_END + FOURIER_WIDTH
IN_WIDTH = F_END + 2 * D_MODEL
D_FF = 4 * D_MODEL
N_MOD = 6
ROPE_THETA = 10000.0
ROPE_AXIS_PAIRS = HEAD_DIM // 4
EPS = 1e-6

LANES = 128
SUBLANES = 8
VMEM_LIMIT = 56 * 1024 * 1024

BF16 = jnp.bfloat16
F32 = jnp.float32


def _cparams(sem):
    return pltpu.CompilerParams(dimension_semantics=sem, vmem_limit_bytes=VMEM_LIMIT)


def _dot(a, b):
    return jnp.dot(a, b, preferred_element_type=F32)


def _rmsnorm_rows(x, g):
    ms = jnp.mean(x * x, axis=-1, keepdims=True)
    return x * lax.rsqrt(ms + EPS) * g


ADA_TN = 1024


def _adaln_kernel(cb_ref, w_ref, b_ref, out_ref, act_ref):
    n_rows = cb_ref.shape[0]

    @pl.when(pl.program_id(0) == 0)
    def _():
        for r in range(n_rows):
            cr = cb_ref[r]
            act_ref[r] = cr * jax.nn.sigmoid(cr)

    out_ref[...] = jnp.zeros(out_ref.shape, F32)
    for t in range(ADA_TN // LANES):
        wt = w_ref[:, t * LANES:(t + 1) * LANES]
        for r in range(n_rows):
            prod = wt * act_ref[r]
            red = prod.reshape(D_MODEL // SUBLANES, SUBLANES, LANES).sum(axis=0)
            tot = red.sum(axis=0, keepdims=True) + b_ref[:, t * LANES:(t + 1) * LANES]
            out_ref[r:r + 1, t * LANES:(t + 1) * LANES] = tot


def _adaln(cb, w_ada, b_ada):
    n_out = w_ada.shape[1]
    return pl.pallas_call(
        _adaln_kernel,
        grid=(n_out // ADA_TN,),
        in_specs=[
            pl.BlockSpec(cb.shape, lambda j: (0, 0, 0)),
            pl.BlockSpec((D_MODEL, ADA_TN), lambda j: (0, j)),
            pl.BlockSpec((1, ADA_TN), lambda j: (0, j)),
        ],
        out_specs=pl.BlockSpec((SUBLANES, ADA_TN), lambda j: (0, j)),
        out_shape=jax.ShapeDtypeStruct((SUBLANES, n_out), F32),
        scratch_shapes=[pltpu.VMEM(cb.shape, F32)],
        compiler_params=_cparams(("arbitrary",)),
        name="adaln",
    )(cb, w_ada, b_ada)


INP_TM = 1024
INP_TN = 1024
INP_SUB = 256
Q_SCALE = HEAD_DIM ** -0.5 * math.log2(math.e)


def _head_norm_rope(acc, g, sel_ref, rope_ref, mult):
    xg = acc * g
    ssq = _dot((acc * acc).astype(BF16), sel_ref[0])
    r = lax.rsqrt(ssq * (1.0 / HEAD_DIM) + EPS) * mult
    if rope_ref is None:
        return xg * r
    swapped = _dot(xg.astype(BF16), sel_ref[1])
    return (xg * rope_ref[0] + swapped * rope_ref[1]) * r


def _col_kind(col):
    if col < Q_END:
        return "q"
    if col < K_END:
        return "k"
    if col < F_END:
        return "plain"
    return "gate"


def _inproj_kernel(x_ref, mod_ref, n1g_ref, w_ref, rope_ref, qg_ref, kg_ref, sel_ref, out_ref, h_ref,
                   *, col_block0, n_col_blocks, use_rope):
    j = pl.program_id(1)

    @pl.when(j == 0)
    def _():
        xn = _rmsnorm_rows(x_ref[...], n1g_ref[...])
        h = xn * (1.0 + mod_ref[0, 1:2, :]) + mod_ref[0, 0:1, :]
        h_ref[...] = h.astype(BF16)

    def epilogue(acc, kind, c0):
        if kind in ("q", "k"):
            g_ref, mult = (qg_ref, Q_SCALE) if kind == "q" else (kg_ref, 1.0)
            res = _head_norm_rope(acc, g_ref[...], sel_ref, rope_ref if use_rope else None, mult)
            out_ref[:, c0:c0 + INP_SUB] = res.astype(BF16)
        elif kind == "plain":
            out_ref[:, c0:c0 + INP_SUB] = acc.astype(BF16)
        else:
            out_ref[:, c0:c0 + INP_SUB] = (0.5 * jnp.tanh(0.5 * acc) + 0.5).astype(BF16)

    def tile(col_block):
        subs = range(0, INP_TN, INP_SUB)
        pending = None
        for c0 in subs:
            acc = _dot(h_ref[...], w_ref[:, c0:c0 + INP_SUB].astype(BF16))
            if pending is not None:
                epilogue(*pending)
            pending = (acc, _col_kind(col_block * INP_TN + c0), c0)
        epilogue(*pending)

    def kinds_of(cb):
        return tuple(_col_kind(cb * INP_TN + c0) for c0 in range(0, INP_TN, INP_SUB))

    lo = 0
    while lo < n_col_blocks:
        hi = lo
        while hi + 1 < n_col_blocks and kinds_of(col_block0 + hi + 1) == kinds_of(col_block0 + lo):
            hi += 1
        pl.when(jnp.logical_and(j >= lo, j <= hi))(functools.partial(tile, col_block0 + lo))
        lo = hi + 1


def _head_select_matrices():
    idx = np.arange(INP_SUB)
    ones_bd = (idx[:, None] // HEAD_DIM == idx[None, :] // HEAD_DIM).astype(np.float32)
    swap = np.zeros((INP_SUB, INP_SUB), np.float32)
    swap[idx[1::2], idx[0::2]] = -1.0
    swap[idx[0::2], idx[1::2]] = 1.0
    return jnp.asarray(np.stack([ones_bd, swap]), dtype=BF16)


def _inproj(x2, mod3, mod_row_of_tile, n1g, w_bf, rope, qg, kg, *, tm, col_block0, n_col_blocks,
            use_rope, rope_tiles, name):
    rows = x2.shape[0]
    sel = _head_select_matrices()
    kern = functools.partial(_inproj_kernel, col_block0=col_block0, n_col_blocks=n_col_blocks,
                             use_rope=use_rope)
    return pl.pallas_call(
        kern,
        grid=(rows // tm, n_col_blocks),
        in_specs=[
            pl.BlockSpec((tm, D_MODEL), lambda i, j: (i, 0)),
            pl.BlockSpec((1, N_MOD, D_MODEL), lambda i, j: (mod_row_of_tile(i), 0, 0)),
            pl.BlockSpec((1, D_MODEL), lambda i, j: (0, 0)),
            pl.BlockSpec((D_MODEL, INP_TN), lambda i, j: (0, j + col_block0)),
            pl.BlockSpec((2, tm, INP_SUB), lambda i, j: (0, i % rope_tiles, 0)),
            pl.BlockSpec((1, INP_SUB), lambda i, j: (0, 0)),
            pl.BlockSpec((1, INP_SUB), lambda i, j: (0, 0)),
            pl.BlockSpec(sel.shape, lambda i, j: (0, 0, 0)),
        ],
        out_specs=pl.BlockSpec((tm, INP_TN), lambda i, j: (i, j)),
        out_shape=jax.ShapeDtypeStruct((rows, n_col_blocks * INP_TN), BF16),
        scratch_shapes=[pltpu.VMEM((tm, D_MODEL), BF16)],
        compiler_params=_cparams(("arbitrary", "arbitrary")),
        name=name,
    )(x2, mod3, n1g, w_bf, rope, qg, kg, sel)


ATT_SUB = 256
ATT_NSUB = 2
ATT_TQ = ATT_SUB * ATT_NSUB
ATT_CHUNK = 1024


def _key_chunks(n_keys):
    chunks = [(c0, ATT_CHUNK) for c0 in range(0, n_keys - ATT_CHUNK + 1, ATT_CHUNK)]
    done = len(chunks) * ATT_CHUNK
    if done < n_keys:
        chunks.append((done, n_keys - done))
    return chunks


def _attn_kernel(q_ref, qn_ref, k_ref, v_ref, kc_ref, vc_ref, *rest, n_cast):
    w_refs, o_ref, wb_refs = rest[:n_cast], rest[n_cast], rest[n_cast + 1:2 * n_cast + 1]
    kall_ref, vt_ref, s_ref, m_ref = rest[2 * n_cast + 1:]
    for w_ref, wb_ref in zip(w_refs, wb_refs):
        wb_ref[...] = w_ref[...].astype(BF16)

    qi = pl.program_id(2)
    n_lat = k_ref.shape[0]
    n_keys = kall_ref.shape[0]

    units = [(sub, g) for sub in range(ATT_NSUB) for g in range(Q_PER_KV)]
    assert len(units) % 2 == 0
    chunks = _key_chunks(n_keys)

    def q_of(ref, u):
        sub, g = units[u]
        return ref[sub * ATT_SUB:(sub + 1) * ATT_SUB, g * HEAD_DIM:(g + 1) * HEAD_DIM]

    def scores_chunk(q_u, slot, c0, cs, m_run):
        s_c = lax.dot_general(kall_ref[c0:c0 + cs, :], q_u, (((1,), (1,)), ((), ())),
                              preferred_element_type=F32)
        s_ref[slot, c0:c0 + cs, :] = s_c
        mc = jnp.max(s_c, axis=0, keepdims=True)
        return mc if m_run is None else jnp.maximum(m_run, mc)

    @pl.when(qi == 0)
    def _():
        kall_ref[0:n_lat, :] = k_ref[...]
        kall_ref[n_lat:, :] = kc_ref[...]
        vt_ref[:, 0:n_lat] = v_ref[...].astype(F32).T.astype(BF16)
        vt_ref[:, n_lat:] = vc_ref[...].astype(F32).T.astype(BF16)
        m0 = None
        for c0, cs in chunks:
            m0 = scores_chunk(q_of(q_ref, 0), 0, c0, cs, m0)
        m_ref[...] = m0

    m_cur = m_ref[...]
    for t in range(len(units)):
        last = t + 1 == len(units)
        q_nxt = q_of(qn_ref, 0) if last else q_of(q_ref, t + 1)
        m_nxt = l = o_acc = None
        for c0, cs in chunks:
            m_nxt = scores_chunk(q_nxt, (t + 1) % 2, c0, cs, m_nxt)
            p_c = jnp.exp2(s_ref[t % 2, c0:c0 + cs, :] - m_cur)
            lc = jnp.sum(p_c, axis=0, keepdims=True)
            l = lc if l is None else l + lc
            pv = _dot(vt_ref[:, c0:c0 + cs], p_c.astype(BF16))
            o_acc = pv if o_acc is None else o_acc + pv
        sub, g = units[t]
        o_ref[sub * ATT_SUB:(sub + 1) * ATT_SUB, g * HEAD_DIM:(g + 1) * HEAD_DIM] = (
            (o_acc * (1.0 / l)).T.astype(BF16))
        m_cur = m_nxt
    m_ref[...] = m_cur


def _attention(p, pc, batch, n_lat, n_ctx, weights):
    q_tiles = n_lat // ATT_TQ
    kcol = Q_END // HEAD_DIM
    vcol = K_END // HEAD_DIM
    n_keys = n_lat + n_ctx
    n_steps = batch * N_KV_HEADS * q_tiles
    slab_specs = []
    for w in weights:
        slab = w.shape[0] // n_steps
        assert slab * n_steps == w.shape[0] and slab % (2 * SUBLANES) == 0
        slab_specs.append(pl.BlockSpec(
            (slab, w.shape[1]), lambda b, h, qi: ((b * N_KV_HEADS + h) * q_tiles + qi, 0)))
    results = pl.pallas_call(
        functools.partial(_attn_kernel, n_cast=len(weights)),
        grid=(batch, N_KV_HEADS, q_tiles),
        in_specs=[
            pl.BlockSpec((ATT_TQ, Q_PER_KV * HEAD_DIM), lambda b, h, qi: (b * q_tiles + qi, h)),
            pl.BlockSpec((ATT_TQ, Q_PER_KV * HEAD_DIM),
                         lambda b, h, qi: (b * q_tiles + jnp.minimum(qi + 1, q_tiles - 1), h)),
            pl.BlockSpec((n_lat, HEAD_DIM), lambda b, h, qi: (b, kcol + h)),
            pl.BlockSpec((n_lat, HEAD_DIM), lambda b, h, qi: (b, vcol + h)),
            pl.BlockSpec((n_ctx, HEAD_DIM), lambda b, h, qi: (b, h)),
            pl.BlockSpec((n_ctx, HEAD_DIM), lambda b, h, qi: (b, N_KV_HEADS + h)),
        ] + slab_specs,
        out_specs=[pl.BlockSpec((ATT_TQ, Q_PER_KV * HEAD_DIM),
                                lambda b, h, qi: (b * q_tiles + qi, h))] + slab_specs,
        out_shape=[jax.ShapeDtypeStruct((batch * n_lat, ATTN_WIDTH), BF16)]
        + [jax.ShapeDtypeStruct(w.shape, BF16) for w in weights],
        scratch_shapes=[pltpu.VMEM((n_keys, HEAD_DIM), BF16), pltpu.VMEM((HEAD_DIM, n_keys), BF16),
                        pltpu.VMEM((2, n_keys, ATT_SUB), F32), pltpu.VMEM((1, ATT_SUB), F32)],
        compiler_params=_cparams(("arbitrary", "arbitrary", "arbitrary")),
        name="attn",
    )(p, p, p, p, pc, pc, *weights)
    return results[0], results[1:]


FOU_T = 256


def _fourier_kernel(u_ref, ct_ref, st_ref, cc_ref, sc_ref, jrev_ref, alt_ref, f_ref,
                    e_ref, o_ref, carry_ref, *, n_lat):
    s = pl.program_id(1)
    half = n_lat // 2
    tiles = half // FOU_T
    row = lax.broadcasted_iota(jnp.int32, (FOU_T, FOURIER_WIDTH), 0)

    def chan_dft(g_mat, h_mat):
        fd, fm = [], []
        for g in range(N_FOURIER_GROUPS):
            sl = slice(g * FOURIER_GROUP_DIM, (g + 1) * FOURIER_GROUP_DIM)
            gc = _dot(g_mat[:, sl].astype(BF16), cc_ref[...])
            if h_mat is None:
                fd.append(gc)
                continue
            hs = _dot(h_mat[:, sl].astype(BF16), sc_ref[...])
            fd.append(gc - hs)
            fm.append(gc + hs)
        return jnp.concatenate(fd, axis=1), (jnp.concatenate(fm, axis=1) if fm else None)

    @pl.when(s == 0)
    def _():
        for t in range(tiles):
            lo = u_ref[t * FOU_T:(t + 1) * FOU_T, :].astype(F32)
            hi_tile = u_ref[(2 * tiles - 1 - t) * FOU_T:(2 * tiles - t) * FOU_T, :]
            part = _dot(jrev_ref[...], hi_tile)
            if t == 0:
                first = jnp.zeros((1, FOURIER_WIDTH), F32)
            else:
                r0 = (2 * tiles - t) * FOU_T
                first = u_ref[r0:r0 + 1, :].astype(F32)
            part = jnp.where(row == 0, first, part)
            e_ref[t * FOU_T:(t + 1) * FOU_T, :] = (lo + part).astype(BF16)
            o_ref[t * FOU_T:(t + 1) * FOU_T, :] = (lo - part).astype(BF16)
        g_nyq = _dot(alt_ref[...], u_ref[...])
        f_nyq, _ = chan_dft(g_nyq, None)
        carry_ref[...] = f_nyq

    scale = n_lat ** -0.5
    u_half = u_ref[half:half + 1, :].astype(F32) * scale
    g_mat = _dot(ct_ref[...], e_ref[...]) + jnp.where(row % 2 == 0, u_half, -u_half)
    h_mat = _dot(st_ref[...], o_ref[...])
    fd, fm = chan_dft(g_mat, h_mat)
    f_ref[0, 0, 0] = fd.astype(BF16)
    fm_bf = fm.astype(BF16)
    mirrored = jnp.where(row == 0, carry_ref[0:1, :], _dot(jrev_ref[...], fm_bf))
    f_ref[0, 1, 0] = mirrored.astype(BF16)
    carry_ref[0:1, :] = fm_bf[0:1, :].astype(F32)


def _fourier(p, tables, batch, n_lat):
    ct, st, cc, sc, jrev, alt = tables
    ucol = V_END // FOURIER_WIDTH
    tiles = n_lat // 2 // FOU_T
    const2 = lambda b, s: (0, 0)
    return pl.pallas_call(
        functools.partial(_fourier_kernel, n_lat=n_lat),
        grid=(batch, tiles),
        in_specs=[
            pl.BlockSpec((n_lat, FOURIER_WIDTH), lambda b, s: (b, ucol)),
            pl.BlockSpec((FOU_T, n_lat // 2), lambda b, s: (tiles - 1 - s, 0)),
            pl.BlockSpec((FOU_T, n_lat // 2), lambda b, s: (tiles - 1 - s, 0)),
            pl.BlockSpec(cc.shape, const2),
            pl.BlockSpec(sc.shape, const2),
            pl.BlockSpec(jrev.shape, const2),
            pl.BlockSpec(alt.shape, const2),
        ],
        out_specs=pl.BlockSpec((1, 2, 1, FOU_T, FOURIER_WIDTH), lambda b, s: (b, 0, tiles - 1 - s, 0, 0)),
        out_shape=jax.ShapeDtypeStruct((batch, 2, tiles, FOU_T, FOURIER_WIDTH), BF16),
        scratch_shapes=[pltpu.VMEM((n_lat // 2, FOURIER_WIDTH), BF16),
                        pltpu.VMEM((n_lat // 2, FOURIER_WIDTH), BF16),
                        pltpu.VMEM((SUBLANES, FOURIER_WIDTH), F32)],
        compiler_params=_cparams(("arbitrary", "arbitrary")),
        name="fourier",
    )(p, ct, st, cc, sc, jrev, alt)


MRG_TM = FOU_T


def _merge_kernel(o_ref, f_ref, gates_ref, x_ref, mod_ref, wa_ref, wf_ref, wo_ref, x1_ref):
    a = _dot(o_ref[...], wa_ref[...])
    fm = _dot(f_ref[0, 0, 0], wf_ref[...])
    z = gates_ref[:, 0:D_MODEL].astype(F32) * a + gates_ref[:, D_MODEL:].astype(F32) * fm
    y = _dot(z.astype(BF16), wo_ref[...])
    x1_ref[...] = x_ref[...] + mod_ref[0, 2:3, :] * y


def _merge(o, f, p, x2, mod3, wa, wf, wo, n_lat):
    rows = x2.shape[0]
    tiles_per_batch = n_lat // MRG_TM
    const = lambda i: (0, 0)
    single = pl.Buffered(1)

    def f_index(i):
        t = i % tiles_per_batch
        upper = t // (tiles_per_batch // 2)
        return (i // tiles_per_batch, upper, jnp.where(upper == 1, tiles_per_batch - 1 - t, t), 0, 0)

    return pl.pallas_call(
        _merge_kernel,
        grid=(rows // MRG_TM,),
        in_specs=[
            pl.BlockSpec((MRG_TM, ATTN_WIDTH), lambda i: (i, 0)),
            pl.BlockSpec((1, 1, 1, MRG_TM, FOURIER_WIDTH), f_index),
            pl.BlockSpec((MRG_TM, 2 * D_MODEL), lambda i: (i, F_END // (2 * D_MODEL))),
            pl.BlockSpec((MRG_TM, D_MODEL), lambda i: (i, 0)),
            pl.BlockSpec((1, N_MOD, D_MODEL), lambda i: (i // tiles_per_batch, 0, 0)),
            pl.BlockSpec(wa.shape, const, pipeline_mode=single),
            pl.BlockSpec(wf.shape, const, pipeline_mode=single),
            pl.BlockSpec(wo.shape, const, pipeline_mode=single),
        ],
        out_specs=pl.BlockSpec((MRG_TM, D_MODEL), lambda i: (i, 0)),
        out_shape=jax.ShapeDtypeStruct((rows, D_MODEL), F32),
        compiler_params=_cparams(("arbitrary",)),
        name="merge",
    )(o, f, p, x2, mod3, wa, wf, wo)


MLP_TM = 512
MLP_TF = 1024


def _mlp_kernel(x1_ref, mod_ref, n2g_ref, w1_ref, w2_ref, fg_ref, out_ref, h_ref, acc_ref):
    kf = pl.program_id(1)

    @pl.when(kf == 0)
    def _():
        xn = _rmsnorm_rows(x1_ref[...], n2g_ref[...])
        h = xn * (1.0 + mod_ref[0, 4:5, :]) + mod_ref[0, 3:4, :]
        h_ref[...] = h.astype(BF16)
        acc_ref[...] = jnp.zeros(acc_ref.shape, F32)

    hid = jnp.maximum(_dot(h_ref[...], w1_ref[...]), 0.0)
    acc_ref[...] += _dot((hid * hid).astype(BF16), w2_ref[...])

    @pl.when(kf == pl.num_programs(1) - 1)
    def _():
        x2 = x1_ref[...] + mod_ref[0, 5:6, :] * acc_ref[...]
        out_ref[...] = _rmsnorm_rows(x2, fg_ref[...])


def _mlp(x1, mod3, n2g, w1, w2, fg, n_lat):
    rows = x1.shape[0]
    tiles_per_batch = n_lat // MLP_TM
    return pl.pallas_call(
        _mlp_kernel,
        grid=(rows // MLP_TM, D_FF // MLP_TF),
        in_specs=[
            pl.BlockSpec((MLP_TM, D_MODEL), lambda i, kf: (i, 0)),
            pl.BlockSpec((1, N_MOD, D_MODEL), lambda i, kf: (i // tiles_per_batch, 0, 0)),
            pl.BlockSpec((1, D_MODEL), lambda i, kf: (0, 0)),
            pl.BlockSpec((D_MODEL, MLP_TF), lambda i, kf: (0, kf)),
            pl.BlockSpec((MLP_TF, D_MODEL), lambda i, kf: (kf, 0)),
            pl.BlockSpec((1, D_MODEL), lambda i, kf: (0, 0)),
        ],
        out_specs=pl.BlockSpec((MLP_TM, D_MODEL), lambda i, kf: (i, 0)),
        out_shape=jax.ShapeDtypeStruct((rows, D_MODEL), F32),
        scratch_shapes=[pltpu.VMEM((MLP_TM, D_MODEL), BF16), pltpu.VMEM((MLP_TM, D_MODEL), F32)],
        compiler_params=_cparams(("arbitrary", "arbitrary")),
        name="mlp",
    )(x1, mod3, n2g, w1, w2, fg)


def _rope_tables(rows):
    lane = jnp.arange(INP_SUB, dtype=jnp.int32)[None, :] % HEAD_DIM
    pos = jnp.arange(rows * GRID_W, dtype=jnp.int32)[:, None]
    pair = lane // 2
    inv = 1.0 / (ROPE_THETA ** ((pair % ROPE_AXIS_PAIRS).astype(F32) / ROPE_AXIS_PAIRS))
    coord = jnp.where(pair < ROPE_AXIS_PAIRS, pos // GRID_W, pos % GRID_W).astype(F32)
    ang = coord * inv
    return jnp.stack([jnp.cos(ang), jnp.sin(ang)], axis=0)


def _dft_tables(size, period, scale):
    n_lo = 1 << ((size.bit_length() - 1) // 2)
    n_hi = size // n_lo
    assert n_lo * n_hi == size
    idx = jnp.arange(size, dtype=jnp.int32)

    def cos_sin(mult):
        ang = ((mult[:, None] * idx[None, :]) % period).astype(F32) * (2.0 * math.pi / period)
        return jnp.cos(ang), jnp.sin(ang)

    c_hi, s_hi = cos_sin(jnp.arange(n_hi, dtype=jnp.int32) * n_lo)
    c_lo, s_lo = cos_sin(jnp.arange(n_lo, dtype=jnp.int32))
    c_hi, s_hi = c_hi[:, None, :] * scale, s_hi[:, None, :] * scale
    c_lo, s_lo = c_lo[None, :, :], s_lo[None, :, :]
    cos = (c_hi * c_lo - s_hi * s_lo).reshape(size, size)
    sin = (s_hi * c_lo + c_hi * s_lo).reshape(size, size)
    return cos.astype(BF16), sin.astype(BF16)


def _fourier_tables(n_lat):
    half = n_lat // 2
    ct, st = _dft_tables(half, n_lat, n_lat ** -0.5)
    cc, sc = _dft_tables(FOURIER_GROUP_DIM, FOURIER_GROUP_DIM, FOURIER_GROUP_DIM ** -0.5)
    jrev = np.zeros((FOU_T, FOU_T), np.float32)
    jrev[np.arange(1, FOU_T), FOU_T - np.arange(1, FOU_T)] = 1.0
    alt = np.zeros((SUBLANES, n_lat), np.float32)
    alt[0] = (1.0 - 2.0 * (np.arange(n_lat) % 2)) * n_lat ** -0.5
    return ct, st, cc, sc, jnp.asarray(jrev, dtype=BF16), jnp.asarray(alt, dtype=BF16)


def kernel(x, c, ctx, c_ctx, w_ada, b_ada, norm1_g, w_in, q_norm_g, k_norm_g, w_attn_o, w_fourier,
           w_out, norm2_g, w1, w2, final_norm_g):
    batch, n_lat, _ = x.shape
    n_ctx = ctx.shape[1]
    assert w_ada.shape[0] == 1, "single-layer block"
    assert n_lat % INP_TM == 0 and n_ctx % SUBLANES == 0

    cc = jnp.concatenate([c, c_ctx[None, :]], axis=0)
    cb = jnp.broadcast_to(cc[:, :, None], (batch + 1, D_MODEL, LANES))
    mod = _adaln(cb, w_ada[0], b_ada)
    mod3 = mod.reshape(SUBLANES, N_MOD, D_MODEL)

    w_in_bf = w_in[0]
    n1g = norm1_g
    heads_per_sub = INP_SUB // HEAD_DIM
    qg, kg = jnp.tile(q_norm_g, (1, heads_per_sub)), jnp.tile(k_norm_g, (1, heads_per_sub))
    rope = _rope_tables(n_lat // GRID_W)

    x2 = x.reshape(batch * n_lat, D_MODEL)
    tiles_per_batch = n_lat // INP_TM
    p = _inproj(x2, mod3, lambda i: i // tiles_per_batch, n1g, w_in_bf, rope, qg, kg,
                tm=INP_TM, col_block0=0, n_col_blocks=IN_WIDTH // INP_TN, use_rope=True,
                rope_tiles=tiles_per_batch, name="inproj")
    ctx2 = ctx.reshape(batch * n_ctx, D_MODEL)
    pc = _inproj(ctx2, mod3, lambda i: batch, n1g, w_in_bf, rope, qg, kg,
                 tm=batch * n_ctx, col_block0=Q_END // INP_TN, n_col_blocks=(V_END - Q_END) // INP_TN,
                 use_rope=False, rope_tiles=1, name="ctxproj")

    o, (wa_bf, wf_bf, wo_bf, w1_bf, w2_bf) = _attention(
        p, pc, batch, n_lat, n_ctx, (w_attn_o[0], w_fourier[0], w_out[0], w1[0], w2[0]))

    f = _fourier(p, _fourier_tables(n_lat), batch, n_lat)

    x1 = _merge(o, f, p, x2, mod3, wa_bf, wf_bf, wo_bf, n_lat)
    out = _mlp(x1, mod3, norm2_g, w1_bf, w2_bf, final_norm_g[None, :], n_lat)
    return out.reshape(batch, n_lat, D_MODEL)
```

```python
import functools
import math

import jax
import jax.numpy as jnp
import numpy as np
from jax import lax
from jax.experimental import pallas as pl
from jax.experimental.pallas import tpu as pltpu

D_MODEL = 2048
GRID_W = 64
HEAD_DIM = 128
N_Q_HEADS = 16
N_KV_HEADS = 4
Q_PER_KV = N_Q_HEADS // N_KV_HEADS
ATTN_WIDTH = N_Q_HEADS * HEAD_DIM
KV_WIDTH = N_KV_HEADS * HEAD_DIM
N_FOURIER_GROUPS = 4
FOURIER_GROUP_DIM = 256
FOURIER_WIDTH = N_FOURIER_GROUPS * FOURIER_GROUP_DIM
Q_END = ATTN_WIDTH
K_END = Q_END + KV_WIDTH
V_END = K_END + KV_WIDTH
F_END = V_END + FOURIER_WIDTH
IN_WIDTH = F_END + 2 * D_MODEL
D_FF = 4 * D_MODEL
N_MOD = 6
ROPE_THETA = 10000.0
ROPE_AXIS_PAIRS = HEAD_DIM // 4
EPS = 1e-6

LANES = 128
SUBLANES = 8
VMEM_LIMIT = 56 * 1024 * 1024

BF16 = jnp.bfloat16
F32 = jnp.float32


def _cparams(sem):
    return pltpu.CompilerParams(dimension_semantics=sem, vmem_limit_bytes=VMEM_LIMIT)


def _dot(a, b):
    return jnp.dot(a, b, preferred_element_type=F32)


def _rmsnorm_rows(x, g):
    ms = jnp.mean(x * x, axis=-1, keepdims=True)
    return x * lax.rsqrt(ms + EPS) * g


NORM_ROWS = 128


def _norm_modulate(x_ref, g, scale, shift, h_ref):
    gs = g * (1.0 + scale)
    for r0 in range(0, x_ref.shape[0], NORM_ROWS):
        xa = x_ref[r0:r0 + NORM_ROWS, :]
        r = lax.rsqrt(jnp.mean(xa * xa, axis=-1, keepdims=True) + EPS)
        xb = x_ref[r0:r0 + NORM_ROWS, :]
        h_ref[r0:r0 + NORM_ROWS, :] = (xb * r * gs + shift).astype(BF16)


ADA_TN = 1024


def _adaln_kernel(cb_ref, w_ref, b_ref, out_ref, act_ref):
    n_rows = cb_ref.shape[0]

    @pl.when(pl.program_id(0) == 0)
    def _():
        for r in range(n_rows):
            cr = cb_ref[r]
            act_ref[r] = cr * jax.nn.sigmoid(cr)

    out_ref[...] = jnp.zeros(out_ref.shape, F32)
    for t in range(ADA_TN // LANES):
        wt = w_ref[:, t * LANES:(t + 1) * LANES]
        for r in range(n_rows):
            prod = wt * act_ref[r]
            red = prod.reshape(D_MODEL // SUBLANES, SUBLANES, LANES).sum(axis=0)
            tot = red.sum(axis=0, keepdims=True) + b_ref[:, t * LANES:(t + 1) * LANES]
            out_ref[r:r + 1, t * LANES:(t + 1) * LANES] = tot


def _adaln(cb, w_ada, b_ada):
    n_out = w_ada.shape[1]
    return pl.pallas_call(
        _adaln_kernel,
        grid=(n_out // ADA_TN,),
        in_specs=[
            pl.BlockSpec(cb.shape, lambda j: (0, 0, 0)),
            pl.BlockSpec((D_MODEL, ADA_TN), lambda j: (0, j)),
            pl.BlockSpec((1, ADA_TN), lambda j: (0, j)),
        ],
        out_specs=pl.BlockSpec((SUBLANES, ADA_TN), lambda j: (0, j)),
        out_shape=jax.ShapeDtypeStruct((SUBLANES, n_out), F32),
        scratch_shapes=[pltpu.VMEM(cb.shape, F32)],
        compiler_params=_cparams(("arbitrary",)),
        name="adaln",
    )(cb, w_ada, b_ada)


INP_TM = 1024
INP_TN = 1024
INP_SUB = 256
Q_SCALE = HEAD_DIM ** -0.5 * math.log2(math.e)


def _head_norm_rope(acc, g, sel_ref, rope_ref, mult):
    xg = acc * g
    ssq = _dot((acc * acc).astype(BF16), sel_ref[0])
    r = lax.rsqrt(ssq * (1.0 / HEAD_DIM) + EPS) * mult
    if rope_ref is None:
        return xg * r
    swapped = _dot(xg.astype(BF16), sel_ref[1])
    return (xg * rope_ref[0] + swapped * rope_ref[1]) * r


def _col_kind(col):
    if col < Q_END:
        return "q"
    if col < K_END:
        return "k"
    if col < F_END:
        return "plain"
    return "gate"


def _inproj_kernel(x_ref, mod_ref, n1g_ref, w_ref, rope_ref, qg_ref, kg_ref, sel_ref, out_ref, h_ref,
                   *, col_block0, n_col_blocks, use_rope):
    j = pl.program_id(1)

    @pl.when(j == 0)
    def _():
        _norm_modulate(x_ref, n1g_ref[...], mod_ref[0, 1:2, :], mod_ref[0, 0:1, :], h_ref)

    def epilogue(acc, kind, c0):
        if kind in ("q", "k"):
            g_ref, mult = (qg_ref, Q_SCALE) if kind == "q" else (kg_ref, 1.0)
            res = _head_norm_rope(acc, g_ref[...], sel_ref, rope_ref if use_rope else None, mult)
            out_ref[:, c0:c0 + INP_SUB] = res.astype(BF16)
        elif kind == "plain":
            out_ref[:, c0:c0 + INP_SUB] = acc.astype(BF16)
        else:
            out_ref[:, c0:c0 + INP_SUB] = (0.5 * jnp.tanh(0.5 * acc) + 0.5).astype(BF16)

    def tile(col_block):
        subs = range(0, INP_TN, INP_SUB)
        pending = None
        for c0 in subs:
            acc = _dot(h_ref[...], w_ref[:, c0:c0 + INP_SUB].astype(BF16))
            if pending is not None:
                epilogue(*pending)
            pending = (acc, _col_kind(col_block * INP_TN + c0), c0)
        epilogue(*pending)

    def kinds_of(cb):
        return tuple(_col_kind(cb * INP_TN + c0) for c0 in range(0, INP_TN, INP_SUB))

    lo = 0
    while lo < n_col_blocks:
        hi = lo
        while hi + 1 < n_col_blocks and kinds_of(col_block0 + hi + 1) == kinds_of(col_block0 + lo):
            hi += 1
        pl.when(jnp.logical_and(j >= lo, j <= hi))(functools.partial(tile, col_block0 + lo))
        lo = hi + 1


def _head_select_matrices():
    idx = np.arange(INP_SUB)
    ones_bd = (idx[:, None] // HEAD_DIM == idx[None, :] // HEAD_DIM).astype(np.float32)
    swap = np.zeros((INP_SUB, INP_SUB), np.float32)
    swap[idx[1::2], idx[0::2]] = -1.0
    swap[idx[0::2], idx[1::2]] = 1.0
    return jnp.asarray(np.stack([ones_bd, swap]), dtype=BF16)


def _inproj(x2, mod3, mod_row_of_tile, n1g, w_bf, rope, qg, kg, *, tm, col_block0, n_col_blocks,
            use_rope, rope_tiles, name):
    rows = x2.shape[0]
    sel = _head_select_matrices()
    kern = functools.partial(_inproj_kernel, col_block0=col_block0, n_col_blocks=n_col_blocks,
                             use_rope=use_rope)
    return pl.pallas_call(
        kern,
        grid=(rows // tm, n_col_blocks),
        in_specs=[
            pl.BlockSpec((tm, D_MODEL), lambda i, j: (i, 0)),
            pl.BlockSpec((1, N_MOD, D_MODEL), lambda i, j: (mod_row_of_tile(i), 0, 0)),
            pl.BlockSpec((1, D_MODEL), lambda i, j: (0, 0)),
            pl.BlockSpec((D_MODEL, INP_TN), lambda i, j: (0, j + col_block0)),
            pl.BlockSpec((2, tm, INP_SUB), lambda i, j: (0, i % rope_tiles, 0)),
            pl.BlockSpec((1, INP_SUB), lambda i, j: (0, 0)),
            pl.BlockSpec((1, INP_SUB), lambda i, j: (0, 0)),
            pl.BlockSpec(sel.shape, lambda i, j: (0, 0, 0)),
        ],
        out_specs=pl.BlockSpec((tm, INP_TN), lambda i, j: (i, j)),
        out_shape=jax.ShapeDtypeStruct((rows, n_col_blocks * INP_TN), BF16),
        scratch_shapes=[pltpu.VMEM((tm, D_MODEL), BF16)],
        compiler_params=_cparams(("arbitrary", "arbitrary")),
        name=name,
    )(x2, mod3, n1g, w_bf, rope, qg, kg, sel)


ATT_SUB = 256
ATT_NSUB = 2
ATT_TQ = ATT_SUB * ATT_NSUB
ATT_CHUNK = 1024


def _key_chunks(n_keys):
    chunks = [(c0, ATT_CHUNK) for c0 in range(0, n_keys - ATT_CHUNK + 1, ATT_CHUNK)]
    done = len(chunks) * ATT_CHUNK
    if done < n_keys:
        chunks.append((done, n_keys - done))
    return chunks


def _attn_kernel(q_ref, qn_ref, k_ref, v_ref, kc_ref, vc_ref, *rest, n_cast):
    w_refs, o_ref, wb_refs = rest[:n_cast], rest[n_cast], rest[n_cast + 1:2 * n_cast + 1]
    kall_ref, vt_ref, s_ref, m_ref = rest[2 * n_cast + 1:]
    for w_ref, wb_ref in zip(w_refs, wb_refs):
        wb_ref[...] = w_ref[...].astype(BF16)

    qi = pl.program_id(2)
    n_lat = k_ref.shape[0]
    n_keys = kall_ref.shape[0]

    units = [(sub, g) for sub in range(ATT_NSUB) for g in range(Q_PER_KV)]
    assert len(units) % 2 == 0
    chunks = _key_chunks(n_keys)

    def q_of(ref, u):
        sub, g = units[u]
        return ref[sub * ATT_SUB:(sub + 1) * ATT_SUB, g * HEAD_DIM:(g + 1) * HEAD_DIM]

    def scores_chunk(q_u, slot, c0, cs, m_run):
        s_c = lax.dot_general(kall_ref[c0:c0 + cs, :], q_u, (((1,), (1,)), ((), ())),
                              preferred_element_type=F32)
        s_ref[slot, c0:c0 + cs, :] = s_c
        mc = jnp.max(s_c, axis=0, keepdims=True)
        return mc if m_run is None else jnp.maximum(m_run, mc)

    @pl.when(qi == 0)
    def _():
        kall_ref[0:n_lat, :] = k_ref[...]
        kall_ref[n_lat:, :] = kc_ref[...]
        vt_ref[:, 0:n_lat] = v_ref[...].astype(F32).T.astype(BF16)
        vt_ref[:, n_lat:] = vc_ref[...].astype(F32).T.astype(BF16)
        m0 = None
        for c0, cs in chunks:
            m0 = scores_chunk(q_of(q_ref, 0), 0, c0, cs, m0)
        m_ref[...] = m0

    m_cur = m_ref[...]
    for t in range(len(units)):
        last = t + 1 == len(units)
        q_nxt = q_of(qn_ref, 0) if last else q_of(q_ref, t + 1)
        m_nxt = l = o_acc = None
        for c0, cs in chunks:
            m_nxt = scores_chunk(q_nxt, (t + 1) % 2, c0, cs, m_nxt)
            p_c = jnp.exp2(s_ref[t % 2, c0:c0 + cs, :] - m_cur)
            lc = jnp.sum(p_c, axis=0, keepdims=True)
            l = lc if l is None else l + lc
            pv = _dot(vt_ref[:, c0:c0 + cs], p_c.astype(BF16))
            o_acc = pv if o_acc is None else o_acc + pv
        sub, g = units[t]
        o_ref[sub * ATT_SUB:(sub + 1) * ATT_SUB, g * HEAD_DIM:(g + 1) * HEAD_DIM] = (
            (o_acc * (1.0 / l)).T.astype(BF16))
        m_cur = m_nxt
    m_ref[...] = m_cur


def _attention(p, pc, batch, n_lat, n_ctx, weights):
    q_tiles = n_lat // ATT_TQ
    kcol = Q_END // HEAD_DIM
    vcol = K_END // HEAD_DIM
    n_keys = n_lat + n_ctx
    n_steps = batch * N_KV_HEADS * q_tiles
    slab_specs = []
    for w in weights:
        slab = w.shape[0] // n_steps
        assert slab * n_steps == w.shape[0] and slab % (2 * SUBLANES) == 0
        slab_specs.append(pl.BlockSpec(
            (slab, w.shape[1]), lambda b, h, qi: ((b * N_KV_HEADS + h) * q_tiles + qi, 0)))
    results = pl.pallas_call(
        functools.partial(_attn_kernel, n_cast=len(weights)),
        grid=(batch, N_KV_HEADS, q_tiles),
        in_specs=[
            pl.BlockSpec((ATT_TQ, Q_PER_KV * HEAD_DIM), lambda b, h, qi: (b * q_tiles + qi, h)),
            pl.BlockSpec((ATT_TQ, Q_PER_KV * HEAD_DIM),
                         lambda b, h, qi: (b * q_tiles + jnp.minimum(qi + 1, q_tiles - 1), h)),
            pl.BlockSpec((n_lat, HEAD_DIM), lambda b, h, qi: (b, kcol + h)),
            pl.BlockSpec((n_lat, HEAD_DIM), lambda b, h, qi: (b, vcol + h)),
            pl.BlockSpec((n_ctx, HEAD_DIM), lambda b, h, qi: (b, h)),
            pl.BlockSpec((n_ctx, HEAD_DIM), lambda b, h, qi: (b, N_KV_HEADS + h)),
        ] + slab_specs,
        out_specs=[pl.BlockSpec((ATT_TQ, Q_PER_KV * HEAD_DIM),
                                lambda b, h, qi: (b * q_tiles + qi, h))] + slab_specs,
        out_shape=[jax.ShapeDtypeStruct((batch * n_lat, ATTN_WIDTH), BF16)]
        + [jax.ShapeDtypeStruct(w.shape, BF16) for w in weights],
        scratch_shapes=[pltpu.VMEM((n_keys, HEAD_DIM), BF16), pltpu.VMEM((HEAD_DIM, n_keys), BF16),
                        pltpu.VMEM((2, n_keys, ATT_SUB), F32), pltpu.VMEM((1, ATT_SUB), F32)],
        compiler_params=_cparams(("arbitrary", "arbitrary", "arbitrary")),
        name="attn",
    )(p, p, p, p, pc, pc, *weights)
    return results[0], results[1:]


FOU_T = 256


def _fourier_kernel(u_ref, ct_ref, st_ref, cc_ref, sc_ref, jrev_ref, alt_ref, f_ref,
                    e_ref, o_ref, carry_ref, *, n_lat):
    s = pl.program_id(1)
    half = n_lat // 2
    tiles = half // FOU_T
    row = lax.broadcasted_iota(jnp.int32, (FOU_T, FOURIER_WIDTH), 0)

    def chan_dft(g_mat, h_mat):
        fd, fm = [], []
        for g in range(N_FOURIER_GROUPS):
            sl = slice(g * FOURIER_GROUP_DIM, (g + 1) * FOURIER_GROUP_DIM)
            gc = _dot(g_mat[:, sl].astype(BF16), cc_ref[...])
            if h_mat is None:
                fd.append(gc)
                continue
            hs = _dot(h_mat[:, sl].astype(BF16), sc_ref[...])
            fd.append(gc - hs)
            fm.append(gc + hs)
        return jnp.concatenate(fd, axis=1), (jnp.concatenate(fm, axis=1) if fm else None)

    @pl.when(s == 0)
    def _():
        for t in range(tiles):
            lo = u_ref[t * FOU_T:(t + 1) * FOU_T, :].astype(F32)
            hi_tile = u_ref[(2 * tiles - 1 - t) * FOU_T:(2 * tiles - t) * FOU_T, :]
            part = _dot(jrev_ref[...], hi_tile)
            if t == 0:
                first = jnp.zeros((1, FOURIER_WIDTH), F32)
            else:
                r0 = (2 * tiles - t) * FOU_T
                first = u_ref[r0:r0 + 1, :].astype(F32)
            part = jnp.where(row == 0, first, part)
            e_ref[t * FOU_T:(t + 1) * FOU_T, :] = (lo + part).astype(BF16)
            o_ref[t * FOU_T:(t + 1) * FOU_T, :] = (lo - part).astype(BF16)
        g_nyq = _dot(alt_ref[...], u_ref[...])
        f_nyq, _ = chan_dft(g_nyq, None)
        carry_ref[...] = f_nyq

    scale = n_lat ** -0.5
    u_half = u_ref[half:half + 1, :].astype(F32) * scale
    g_mat = _dot(ct_ref[...], e_ref[...]) + jnp.where(row % 2 == 0, u_half, -u_half)
    h_mat = _dot(st_ref[...], o_ref[...])
    fd, fm = chan_dft(g_mat, h_mat)
    f_ref[0, 0, 0] = fd.astype(BF16)
    fm_bf = fm.astype(BF16)
    mirrored = jnp.where(row == 0, carry_ref[0:1, :], _dot(jrev_ref[...], fm_bf))
    f_ref[0, 1, 0] = mirrored.astype(BF16)
    carry_ref[0:1, :] = fm_bf[0:1, :].astype(F32)


def _fourier(p, tables, batch, n_lat):
    ct, st, cc, sc, jrev, alt = tables
    ucol = V_END // FOURIER_WIDTH
    tiles = n_lat // 2 // FOU_T
    const2 = lambda b, s: (0, 0)
    return pl.pallas_call(
        functools.partial(_fourier_kernel, n_lat=n_lat),
        grid=(batch, tiles),
        in_specs=[
            pl.BlockSpec((n_lat, FOURIER_WIDTH), lambda b, s: (b, ucol)),
            pl.BlockSpec((FOU_T, n_lat // 2), lambda b, s: (tiles - 1 - s, 0)),
            pl.BlockSpec((FOU_T, n_lat // 2), lambda b, s: (tiles - 1 - s, 0)),
            pl.BlockSpec(cc.shape, const2),
            pl.BlockSpec(sc.shape, const2),
            pl.BlockSpec(jrev.shape, const2),
            pl.BlockSpec(alt.shape, const2),
        ],
        out_specs=pl.BlockSpec((1, 2, 1, FOU_T, FOURIER_WIDTH), lambda b, s: (b, 0, tiles - 1 - s, 0, 0)),
        out_shape=jax.ShapeDtypeStruct((batch, 2, tiles, FOU_T, FOURIER_WIDTH), BF16),
        scratch_shapes=[pltpu.VMEM((n_lat // 2, FOURIER_WIDTH), BF16),
                        pltpu.VMEM((n_lat // 2, FOURIER_WIDTH), BF16),
                        pltpu.VMEM((SUBLANES, FOURIER_WIDTH), F32)],
        compiler_params=_cparams(("arbitrary", "arbitrary")),
        name="fourier",
    )(p, ct, st, cc, sc, jrev, alt)


MRG_TM = FOU_T


def _merge_kernel(o_ref, f_ref, gates_ref, x_ref, mod_ref, wa_ref, wf_ref, wo_ref, x1_ref):
    a = _dot(o_ref[...], wa_ref[...])
    fm = _dot(f_ref[0, 0, 0], wf_ref[...])
    z = gates_ref[:, 0:D_MODEL].astype(F32) * a + gates_ref[:, D_MODEL:].astype(F32) * fm
    y = _dot(z.astype(BF16), wo_ref[...])
    x1_ref[...] = x_ref[...] + mod_ref[0, 2:3, :] * y


def _merge(o, f, p, x2, mod3, wa, wf, wo, n_lat):
    rows = x2.shape[0]
    tiles_per_batch = n_lat // MRG_TM
    const = lambda i: (0, 0)
    single = pl.Buffered(1)

    def f_index(i):
        t = i % tiles_per_batch
        upper = t // (tiles_per_batch // 2)
        return (i // tiles_per_batch, upper, jnp.where(upper == 1, tiles_per_batch - 1 - t, t), 0, 0)

    return pl.pallas_call(
        _merge_kernel,
        grid=(rows // MRG_TM,),
        in_specs=[
            pl.BlockSpec((MRG_TM, ATTN_WIDTH), lambda i: (i, 0)),
            pl.BlockSpec((1, 1, 1, MRG_TM, FOURIER_WIDTH), f_index),
            pl.BlockSpec((MRG_TM, 2 * D_MODEL), lambda i: (i, F_END // (2 * D_MODEL))),
            pl.BlockSpec((MRG_TM, D_MODEL), lambda i: (i, 0)),
            pl.BlockSpec((1, N_MOD, D_MODEL), lambda i: (i // tiles_per_batch, 0, 0)),
            pl.BlockSpec(wa.shape, const, pipeline_mode=single),
            pl.BlockSpec(wf.shape, const, pipeline_mode=single),
            pl.BlockSpec(wo.shape, const, pipeline_mode=single),
        ],
        out_specs=pl.BlockSpec((MRG_TM, D_MODEL), lambda i: (i, 0)),
        out_shape=jax.ShapeDtypeStruct((rows, D_MODEL), F32),
        compiler_params=_cparams(("arbitrary",)),
        name="merge",
    )(o, f, p, x2, mod3, wa, wf, wo)


MLP_TM = 512
MLP_TF = 1024


def _mlp_kernel(x1_ref, mod_ref, n2g_ref, w1_ref, w2_ref, fg_ref, out_ref, h_ref, acc_ref):
    kf = pl.program_id(1)

    @pl.when(kf == 0)
    def _():
        _norm_modulate(x1_ref, n2g_ref[...], mod_ref[0, 4:5, :], mod_ref[0, 3:4, :], h_ref)
        acc_ref[...] = jnp.zeros(acc_ref.shape, F32)

    hid = jnp.maximum(_dot(h_ref[...], w1_ref[...]), 0.0)
    acc_ref[...] += _dot((hid * hid).astype(BF16), w2_ref[...])

    @pl.when(kf == pl.num_programs(1) - 1)
    def _():
        x2 = x1_ref[...] + mod_ref[0, 5:6, :] * acc_ref[...]
        out_ref[...] = _rmsnorm_rows(x2, fg_ref[...])


def _mlp(x1, mod3, n2g, w1, w2, fg, n_lat):
    rows = x1.shape[0]
    tiles_per_batch = n_lat // MLP_TM
    return pl.pallas_call(
        _mlp_kernel,
        grid=(rows // MLP_TM, D_FF // MLP_TF),
        in_specs=[
            pl.BlockSpec((MLP_TM, D_MODEL), lambda i, kf: (i, 0)),
            pl.BlockSpec((1, N_MOD, D_MODEL), lambda i, kf: (i // tiles_per_batch, 0, 0)),
            pl.BlockSpec((1, D_MODEL), lambda i, kf: (0, 0)),
            pl.BlockSpec((D_MODEL, MLP_TF), lambda i, kf: (0, kf)),
            pl.BlockSpec((MLP_TF, D_MODEL), lambda i, kf: (kf, 0)),
            pl.BlockSpec((1, D_MODEL), lambda i, kf: (0, 0)),
        ],
        out_specs=pl.BlockSpec((MLP_TM, D_MODEL), lambda i, kf: (i, 0)),
        out_shape=jax.ShapeDtypeStruct((rows, D_MODEL), F32),
        scratch_shapes=[pltpu.VMEM((MLP_TM, D_MODEL), BF16), pltpu.VMEM((MLP_TM, D_MODEL), F32)],
        compiler_params=_cparams(("arbitrary", "arbitrary")),
        name="mlp",
    )(x1, mod3, n2g, w1, w2, fg)


def _rope_tables(rows):
    half = HEAD_DIM // 2
    inv = 1.0 / (ROPE_THETA ** ((jnp.arange(half, dtype=jnp.int32) // 2).astype(F32) / ROPE_AXIS_PAIRS))
    row_ang = jnp.arange(rows, dtype=F32)[:, None] * inv
    col_ang = jnp.arange(GRID_W, dtype=F32)[:, None] * inv

    def expand(fn):
        full = jnp.concatenate([
            jnp.broadcast_to(fn(row_ang)[:, None, :], (rows, GRID_W, half)),
            jnp.broadcast_to(fn(col_ang)[None, :, :], (rows, GRID_W, half))], axis=-1)
        full = full.reshape(rows * GRID_W, HEAD_DIM)
        return jnp.tile(full, (1, INP_SUB // HEAD_DIM))

    return jnp.stack([expand(jnp.cos), expand(jnp.sin)], axis=0)


def _dft_tables(size, period, scale):
    n_lo = 1 << ((size.bit_length() - 1) // 2)
    n_hi = size // n_lo
    assert n_lo * n_hi == size
    idx = jnp.arange(size, dtype=jnp.int32)

    def cos_sin(mult):
        ang = ((mult[:, None] * idx[None, :]) % period).astype(F32) * (2.0 * math.pi / period)
        return jnp.cos(ang), jnp.sin(ang)

    c_hi, s_hi = cos_sin(jnp.arange(n_hi, dtype=jnp.int32) * n_lo)
    c_lo, s_lo = cos_sin(jnp.arange(n_lo, dtype=jnp.int32))
    c_hi, s_hi = c_hi[:, None, :] * scale, s_hi[:, None, :] * scale
    c_lo, s_lo = c_lo[None, :, :], s_lo[None, :, :]
    cos = (c_hi * c_lo - s_hi * s_lo).reshape(size, size)
    sin = (s_hi * c_lo + c_hi * s_lo).reshape(size, size)
    return cos.astype(BF16), sin.astype(BF16)


def _fourier_tables(n_lat):
    half = n_lat // 2
    ct, st = _dft_tables(half, n_lat, n_lat ** -0.5)
    cc, sc = _dft_tables(FOURIER_GROUP_DIM, FOURIER_GROUP_DIM, FOURIER_GROUP_DIM ** -0.5)
    jrev = np.zeros((FOU_T, FOU_T), np.float32)
    jrev[np.arange(1, FOU_T), FOU_T - np.arange(1, FOU_T)] = 1.0
    alt = np.zeros((SUBLANES, n_lat), np.float32)
    alt[0] = (1.0 - 2.0 * (np.arange(n_lat) % 2)) * n_lat ** -0.5
    return ct, st, cc, sc, jnp.asarray(jrev, dtype=BF16), jnp.asarray(alt, dtype=BF16)


def kernel(x, c, ctx, c_ctx, w_ada, b_ada, norm1_g, w_in, q_norm_g, k_norm_g, w_attn_o, w_fourier,
           w_out, norm2_g, w1, w2, final_norm_g):
    batch, n_lat, _ = x.shape
    n_ctx = ctx.shape[1]
    assert w_ada.shape[0] == 1, "single-layer block"
    assert n_lat % INP_TM == 0 and n_ctx % SUBLANES == 0

    cc = jnp.concatenate([c, c_ctx[None, :]], axis=0)
    cb = jnp.broadcast_to(cc[:, :, None], (batch + 1, D_MODEL, LANES))
    mod = _adaln(cb, w_ada[0], b_ada)
    mod3 = mod.reshape(SUBLANES, N_MOD, D_MODEL)

    w_in_bf = w_in[0]
    n1g = norm1_g
    heads_per_sub = INP_SUB // HEAD_DIM
    qg, kg = jnp.tile(q_norm_g, (1, heads_per_sub)), jnp.tile(k_norm_g, (1, heads_per_sub))
    rope = _rope_tables(n_lat // GRID_W)

    x2 = x.reshape(batch * n_lat, D_MODEL)
    tiles_per_batch = n_lat // INP_TM
    p = _inproj(x2, mod3, lambda i: i // tiles_per_batch, n1g, w_in_bf, rope, qg, kg,
                tm=INP_TM, col_block0=0, n_col_blocks=IN_WIDTH // INP_TN, use_rope=True,
                rope_tiles=tiles_per_batch, name="inproj")
    ctx2 = ctx.reshape(batch * n_ctx, D_MODEL)
    pc = _inproj(ctx2, mod3, lambda i: batch, n1g, w_in_bf, rope, qg, kg,
                 tm=batch * n_ctx, col_block0=Q_END // INP_TN, n_col_blocks=(V_END - Q_END) // INP_TN,
                 use_rope=False, rope_tiles=1, name="ctxproj")

    o, (wa_bf, wf_bf, wo_bf, w1_bf, w2_bf) = _attention(
        p, pc, batch, n_lat, n_ctx, (w_attn_o[0], w_fourier[0], w_out[0], w1[0], w2[0]))

    f = _fourier(p, _fourier_tables(n_lat), batch, n_lat)

    x1 = _merge(o, f, p, x2, mod3, wa_bf, wf_bf, wo_bf, n_lat)
    out = _mlp(x1, mod3, norm2_g, w1_bf, w2_bf, final_norm_g[None, :], n_lat)
    return out.reshape(batch, n_lat, D_MODEL)
```

```python
import functools
import math

import jax
import jax.numpy as jnp
import numpy as np
from jax import lax
from jax.experimental import pallas as pl
from jax.experimental.pallas import tpu as pltpu

D_MODEL = 2048
GRID_W = 64
HEAD_DIM = 128
N_Q_HEADS = 16
N_KV_HEADS = 4
Q_PER_KV = N_Q_HEADS // N_KV_HEADS
ATTN_WIDTH = N_Q_HEADS * HEAD_DIM
KV_WIDTH = N_KV_HEADS * HEAD_DIM
N_FOURIER_GROUPS = 4
FOURIER_GROUP_DIM = 256
FOURIER_WIDTH = N_FOURIER_GROUPS * FOURIER_GROUP_DIM
Q_END = ATTN_WIDTH
K_END = Q_END + KV_WIDTH
V_END = K_END + KV_WIDTH
F_END = V_END + FOURIER_WIDTH
IN_WIDTH = F_END + 2 * D_MODEL
D_FF = 4 * D_MODEL
N_MOD = 6
ROPE_THETA = 10000.0
ROPE_AXIS_PAIRS = HEAD_DIM // 4
EPS = 1e-6

LANES = 128
SUBLANES = 8
VMEM_LIMIT = 56 * 1024 * 1024

BF16 = jnp.bfloat16
F32 = jnp.float32


def _cparams(sem):
    return pltpu.CompilerParams(dimension_semantics=sem, vmem_limit_bytes=VMEM_LIMIT)


def _dot(a, b):
    return jnp.dot(a, b, preferred_element_type=F32)


def _rmsnorm_rows(x, g):
    ms = jnp.mean(x * x, axis=-1, keepdims=True)
    return x * lax.rsqrt(ms + EPS) * g


NORM_ROWS = 128


def _norm_modulate(x_ref, g, scale, shift, h_ref):
    gs = g * (1.0 + scale)
    for r0 in range(0, x_ref.shape[0], NORM_ROWS):
        xa = x_ref[r0:r0 + NORM_ROWS, :]
        r = lax.rsqrt(jnp.mean(xa * xa, axis=-1, keepdims=True) + EPS)
        xb = x_ref[r0:r0 + NORM_ROWS, :]
        h_ref[r0:r0 + NORM_ROWS, :] = (xb * r * gs + shift).astype(BF16)


ADA_TN = 1024


def _adaln_kernel(cb_ref, w_ref, b_ref, out_ref, act_ref):
    n_rows = cb_ref.shape[0]

    @pl.when(pl.program_id(0) == 0)
    def _():
        for r in range(n_rows):
            cr = cb_ref[r]
            act_ref[r] = cr * jax.nn.sigmoid(cr)

    n_tiles = ADA_TN // LANES

    def slab(kc, accs):
        k0 = pl.multiple_of(kc * SUBLANES, SUBLANES)
        w8 = w_ref[pl.ds(k0, SUBLANES), :]
        a8 = [act_ref[r, pl.ds(k0, SUBLANES), :] for r in range(n_rows)]
        return tuple(accs[r * n_tiles + t] + w8[:, t * LANES:(t + 1) * LANES] * a8[r]
                     for r in range(n_rows) for t in range(n_tiles))

    zero = jnp.zeros((SUBLANES, LANES), F32)
    accs = lax.fori_loop(0, D_MODEL // SUBLANES, slab, (zero,) * (n_rows * n_tiles), unroll=4)
    out_ref[...] = jnp.zeros(out_ref.shape, F32)
    for r in range(n_rows):
        for t in range(n_tiles):
            tot = accs[r * n_tiles + t].sum(axis=0, keepdims=True)
            out_ref[r:r + 1, t * LANES:(t + 1) * LANES] = tot + b_ref[:, t * LANES:(t + 1) * LANES]


def _adaln(cb, w_ada, b_ada):
    n_out = w_ada.shape[1]
    return pl.pallas_call(
        _adaln_kernel,
        grid=(n_out // ADA_TN,),
        in_specs=[
            pl.BlockSpec(cb.shape, lambda j: (0, 0, 0)),
            pl.BlockSpec((D_MODEL, ADA_TN), lambda j: (0, j)),
            pl.BlockSpec((1, ADA_TN), lambda j: (0, j)),
        ],
        out_specs=pl.BlockSpec((SUBLANES, ADA_TN), lambda j: (0, j)),
        out_shape=jax.ShapeDtypeStruct((SUBLANES, n_out), F32),
        scratch_shapes=[pltpu.VMEM(cb.shape, F32)],
        compiler_params=_cparams(("arbitrary",)),
        name="adaln",
    )(cb, w_ada, b_ada)


INP_TM = 1024
INP_TN = 1024
INP_SUB = 256
Q_SCALE = HEAD_DIM ** -0.5 * math.log2(math.e)


def _head_norm_rope(acc, g, sel_ref, rope_ref, mult):
    xg = acc * g
    ssq = _dot((acc * acc).astype(BF16), sel_ref[0])
    r = lax.rsqrt(ssq * (1.0 / HEAD_DIM) + EPS) * mult
    if rope_ref is None:
        return xg * r
    swapped = _dot(xg.astype(BF16), sel_ref[1])
    heads = []
    for hh in range(INP_SUB // HEAD_DIM):
        sl = slice(hh * HEAD_DIM, (hh + 1) * HEAD_DIM)
        heads.append(xg[:, sl] * rope_ref[0] + swapped[:, sl] * rope_ref[1])
    return jnp.concatenate(heads, axis=1) * r


def _col_kind(col):
    if col < Q_END:
        return "q"
    if col < K_END:
        return "k"
    if col < F_END:
        return "plain"
    return "gate"


def _inproj_kernel(x_ref, mod_ref, n1g_ref, w_ref, rope_ref, qg_ref, kg_ref, sel_ref, out_ref, h_ref,
                   *, col_block0, n_col_blocks, use_rope):
    j = pl.program_id(1)

    @pl.when(j == 0)
    def _():
        _norm_modulate(x_ref, n1g_ref[...], mod_ref[0, 1:2, :], mod_ref[0, 0:1, :], h_ref)

    def epilogue(acc, kind, c0):
        if kind in ("q", "k"):
            g_ref, mult = (qg_ref, Q_SCALE) if kind == "q" else (kg_ref, 1.0)
            res = _head_norm_rope(acc, g_ref[...], sel_ref, rope_ref if use_rope else None, mult)
            out_ref[:, c0:c0 + INP_SUB] = res.astype(BF16)
        elif kind == "plain":
            out_ref[:, c0:c0 + INP_SUB] = acc.astype(BF16)
        else:
            out_ref[:, c0:c0 + INP_SUB] = (0.5 * jnp.tanh(0.5 * acc) + 0.5).astype(BF16)

    def tile(col_block):
        subs = range(0, INP_TN, INP_SUB)
        pending = None
        for c0 in subs:
            acc = _dot(h_ref[...], w_ref[:, c0:c0 + INP_SUB].astype(BF16))
            if pending is not None:
                epilogue(*pending)
            pending = (acc, _col_kind(col_block * INP_TN + c0), c0)
        epilogue(*pending)

    def kinds_of(cb):
        return tuple(_col_kind(cb * INP_TN + c0) for c0 in range(0, INP_TN, INP_SUB))

    lo = 0
    while lo < n_col_blocks:
        hi = lo
        while hi + 1 < n_col_blocks and kinds_of(col_block0 + hi + 1) == kinds_of(col_block0 + lo):
            hi += 1
        pl.when(jnp.logical_and(j >= lo, j <= hi))(functools.partial(tile, col_block0 + lo))
        lo = hi + 1


def _head_select_matrices():
    idx = np.arange(INP_SUB)
    ones_bd = (idx[:, None] // HEAD_DIM == idx[None, :] // HEAD_DIM).astype(np.float32)
    swap = np.zeros((INP_SUB, INP_SUB), np.float32)
    swap[idx[1::2], idx[0::2]] = -1.0
    swap[idx[0::2], idx[1::2]] = 1.0
    return jnp.asarray(np.stack([ones_bd, swap]), dtype=BF16)


def _inproj(x2, mod3, mod_row_of_tile, n1g, w_bf, rope, qg, kg, *, tm, col_block0, n_col_blocks,
            use_rope, rope_tiles, name):
    rows = x2.shape[0]
    sel = _head_select_matrices()
    kern = functools.partial(_inproj_kernel, col_block0=col_block0, n_col_blocks=n_col_blocks,
                             use_rope=use_rope)
    return pl.pallas_call(
        kern,
        grid=(rows // tm, n_col_blocks),
        in_specs=[
            pl.BlockSpec((tm, D_MODEL), lambda i, j: (i, 0)),
            pl.BlockSpec((1, N_MOD, D_MODEL), lambda i, j: (mod_row_of_tile(i), 0, 0)),
            pl.BlockSpec((1, D_MODEL), lambda i, j: (0, 0)),
            pl.BlockSpec((D_MODEL, INP_TN), lambda i, j: (0, j + col_block0)),
            pl.BlockSpec((2, tm, HEAD_DIM), lambda i, j: (0, i % rope_tiles, 0)),
            pl.BlockSpec((1, INP_SUB), lambda i, j: (0, 0)),
            pl.BlockSpec((1, INP_SUB), lambda i, j: (0, 0)),
            pl.BlockSpec(sel.shape, lambda i, j: (0, 0, 0)),
        ],
        out_specs=pl.BlockSpec((tm, INP_TN), lambda i, j: (i, j)),
        out_shape=jax.ShapeDtypeStruct((rows, n_col_blocks * INP_TN), BF16),
        scratch_shapes=[pltpu.VMEM((tm, D_MODEL), BF16)],
        compiler_params=_cparams(("arbitrary", "arbitrary")),
        name=name,
    )(x2, mod3, n1g, w_bf, rope, qg, kg, sel)


ATT_SUB = 256
ATT_NSUB = 2
ATT_TQ = ATT_SUB * ATT_NSUB
ATT_CHUNK = 1024


def _key_chunks(n_keys):
    chunks = [(c0, ATT_CHUNK) for c0 in range(0, n_keys - ATT_CHUNK + 1, ATT_CHUNK)]
    done = len(chunks) * ATT_CHUNK
    if done < n_keys:
        chunks.append((done, n_keys - done))
    return chunks


def _attn_kernel(q_ref, qn_ref, k_ref, v_ref, kc_ref, vc_ref, *rest, n_cast):
    w_refs, o_ref, wb_refs = rest[:n_cast], rest[n_cast], rest[n_cast + 1:2 * n_cast + 1]
    kall_ref, vt_ref, s_ref, m_ref = rest[2 * n_cast + 1:]
    for w_ref, wb_ref in zip(w_refs, wb_refs):
        wb_ref[...] = w_ref[...].astype(BF16)

    qi = pl.program_id(2)
    n_lat = k_ref.shape[0]
    n_keys = kall_ref.shape[0]

    units = [(sub, g) for sub in range(ATT_NSUB) for g in range(Q_PER_KV)]
    assert len(units) % 2 == 0
    chunks = _key_chunks(n_keys)

    def q_of(ref, u):
        sub, g = units[u]
        return ref[sub * ATT_SUB:(sub + 1) * ATT_SUB, g * HEAD_DIM:(g + 1) * HEAD_DIM]

    def scores_chunk(q_u, slot, c0, cs, m_run):
        s_c = lax.dot_general(kall_ref[c0:c0 + cs, :], q_u, (((1,), (1,)), ((), ())),
                              preferred_element_type=F32)
        s_ref[slot, c0:c0 + cs, :] = s_c
        mc = jnp.max(s_c, axis=0, keepdims=True)
        return mc if m_run is None else jnp.maximum(m_run, mc)

    @pl.when(qi == 0)
    def _():
        kall_ref[0:n_lat, :] = k_ref[...]
        kall_ref[n_lat:, :] = kc_ref[...]
        vt_ref[:, 0:n_lat] = v_ref[...].astype(F32).T.astype(BF16)
        vt_ref[:, n_lat:] = vc_ref[...].astype(F32).T.astype(BF16)
        m0 = None
        for c0, cs in chunks:
            m0 = scores_chunk(q_of(q_ref, 0), 0, c0, cs, m0)
        m_ref[...] = m0

    m_cur = m_ref[...]
    for t in range(len(units)):
        last = t + 1 == len(units)
        q_nxt = q_of(qn_ref, 0) if last else q_of(q_ref, t + 1)
        m_nxt = l = o_acc = None
        for c0, cs in chunks:
            m_nxt = scores_chunk(q_nxt, (t + 1) % 2, c0, cs, m_nxt)
            p_c = jnp.exp2(s_ref[t % 2, c0:c0 + cs, :] - m_cur)
            lc = jnp.sum(p_c, axis=0, keepdims=True)
            l = lc if l is None else l + lc
            pv = _dot(vt_ref[:, c0:c0 + cs], p_c.astype(BF16))
            o_acc = pv if o_acc is None else o_acc + pv
        sub, g = units[t]
        o_ref[sub * ATT_SUB:(sub + 1) * ATT_SUB, g * HEAD_DIM:(g + 1) * HEAD_DIM] = (
            (o_acc * (1.0 / l)).T.astype(BF16))
        m_cur = m_nxt
    m_ref[...] = m_cur


def _attention(p, pc, batch, n_lat, n_ctx, weights):
    q_tiles = n_lat // ATT_TQ
    kcol = Q_END // HEAD_DIM
    vcol = K_END // HEAD_DIM
    n_keys = n_lat + n_ctx
    n_steps = batch * N_KV_HEADS * q_tiles
    slab_specs = []
    for w in weights:
        slab = w.shape[0] // n_steps
        assert slab * n_steps == w.shape[0] and slab % (2 * SUBLANES) == 0
        slab_specs.append(pl.BlockSpec(
            (slab, w.shape[1]), lambda b, h, qi: ((b * N_KV_HEADS + h) * q_tiles + qi, 0)))
    results = pl.pallas_call(
        functools.partial(_attn_kernel, n_cast=len(weights)),
        grid=(batch, N_KV_HEADS, q_tiles),
        in_specs=[
            pl.BlockSpec((ATT_TQ, Q_PER_KV * HEAD_DIM), lambda b, h, qi: (b * q_tiles + qi, h)),
            pl.BlockSpec((ATT_TQ, Q_PER_KV * HEAD_DIM),
                         lambda b, h, qi: (b * q_tiles + jnp.minimum(qi + 1, q_tiles - 1), h)),
            pl.BlockSpec((n_lat, HEAD_DIM), lambda b, h, qi: (b, kcol + h)),
            pl.BlockSpec((n_lat, HEAD_DIM), lambda b, h, qi: (b, vcol + h)),
            pl.BlockSpec((n_ctx, HEAD_DIM), lambda b, h, qi: (b, h)),
            pl.BlockSpec((n_ctx, HEAD_DIM), lambda b, h, qi: (b, N_KV_HEADS + h)),
        ] + slab_specs,
        out_specs=[pl.BlockSpec((ATT_TQ, Q_PER_KV * HEAD_DIM),
                                lambda b, h, qi: (b * q_tiles + qi, h))] + slab_specs,
        out_shape=[jax.ShapeDtypeStruct((batch * n_lat, ATTN_WIDTH), BF16)]
        + [jax.ShapeDtypeStruct(w.shape, BF16) for w in weights],
        scratch_shapes=[pltpu.VMEM((n_keys, HEAD_DIM), BF16), pltpu.VMEM((HEAD_DIM, n_keys), BF16),
                        pltpu.VMEM((2, n_keys, ATT_SUB), F32), pltpu.VMEM((1, ATT_SUB), F32)],
        compiler_params=_cparams(("arbitrary", "arbitrary", "arbitrary")),
        name="attn",
    )(p, p, p, p, pc, pc, *weights)
    return results[0], results[1:]


FOU_T = 256


def _fourier_kernel(u_ref, ct_ref, st_ref, cc_ref, sc_ref, jrev_ref, alt_ref, f_ref,
                    e_ref, o_ref, carry_ref, *, n_lat):
    s = pl.program_id(1)
    half = n_lat // 2
    tiles = half // FOU_T
    row = lax.broadcasted_iota(jnp.int32, (FOU_T, FOURIER_WIDTH), 0)

    def chan_dft(g_mat, h_mat):
        fd, fm = [], []
        for g in range(N_FOURIER_GROUPS):
            sl = slice(g * FOURIER_GROUP_DIM, (g + 1) * FOURIER_GROUP_DIM)
            gc = _dot(g_mat[:, sl].astype(BF16), cc_ref[...])
            if h_mat is None:
                fd.append(gc)
                continue
            hs = _dot(h_mat[:, sl].astype(BF16), sc_ref[...])
            fd.append(gc - hs)
            fm.append(gc + hs)
        return jnp.concatenate(fd, axis=1), (jnp.concatenate(fm, axis=1) if fm else None)

    @pl.when(s == 0)
    def _():
        for t in range(tiles):
            lo = u_ref[t * FOU_T:(t + 1) * FOU_T, :].astype(F32)
            hi_tile = u_ref[(2 * tiles - 1 - t) * FOU_T:(2 * tiles - t) * FOU_T, :]
            part = _dot(jrev_ref[...], hi_tile)
            if t == 0:
                first = jnp.zeros((1, FOURIER_WIDTH), F32)
            else:
                r0 = (2 * tiles - t) * FOU_T
                first = u_ref[r0:r0 + 1, :].astype(F32)
            part = jnp.where(row == 0, first, part)
            e_ref[t * FOU_T:(t + 1) * FOU_T, :] = (lo + part).astype(BF16)
            o_ref[t * FOU_T:(t + 1) * FOU_T, :] = (lo - part).astype(BF16)
        g_nyq = _dot(alt_ref[...], u_ref[...])
        f_nyq, _ = chan_dft(g_nyq, None)
        carry_ref[...] = f_nyq

    scale = n_lat ** -0.5
    u_half = u_ref[half:half + 1, :].astype(F32) * scale
    g_mat = _dot(ct_ref[...], e_ref[...]) + jnp.where(row % 2 == 0, u_half, -u_half)
    h_mat = _dot(st_ref[...], o_ref[...])
    fd, fm = chan_dft(g_mat, h_mat)
    f_ref[0, 0, 0] = fd.astype(BF16)
    fm_bf = fm.astype(BF16)
    mirrored = jnp.where(row == 0, carry_ref[0:1, :], _dot(jrev_ref[...], fm_bf))
    f_ref[0, 1, 0] = mirrored.astype(BF16)
    carry_ref[0:1, :] = fm_bf[0:1, :].astype(F32)


def _fourier(p, tables, batch, n_lat):
    ct, st, cc, sc, jrev, alt = tables
    ucol = V_END // FOURIER_WIDTH
    tiles = n_lat // 2 // FOU_T
    const2 = lambda b, s: (0, 0)
    return pl.pallas_call(
        functools.partial(_fourier_kernel, n_lat=n_lat),
        grid=(batch, tiles),
        in_specs=[
            pl.BlockSpec((n_lat, FOURIER_WIDTH), lambda b, s: (b, ucol)),
            pl.BlockSpec((FOU_T, n_lat // 2), lambda b, s: (tiles - 1 - s, 0)),
            pl.BlockSpec((FOU_T, n_lat // 2), lambda b, s: (tiles - 1 - s, 0)),
            pl.BlockSpec(cc.shape, const2),
            pl.BlockSpec(sc.shape, const2),
            pl.BlockSpec(jrev.shape, const2),
            pl.BlockSpec(alt.shape, const2),
        ],
        out_specs=pl.BlockSpec((1, 2, 1, FOU_T, FOURIER_WIDTH), lambda b, s: (b, 0, tiles - 1 - s, 0, 0)),
        out_shape=jax.ShapeDtypeStruct((batch, 2, tiles, FOU_T, FOURIER_WIDTH), BF16),
        scratch_shapes=[pltpu.VMEM((n_lat // 2, FOURIER_WIDTH), BF16),
                        pltpu.VMEM((n_lat // 2, FOURIER_WIDTH), BF16),
                        pltpu.VMEM((SUBLANES, FOURIER_WIDTH), F32)],
        compiler_params=_cparams(("arbitrary", "arbitrary")),
        name="fourier",
    )(p, ct, st, cc, sc, jrev, alt)


MRG_TM = FOU_T


def _merge_kernel(o_ref, f_ref, gates_ref, x_ref, mod_ref, wa_ref, wf_ref, wo_ref, x1_ref):
    a = _dot(o_ref[...], wa_ref[...])
    fm = _dot(f_ref[0, 0, 0], wf_ref[...])
    z = gates_ref[:, 0:D_MODEL].astype(F32) * a + gates_ref[:, D_MODEL:].astype(F32) * fm
    y = _dot(z.astype(BF16), wo_ref[...])
    x1_ref[...] = x_ref[...] + mod_ref[0, 2:3, :] * y


def _merge(o, f, p, x2, mod3, wa, wf, wo, n_lat):
    rows = x2.shape[0]
    tiles_per_batch = n_lat // MRG_TM
    const = lambda i: (0, 0)
    single = pl.Buffered(1)

    def f_index(i):
        t = i % tiles_per_batch
        upper = t // (tiles_per_batch // 2)
        return (i // tiles_per_batch, upper, jnp.where(upper == 1, tiles_per_batch - 1 - t, t), 0, 0)

    return pl.pallas_call(
        _merge_kernel,
        grid=(rows // MRG_TM,),
        in_specs=[
            pl.BlockSpec((MRG_TM, ATTN_WIDTH), lambda i: (i, 0)),
            pl.BlockSpec((1, 1, 1, MRG_TM, FOURIER_WIDTH), f_index),
            pl.BlockSpec((MRG_TM, 2 * D_MODEL), lambda i: (i, F_END // (2 * D_MODEL))),
            pl.BlockSpec((MRG_TM, D_MODEL), lambda i: (i, 0)),
            pl.BlockSpec((1, N_MOD, D_MODEL), lambda i: (i // tiles_per_batch, 0, 0)),
            pl.BlockSpec(wa.shape, const, pipeline_mode=single),
            pl.BlockSpec(wf.shape, const, pipeline_mode=single),
            pl.BlockSpec(wo.shape, const, pipeline_mode=single),
        ],
        out_specs=pl.BlockSpec((MRG_TM, D_MODEL), lambda i: (i, 0)),
        out_shape=jax.ShapeDtypeStruct((rows, D_MODEL), F32),
        compiler_params=_cparams(("arbitrary",)),
        name="merge",
    )(o, f, p, x2, mod3, wa, wf, wo)


MLP_TM = 512
MLP_TF = 1024


def _mlp_kernel(x1_ref, mod_ref, n2g_ref, w1_ref, w2_ref, fg_ref, out_ref, h_ref, acc_ref):
    kf = pl.program_id(1)

    @pl.when(kf == 0)
    def _():
        _norm_modulate(x1_ref, n2g_ref[...], mod_ref[0, 4:5, :], mod_ref[0, 3:4, :], h_ref)
        acc_ref[...] = jnp.zeros(acc_ref.shape, F32)

    hid = jnp.maximum(_dot(h_ref[...], w1_ref[...]), 0.0)
    acc_ref[...] += _dot((hid * hid).astype(BF16), w2_ref[...])

    @pl.when(kf == pl.num_programs(1) - 1)
    def _():
        x2 = x1_ref[...] + mod_ref[0, 5:6, :] * acc_ref[...]
        out_ref[...] = _rmsnorm_rows(x2, fg_ref[...])


def _mlp(x1, mod3, n2g, w1, w2, fg, n_lat):
    rows = x1.shape[0]
    tiles_per_batch = n_lat // MLP_TM
    return pl.pallas_call(
        _mlp_kernel,
        grid=(rows // MLP_TM, D_FF // MLP_TF),
        in_specs=[
            pl.BlockSpec((MLP_TM, D_MODEL), lambda i, kf: (i, 0)),
            pl.BlockSpec((1, N_MOD, D_MODEL), lambda i, kf: (i // tiles_per_batch, 0, 0)),
            pl.BlockSpec((1, D_MODEL), lambda i, kf: (0, 0)),
            pl.BlockSpec((D_MODEL, MLP_TF), lambda i, kf: (0, kf)),
            pl.BlockSpec((MLP_TF, D_MODEL), lambda i, kf: (kf, 0)),
            pl.BlockSpec((1, D_MODEL), lambda i, kf: (0, 0)),
        ],
        out_specs=pl.BlockSpec((MLP_TM, D_MODEL), lambda i, kf: (i, 0)),
        out_shape=jax.ShapeDtypeStruct((rows, D_MODEL), F32),
        scratch_shapes=[pltpu.VMEM((MLP_TM, D_MODEL), BF16), pltpu.VMEM((MLP_TM, D_MODEL), F32)],
        compiler_params=_cparams(("arbitrary", "arbitrary")),
        name="mlp",
    )(x1, mod3, n2g, w1, w2, fg)


def _rope_tables(rows):
    pair = np.arange(HEAD_DIM) // 2
    inv = (np.float32(1.0) / np.float32(ROPE_THETA) ** (
        (pair % ROPE_AXIS_PAIRS).astype(np.float32) / np.float32(ROPE_AXIS_PAIRS))).astype(np.float32)
    pos = np.arange(rows * GRID_W)[:, None]
    coord = np.where(pair[None, :] < ROPE_AXIS_PAIRS, pos // GRID_W, pos % GRID_W).astype(np.float32)
    ang = coord * inv[None, :]
    return jnp.asarray(np.stack([np.cos(ang), np.sin(ang)]).astype(np.float32))


def _dft_tables(size, period, scale):
    n_lo = 1 << ((size.bit_length() - 1) // 2)
    n_hi = size // n_lo
    assert n_lo * n_hi == size
    idx = jnp.arange(size, dtype=jnp.int32)

    def cos_sin(mult):
        ang = ((mult[:, None] * idx[None, :]) % period).astype(F32) * (2.0 * math.pi / period)
        return jnp.cos(ang), jnp.sin(ang)

    c_hi, s_hi = cos_sin(jnp.arange(n_hi, dtype=jnp.int32) * n_lo)
    c_lo, s_lo = cos_sin(jnp.arange(n_lo, dtype=jnp.int32))
    c_hi, s_hi = c_hi[:, None, :] * scale, s_hi[:, None, :] * scale
    c_lo, s_lo = c_lo[None, :, :], s_lo[None, :, :]
    cos = (c_hi * c_lo - s_hi * s_lo).reshape(size, size)
    sin = (s_hi * c_lo + c_hi * s_lo).reshape(size, size)
    return cos.astype(BF16), sin.astype(BF16)


def _fourier_tables(n_lat):
    half = n_lat // 2
    ct, st = _dft_tables(half, n_lat, n_lat ** -0.5)
    cc, sc = _dft_tables(FOURIER_GROUP_DIM, FOURIER_GROUP_DIM, FOURIER_GROUP_DIM ** -0.5)
    jrev = np.zeros((FOU_T, FOU_T), np.float32)
    jrev[np.arange(1, FOU_T), FOU_T - np.arange(1, FOU_T)] = 1.0
    alt = np.zeros((SUBLANES, n_lat), np.float32)
    alt[0] = (1.0 - 2.0 * (np.arange(n_lat) % 2)) * n_lat ** -0.5
    return ct, st, cc, sc, jnp.asarray(jrev, dtype=BF16), jnp.asarray(alt, dtype=BF16)


def kernel(x, c, ctx, c_ctx, w_ada, b_ada, norm1_g, w_in, q_norm_g, k_norm_g, w_attn_o, w_fourier,
           w_out, norm2_g, w1, w2, final_norm_g):
    batch, n_lat, _ = x.shape
    n_ctx = ctx.shape[1]
    assert w_ada.shape[0] == 1, "single-layer block"
    assert n_lat % INP_TM == 0 and n_ctx % SUBLANES == 0

    cc = jnp.concatenate([c, c_ctx[None, :]], axis=0)
    cb = jnp.broadcast_to(cc[:, :, None], (batch + 1, D_MODEL, LANES))
    mod = _adaln(cb, w_ada[0], b_ada)
    mod3 = mod.reshape(SUBLANES, N_MOD, D_MODEL)

    w_in_bf = w_in[0]
    n1g = norm1_g
    heads_per_sub = INP_SUB // HEAD_DIM
    qg, kg = jnp.tile(q_norm_g, (1, heads_per_sub)), jnp.tile(k_norm_g, (1, heads_per_sub))
    rope = _rope_tables(n_lat // GRID_W)

    x2 = x.reshape(batch * n_lat, D_MODEL)
    tiles_per_batch = n_lat // INP_TM
    p = _inproj(x2, mod3, lambda i: i // tiles_per_batch, n1g, w_in_bf, rope, qg, kg,
                tm=INP_TM, col_block0=0, n_col_blocks=IN_WIDTH // INP_TN, use_rope=True,
                rope_tiles=tiles_per_batch, name="inproj")
    ctx2 = ctx.reshape(batch * n_ctx, D_MODEL)
    pc = _inproj(ctx2, mod3, lambda i: batch, n1g, w_in_bf, rope, qg, kg,
                 tm=batch * n_ctx, col_block0=Q_END // INP_TN, n_col_blocks=(V_END - Q_END) // INP_TN,
                 use_rope=False, rope_tiles=1, name="ctxproj")

    o, (wa_bf, wf_bf, wo_bf, w1_bf, w2_bf) = _attention(
        p, pc, batch, n_lat, n_ctx, (w_attn_o[0], w_fourier[0], w_out[0], w1[0], w2[0]))

    f = _fourier(p, _fourier_tables(n_lat), batch, n_lat)

    x1 = _merge(o, f, p, x2, mod3, wa_bf, wf_bf, wo_bf, n_lat)
    out = _mlp(x1, mod3, norm2_g, w1_bf, w2_bf, final_norm_g[None, :], n_lat)
    return out.reshape(batch, n_lat, D_MODEL)
```

```python
import functools
import math

import jax
import jax.numpy as jnp
import numpy as np
from jax import lax
from jax.experimental import pallas as pl
from jax.experimental.pallas import tpu as pltpu

D_MODEL = 2048
GRID_W = 64
HEAD_DIM = 128
N_Q_HEADS = 16
N_KV_HEADS = 4
Q_PER_KV = N_Q_HEADS // N_KV_HEADS
ATTN_WIDTH = N_Q_HEADS * HEAD_DIM
KV_WIDTH = N_KV_HEADS * HEAD_DIM
N_FOURIER_GROUPS = 4
FOURIER_GROUP_DIM = 256
FOURIER_WIDTH = N_FOURIER_GROUPS * FOURIER_GROUP_DIM
Q_END = ATTN_WIDTH
K_END = Q_END + KV_WIDTH
V_END = K_END + KV_WIDTH
F_END = V_END + FOURIER_WIDTH
IN_WIDTH = F_END + 2 * D_MODEL
D_FF = 4 * D_MODEL
N_MOD = 6
ROPE_THETA = 10000.0
ROPE_AXIS_PAIRS = HEAD_DIM // 4
EPS = 1e-6

LANES = 128
SUBLANES = 8
VMEM_LIMIT = 56 * 1024 * 1024

BF16 = jnp.bfloat16
F32 = jnp.float32


def _cparams(sem):
    return pltpu.CompilerParams(dimension_semantics=sem, vmem_limit_bytes=VMEM_LIMIT)


def _dot(a, b):
    return jnp.dot(a, b, preferred_element_type=F32)


def _rmsnorm_rows(x, g):
    ms = jnp.mean(x * x, axis=-1, keepdims=True)
    return x * lax.rsqrt(ms + EPS) * g


NORM_ROWS = 128


def _norm_modulate(x_ref, g, scale, shift, h_ref):
    gs = g * (1.0 + scale)
    for r0 in range(0, x_ref.shape[0], NORM_ROWS):
        xa = x_ref[r0:r0 + NORM_ROWS, :]
        r = lax.rsqrt(jnp.mean(xa * xa, axis=-1, keepdims=True) + EPS)
        xb = x_ref[r0:r0 + NORM_ROWS, :]
        h_ref[r0:r0 + NORM_ROWS, :] = (xb * r * gs + shift).astype(BF16)


ADA_TN = 1024


def _adaln_kernel(cb_ref, w_ref, b_ref, out_ref, act_ref):
    n_rows = cb_ref.shape[0]

    @pl.when(pl.program_id(0) == 0)
    def _():
        for r in range(n_rows):
            cr = cb_ref[r]
            act_ref[r] = cr * jax.nn.sigmoid(cr)

    n_tiles = ADA_TN // LANES

    def slab(kc, accs):
        k0 = pl.multiple_of(kc * SUBLANES, SUBLANES)
        w8 = w_ref[pl.ds(k0, SUBLANES), :]
        a8 = [act_ref[r, pl.ds(k0, SUBLANES), :] for r in range(n_rows)]
        return tuple(accs[r * n_tiles + t] + w8[:, t * LANES:(t + 1) * LANES] * a8[r]
                     for r in range(n_rows) for t in range(n_tiles))

    zero = jnp.zeros((SUBLANES, LANES), F32)
    accs = lax.fori_loop(0, D_MODEL // SUBLANES, slab, (zero,) * (n_rows * n_tiles), unroll=4)
    out_ref[...] = jnp.zeros(out_ref.shape, F32)
    for r in range(n_rows):
        for t in range(n_tiles):
            tot = accs[r * n_tiles + t].sum(axis=0, keepdims=True)
            out_ref[r:r + 1, t * LANES:(t + 1) * LANES] = tot + b_ref[:, t * LANES:(t + 1) * LANES]


def _adaln(cb, w_ada, b_ada):
    n_out = w_ada.shape[1]
    return pl.pallas_call(
        _adaln_kernel,
        grid=(n_out // ADA_TN,),
        in_specs=[
            pl.BlockSpec(cb.shape, lambda j: (0, 0, 0)),
            pl.BlockSpec((D_MODEL, ADA_TN), lambda j: (0, j)),
            pl.BlockSpec((1, ADA_TN), lambda j: (0, j)),
        ],
        out_specs=pl.BlockSpec((SUBLANES, ADA_TN), lambda j: (0, j)),
        out_shape=jax.ShapeDtypeStruct((SUBLANES, n_out), F32),
        scratch_shapes=[pltpu.VMEM(cb.shape, F32)],
        compiler_params=_cparams(("arbitrary",)),
        name="adaln",
    )(cb, w_ada, b_ada)


INP_TM = 1024
INP_TN = 1024
INP_SUB = 256
INP_GROUP = 2
Q_SCALE = HEAD_DIM ** -0.5 * math.log2(math.e)


def _head_norm_rope(acc, g, sel_ref, rope_ref, mult):
    xg = acc * g
    ssq = _dot((acc * acc).astype(BF16), sel_ref[0])
    r = lax.rsqrt(ssq * (1.0 / HEAD_DIM) + EPS) * mult
    if rope_ref is None:
        return xg * r
    swapped = _dot(xg.astype(BF16), sel_ref[1])
    heads = []
    for hh in range(INP_SUB // HEAD_DIM):
        sl = slice(hh * HEAD_DIM, (hh + 1) * HEAD_DIM)
        heads.append(xg[:, sl] * rope_ref[0] + swapped[:, sl] * rope_ref[1])
    return jnp.concatenate(heads, axis=1) * r


def _col_kind(col):
    if col < Q_END:
        return "q"
    if col < K_END:
        return "k"
    if col < F_END:
        return "plain"
    return "gate"


def _inproj_kernel(x_ref, mod_ref, n1g_ref, w_ref, rope_ref, qg_ref, kg_ref, sel_ref, out_ref, h_ref,
                   *, col_block0, n_col_blocks, use_rope):
    j = pl.program_id(1)
    s = pl.program_id(2)

    @pl.when(j == 0)
    def _():
        _norm_modulate(x_ref, n1g_ref[...], mod_ref[0, 1:2, :], mod_ref[0, 0:1, :], h_ref.at[s])

    def epilogue(acc, kind, c0):
        if kind in ("q", "k"):
            g_ref, mult = (qg_ref, Q_SCALE) if kind == "q" else (kg_ref, 1.0)
            res = _head_norm_rope(acc, g_ref[...], sel_ref, rope_ref if use_rope else None, mult)
            out_ref[:, c0:c0 + INP_SUB] = res.astype(BF16)
        elif kind == "plain":
            out_ref[:, c0:c0 + INP_SUB] = acc.astype(BF16)
        else:
            out_ref[:, c0:c0 + INP_SUB] = (0.5 * jnp.tanh(0.5 * acc) + 0.5).astype(BF16)

    def tile(col_block):
        subs = range(0, INP_TN, INP_SUB)
        pending = None
        for c0 in subs:
            acc = _dot(h_ref[s], w_ref[:, c0:c0 + INP_SUB].astype(BF16))
            if pending is not None:
                epilogue(*pending)
            pending = (acc, _col_kind(col_block * INP_TN + c0), c0)
        epilogue(*pending)

    def kinds_of(cb):
        return tuple(_col_kind(cb * INP_TN + c0) for c0 in range(0, INP_TN, INP_SUB))

    lo = 0
    while lo < n_col_blocks:
        hi = lo
        while hi + 1 < n_col_blocks and kinds_of(col_block0 + hi + 1) == kinds_of(col_block0 + lo):
            hi += 1
        pl.when(jnp.logical_and(j >= lo, j <= hi))(functools.partial(tile, col_block0 + lo))
        lo = hi + 1


def _head_select_matrices():
    idx = np.arange(INP_SUB)
    ones_bd = (idx[:, None] // HEAD_DIM == idx[None, :] // HEAD_DIM).astype(np.float32)
    swap = np.zeros((INP_SUB, INP_SUB), np.float32)
    swap[idx[1::2], idx[0::2]] = -1.0
    swap[idx[0::2], idx[1::2]] = 1.0
    return jnp.asarray(np.stack([ones_bd, swap]), dtype=BF16)


def _inproj(x2, mod3, mod_row_of_tile, n1g, w_in, rope, qg, kg, *, tm, group, col_block0,
            n_col_blocks, use_rope, rope_tiles, name):
    rows = x2.shape[0]
    sel = _head_select_matrices()
    kern = functools.partial(_inproj_kernel, col_block0=col_block0, n_col_blocks=n_col_blocks,
                             use_rope=use_rope)
    assert rows % (tm * group) == 0

    def tile_of(g, s):
        return g * group + s

    def x_index(g, j, s):
        return (tile_of(g, jnp.where(j == 0, s, group - 1)), 0)

    return pl.pallas_call(
        kern,
        grid=(rows // (tm * group), n_col_blocks, group),
        in_specs=[
            pl.BlockSpec((tm, D_MODEL), x_index),
            pl.BlockSpec((1, N_MOD, D_MODEL), lambda g, j, s: (mod_row_of_tile(tile_of(g, s)), 0, 0)),
            pl.BlockSpec((1, D_MODEL), lambda g, j, s: (0, 0)),
            pl.BlockSpec((D_MODEL, INP_TN), lambda g, j, s: (0, j + col_block0)),
            pl.BlockSpec((2, tm, HEAD_DIM), lambda g, j, s: (0, tile_of(g, s) % rope_tiles, 0)),
            pl.BlockSpec((1, INP_SUB), lambda g, j, s: (0, 0)),
            pl.BlockSpec((1, INP_SUB), lambda g, j, s: (0, 0)),
            pl.BlockSpec(sel.shape, lambda g, j, s: (0, 0, 0)),
        ],
        out_specs=pl.BlockSpec((tm, INP_TN), lambda g, j, s: (tile_of(g, s), j)),
        out_shape=jax.ShapeDtypeStruct((rows, n_col_blocks * INP_TN), BF16),
        scratch_shapes=[pltpu.VMEM((group, tm, D_MODEL), BF16)],
        compiler_params=_cparams(("arbitrary", "arbitrary", "arbitrary")),
        name=name,
    )(x2, mod3, n1g, w_in, rope, qg, kg, sel)


ATT_SUB = 256
ATT_NSUB = 2
ATT_TQ = ATT_SUB * ATT_NSUB
ATT_CHUNK = 1024


def _key_chunks(n_keys):
    chunks = [(c0, ATT_CHUNK) for c0 in range(0, n_keys - ATT_CHUNK + 1, ATT_CHUNK)]
    done = len(chunks) * ATT_CHUNK
    if done < n_keys:
        chunks.append((done, n_keys - done))
    return chunks


def _attn_kernel(q_ref, qn_ref, k_ref, v_ref, kc_ref, vc_ref, *rest, n_cast):
    w_refs, o_ref, wb_refs = rest[:n_cast], rest[n_cast], rest[n_cast + 1:2 * n_cast + 1]
    kall_ref, vt_ref, s_ref, m_ref = rest[2 * n_cast + 1:]
    for w_ref, wb_ref in zip(w_refs, wb_refs):
        wb_ref[...] = w_ref[...].astype(BF16)

    qi = pl.program_id(2)
    n_lat = k_ref.shape[0]
    n_keys = kall_ref.shape[0]

    units = [(sub, g) for sub in range(ATT_NSUB) for g in range(Q_PER_KV)]
    assert len(units) % 2 == 0
    chunks = _key_chunks(n_keys)

    def q_of(ref, u):
        sub, g = units[u]
        return ref[sub * ATT_SUB:(sub + 1) * ATT_SUB, g * HEAD_DIM:(g + 1) * HEAD_DIM]

    def scores_chunk(q_u, slot, c0, cs, m_run):
        s_c = lax.dot_general(kall_ref[c0:c0 + cs, :], q_u, (((1,), (1,)), ((), ())),
                              preferred_element_type=F32)
        s_ref[slot, c0:c0 + cs, :] = s_c
        mc = jnp.max(s_c, axis=0, keepdims=True)
        return mc if m_run is None else jnp.maximum(m_run, mc)

    @pl.when(qi == 0)
    def _():
        kall_ref[0:n_lat, :] = k_ref[...]
        kall_ref[n_lat:, :] = kc_ref[...]
        vt_ref[:, 0:n_lat] = v_ref[...].astype(F32).T.astype(BF16)
        vt_ref[:, n_lat:] = vc_ref[...].astype(F32).T.astype(BF16)
        m0 = None
        for c0, cs in chunks:
            m0 = scores_chunk(q_of(q_ref, 0), 0, c0, cs, m0)
        m_ref[...] = m0

    m_cur = m_ref[...]
    for t in range(len(units)):
        last = t + 1 == len(units)
        q_nxt = q_of(qn_ref, 0) if last else q_of(q_ref, t + 1)
        m_nxt = l = o_acc = None
        for c0, cs in chunks:
            m_nxt = scores_chunk(q_nxt, (t + 1) % 2, c0, cs, m_nxt)
            p_c = jnp.exp2(s_ref[t % 2, c0:c0 + cs, :] - m_cur)
            lc = jnp.sum(p_c, axis=0, keepdims=True)
            l = lc if l is None else l + lc
            pv = _dot(vt_ref[:, c0:c0 + cs], p_c.astype(BF16))
            o_acc = pv if o_acc is None else o_acc + pv
        sub, g = units[t]
        o_ref[sub * ATT_SUB:(sub + 1) * ATT_SUB, g * HEAD_DIM:(g + 1) * HEAD_DIM] = (
            (o_acc * (1.0 / l)).T.astype(BF16))
        m_cur = m_nxt
    m_ref[...] = m_cur


def _attention(p, pc, batch, n_lat, n_ctx, weights):
    q_tiles = n_lat // ATT_TQ
    kcol = Q_END // HEAD_DIM
    vcol = K_END // HEAD_DIM
    n_keys = n_lat + n_ctx
    n_steps = batch * N_KV_HEADS * q_tiles
    slab_specs = []
    for w in weights:
        slab = w.shape[0] // n_steps
        assert slab * n_steps == w.shape[0] and slab % (2 * SUBLANES) == 0
        slab_specs.append(pl.BlockSpec(
            (slab, w.shape[1]), lambda b, h, qi: ((b * N_KV_HEADS + h) * q_tiles + qi, 0)))
    results = pl.pallas_call(
        functools.partial(_attn_kernel, n_cast=len(weights)),
        grid=(batch, N_KV_HEADS, q_tiles),
        in_specs=[
            pl.BlockSpec((ATT_TQ, Q_PER_KV * HEAD_DIM), lambda b, h, qi: (b * q_tiles + qi, h)),
            pl.BlockSpec((ATT_TQ, Q_PER_KV * HEAD_DIM),
                         lambda b, h, qi: (b * q_tiles + jnp.minimum(qi + 1, q_tiles - 1), h)),
            pl.BlockSpec((n_lat, HEAD_DIM), lambda b, h, qi: (b, kcol + h)),
            pl.BlockSpec((n_lat, HEAD_DIM), lambda b, h, qi: (b, vcol + h)),
            pl.BlockSpec((n_ctx, HEAD_DIM), lambda b, h, qi: (b, h)),
            pl.BlockSpec((n_ctx, HEAD_DIM), lambda b, h, qi: (b, N_KV_HEADS + h)),
        ] + slab_specs,
        out_specs=[pl.BlockSpec((ATT_TQ, Q_PER_KV * HEAD_DIM),
                                lambda b, h, qi: (b * q_tiles + qi, h))] + slab_specs,
        out_shape=[jax.ShapeDtypeStruct((batch * n_lat, ATTN_WIDTH), BF16)]
        + [jax.ShapeDtypeStruct(w.shape, BF16) for w in weights],
        scratch_shapes=[pltpu.VMEM((n_keys, HEAD_DIM), BF16), pltpu.VMEM((HEAD_DIM, n_keys), BF16),
                        pltpu.VMEM((2, n_keys, ATT_SUB), F32), pltpu.VMEM((1, ATT_SUB), F32)],
        compiler_params=_cparams(("arbitrary", "arbitrary", "arbitrary")),
        name="attn",
    )(p, p, p, p, pc, pc, *weights)
    return results[0], results[1:]


FOU_T = 256


def _fourier_kernel(u_ref, ct_ref, st_ref, cc_ref, sc_ref, jrev_ref, alt_ref, f_ref,
                    e_ref, o_ref, carry_ref, *, n_lat):
    s = pl.program_id(1)
    half = n_lat // 2
    tiles = half // FOU_T
    row = lax.broadcasted_iota(jnp.int32, (FOU_T, FOURIER_WIDTH), 0)

    def chan_dft(g_mat, h_mat):
        fd, fm = [], []
        for g in range(N_FOURIER_GROUPS):
            sl = slice(g * FOURIER_GROUP_DIM, (g + 1) * FOURIER_GROUP_DIM)
            gc = _dot(g_mat[:, sl].astype(BF16), cc_ref[...])
            if h_mat is None:
                fd.append(gc)
                continue
            hs = _dot(h_mat[:, sl].astype(BF16), sc_ref[...])
            fd.append(gc - hs)
            fm.append(gc + hs)
        return jnp.concatenate(fd, axis=1), (jnp.concatenate(fm, axis=1) if fm else None)

    @pl.when(s == 0)
    def _():
        for t in range(tiles):
            lo = u_ref[t * FOU_T:(t + 1) * FOU_T, :].astype(F32)
            hi_tile = u_ref[(2 * tiles - 1 - t) * FOU_T:(2 * tiles - t) * FOU_T, :]
            part = _dot(jrev_ref[...], hi_tile)
            if t == 0:
                first = jnp.zeros((1, FOURIER_WIDTH), F32)
            else:
                r0 = (2 * tiles - t) * FOU_T
                first = u_ref[r0:r0 + 1, :].astype(F32)
            part = jnp.where(row == 0, first, part)
            e_ref[t * FOU_T:(t + 1) * FOU_T, :] = (lo + part).astype(BF16)
            o_ref[t * FOU_T:(t + 1) * FOU_T, :] = (lo - part).astype(BF16)
        g_nyq = _dot(alt_ref[...], u_ref[...])
        f_nyq, _ = chan_dft(g_nyq, None)
        carry_ref[...] = f_nyq

    scale = n_lat ** -0.5
    u_half = u_ref[half:half + 1, :].astype(F32) * scale
    g_mat = _dot(ct_ref[...], e_ref[...]) + jnp.where(row % 2 == 0, u_half, -u_half)
    h_mat = _dot(st_ref[...], o_ref[...])
    fd, fm = chan_dft(g_mat, h_mat)
    f_ref[0, 0, 0] = fd.astype(BF16)
    fm_bf = fm.astype(BF16)
    mirrored = jnp.where(row == 0, carry_ref[0:1, :], _dot(jrev_ref[...], fm_bf))
    f_ref[0, 1, 0] = mirrored.astype(BF16)
    carry_ref[0:1, :] = fm_bf[0:1, :].astype(F32)


def _fourier(p, tables, batch, n_lat):
    ct, st, cc, sc, jrev, alt = tables
    ucol = V_END // FOURIER_WIDTH
    tiles = n_lat // 2 // FOU_T
    const2 = lambda b, s: (0, 0)
    return pl.pallas_call(
        functools.partial(_fourier_kernel, n_lat=n_lat),
        grid=(batch, tiles),
        in_specs=[
            pl.BlockSpec((n_lat, FOURIER_WIDTH), lambda b, s: (b, ucol)),
            pl.BlockSpec((FOU_T, n_lat // 2), lambda b, s: (tiles - 1 - s, 0)),
            pl.BlockSpec((FOU_T, n_lat // 2), lambda b, s: (tiles - 1 - s, 0)),
            pl.BlockSpec(cc.shape, const2),
            pl.BlockSpec(sc.shape, const2),
            pl.BlockSpec(jrev.shape, const2),
            pl.BlockSpec(alt.shape, const2),
        ],
        out_specs=pl.BlockSpec((1, 2, 1, FOU_T, FOURIER_WIDTH), lambda b, s: (b, 0, tiles - 1 - s, 0, 0)),
        out_shape=jax.ShapeDtypeStruct((batch, 2, tiles, FOU_T, FOURIER_WIDTH), BF16),
        scratch_shapes=[pltpu.VMEM((n_lat // 2, FOURIER_WIDTH), BF16),
                        pltpu.VMEM((n_lat // 2, FOURIER_WIDTH), BF16),
                        pltpu.VMEM((SUBLANES, FOURIER_WIDTH), F32)],
        compiler_params=_cparams(("arbitrary", "arbitrary")),
        name="fourier",
    )(p, ct, st, cc, sc, jrev, alt)


MRG_TM = FOU_T


def _merge_kernel(o_ref, f_ref, gates_ref, x_ref, mod_ref, wa_ref, wf_ref, wo_ref, x1_ref):
    a = _dot(o_ref[...], wa_ref[...])
    fm = _dot(f_ref[0, 0, 0], wf_ref[...])
    z = gates_ref[:, 0:D_MODEL].astype(F32) * a + gates_ref[:, D_MODEL:].astype(F32) * fm
    y = _dot(z.astype(BF16), wo_ref[...])
    x1_ref[...] = x_ref[...] + mod_ref[0, 2:3, :] * y


def _merge(o, f, p, x2, mod3, wa, wf, wo, n_lat):
    rows = x2.shape[0]
    tiles_per_batch = n_lat // MRG_TM
    const = lambda i: (0, 0)
    single = pl.Buffered(1)

    def f_index(i):
        t = i % tiles_per_batch
        upper = t // (tiles_per_batch // 2)
        return (i // tiles_per_batch, upper, jnp.where(upper == 1, tiles_per_batch - 1 - t, t), 0, 0)

    return pl.pallas_call(
        _merge_kernel,
        grid=(rows // MRG_TM,),
        in_specs=[
            pl.BlockSpec((MRG_TM, ATTN_WIDTH), lambda i: (i, 0)),
            pl.BlockSpec((1, 1, 1, MRG_TM, FOURIER_WIDTH), f_index),
            pl.BlockSpec((MRG_TM, 2 * D_MODEL), lambda i: (i, F_END // (2 * D_MODEL))),
            pl.BlockSpec((MRG_TM, D_MODEL), lambda i: (i, 0)),
            pl.BlockSpec((1, N_MOD, D_MODEL), lambda i: (i // tiles_per_batch, 0, 0)),
            pl.BlockSpec(wa.shape, const, pipeline_mode=single),
            pl.BlockSpec(wf.shape, const, pipeline_mode=single),
            pl.BlockSpec(wo.shape, const, pipeline_mode=single),
        ],
        out_specs=pl.BlockSpec((MRG_TM, D_MODEL), lambda i: (i, 0)),
        out_shape=jax.ShapeDtypeStruct((rows, D_MODEL), F32),
        compiler_params=_cparams(("arbitrary",)),
        name="merge",
    )(o, f, p, x2, mod3, wa, wf, wo)


MLP_TM = 512
MLP_TF = 1024


def _mlp_kernel(x1_ref, mod_ref, n2g_ref, w1_ref, w2_ref, fg_ref, out_ref, h_ref, acc_ref):
    kf = pl.program_id(1)

    @pl.when(kf == 0)
    def _():
        _norm_modulate(x1_ref, n2g_ref[...], mod_ref[0, 4:5, :], mod_ref[0, 3:4, :], h_ref)
        acc_ref[...] = jnp.zeros(acc_ref.shape, F32)

    hid = jnp.maximum(_dot(h_ref[...], w1_ref[...]), 0.0)
    acc_ref[...] += _dot((hid * hid).astype(BF16), w2_ref[...])

    @pl.when(kf == pl.num_programs(1) - 1)
    def _():
        x2 = x1_ref[...] + mod_ref[0, 5:6, :] * acc_ref[...]
        out_ref[...] = _rmsnorm_rows(x2, fg_ref[...])


def _mlp(x1, mod3, n2g, w1, w2, fg, n_lat):
    rows = x1.shape[0]
    tiles_per_batch = n_lat // MLP_TM
    return pl.pallas_call(
        _mlp_kernel,
        grid=(rows // MLP_TM, D_FF // MLP_TF),
        in_specs=[
            pl.BlockSpec((MLP_TM, D_MODEL), lambda i, kf: (i, 0)),
            pl.BlockSpec((1, N_MOD, D_MODEL), lambda i, kf: (i // tiles_per_batch, 0, 0)),
            pl.BlockSpec((1, D_MODEL), lambda i, kf: (0, 0)),
            pl.BlockSpec((D_MODEL, MLP_TF), lambda i, kf: (0, kf)),
            pl.BlockSpec((MLP_TF, D_MODEL), lambda i, kf: (kf, 0)),
            pl.BlockSpec((1, D_MODEL), lambda i, kf: (0, 0)),
        ],
        out_specs=pl.BlockSpec((MLP_TM, D_MODEL), lambda i, kf: (i, 0)),
        out_shape=jax.ShapeDtypeStruct((rows, D_MODEL), F32),
        scratch_shapes=[pltpu.VMEM((MLP_TM, D_MODEL), BF16), pltpu.VMEM((MLP_TM, D_MODEL), F32)],
        compiler_params=_cparams(("arbitrary", "arbitrary")),
        name="mlp",
    )(x1, mod3, n2g, w1, w2, fg)


def _rope_tables(rows):
    pair = np.arange(HEAD_DIM) // 2
    inv = (np.float32(1.0) / np.float32(ROPE_THETA) ** (
        (pair % ROPE_AXIS_PAIRS).astype(np.float32) / np.float32(ROPE_AXIS_PAIRS))).astype(np.float32)
    pos = np.arange(rows * GRID_W)[:, None]
    coord = np.where(pair[None, :] < ROPE_AXIS_PAIRS, pos // GRID_W, pos % GRID_W).astype(np.float32)
    ang = coord * inv[None, :]
    return jnp.asarray(np.stack([np.cos(ang), np.sin(ang)]).astype(np.float32))


def _dft_tables(size, period, scale):
    n_lo = 1 << ((size.bit_length() - 1) // 2)
    n_hi = size // n_lo
    assert n_lo * n_hi == size
    idx = jnp.arange(size, dtype=jnp.int32)

    def cos_sin(mult):
        ang = ((mult[:, None] * idx[None, :]) % period).astype(F32) * (2.0 * math.pi / period)
        return jnp.cos(ang), jnp.sin(ang)

    c_hi, s_hi = cos_sin(jnp.arange(n_hi, dtype=jnp.int32) * n_lo)
    c_lo, s_lo = cos_sin(jnp.arange(n_lo, dtype=jnp.int32))
    c_hi, s_hi = c_hi[:, None, :] * scale, s_hi[:, None, :] * scale
    c_lo, s_lo = c_lo[None, :, :], s_lo[None, :, :]
    cos = (c_hi * c_lo - s_hi * s_lo).reshape(size, size)
    sin = (s_hi * c_lo + c_hi * s_lo).reshape(size, size)
    return cos.astype(BF16), sin.astype(BF16)


def _fourier_tables(n_lat):
    half = n_lat // 2
    ct, st = _dft_tables(half, n_lat, n_lat ** -0.5)
    cc, sc = _dft_tables(FOURIER_GROUP_DIM, FOURIER_GROUP_DIM, FOURIER_GROUP_DIM ** -0.5)
    jrev = np.zeros((FOU_T, FOU_T), np.float32)
    jrev[np.arange(1, FOU_T), FOU_T - np.arange(1, FOU_T)] = 1.0
    alt = np.zeros((SUBLANES, n_lat), np.float32)
    alt[0] = (1.0 - 2.0 * (np.arange(n_lat) % 2)) * n_lat ** -0.5
    return ct, st, cc, sc, jnp.asarray(jrev, dtype=BF16), jnp.asarray(alt, dtype=BF16)


def kernel(x, c, ctx, c_ctx, w_ada, b_ada, norm1_g, w_in, q_norm_g, k_norm_g, w_attn_o, w_fourier,
           w_out, norm2_g, w1, w2, final_norm_g):
    batch, n_lat, _ = x.shape
    n_ctx = ctx.shape[1]
    assert w_ada.shape[0] == 1, "single-layer block"
    assert n_lat % INP_TM == 0 and n_ctx % SUBLANES == 0

    cc = jnp.concatenate([c, c_ctx[None, :]], axis=0)
    cb = jnp.broadcast_to(cc[:, :, None], (batch + 1, D_MODEL, LANES))
    mod = _adaln(cb, w_ada[0], b_ada)
    mod3 = mod.reshape(SUBLANES, N_MOD, D_MODEL)

    w_in_bf = w_in[0]
    n1g = norm1_g
    heads_per_sub = INP_SUB // HEAD_DIM
    qg, kg = jnp.tile(q_norm_g, (1, heads_per_sub)), jnp.tile(k_norm_g, (1, heads_per_sub))
    rope = _rope_tables(n_lat // GRID_W)

    x2 = x.reshape(batch * n_lat, D_MODEL)
    tiles_per_batch = n_lat // INP_TM
    p = _inproj(x2, mod3, lambda i: i // tiles_per_batch, n1g, w_in_bf, rope, qg, kg,
                tm=INP_TM, group=INP_GROUP, col_block0=0, n_col_blocks=IN_WIDTH // INP_TN, use_rope=True,
                rope_tiles=tiles_per_batch, name="inproj")
    ctx2 = ctx.reshape(batch * n_ctx, D_MODEL)
    pc = _inproj(ctx2, mod3, lambda i: batch, n1g, w_in_bf, rope, qg, kg,
                 tm=batch * n_ctx, group=1, col_block0=Q_END // INP_TN,
                 n_col_blocks=(V_END - Q_END) // INP_TN,
                 use_rope=False, rope_tiles=1, name="ctxproj")

    o, (wa_bf, wf_bf, wo_bf, w1_bf, w2_bf) = _attention(
        p, pc, batch, n_lat, n_ctx, (w_attn_o[0], w_fourier[0], w_out[0], w1[0], w2[0]))

    f = _fourier(p, _fourier_tables(n_lat), batch, n_lat)

    x1 = _merge(o, f, p, x2, mod3, wa_bf, wf_bf, wo_bf, n_lat)
    out = _mlp(x1, mod3, norm2_g, w1_bf, w2_bf, final_norm_g[None, :], n_lat)
    return out.reshape(batch, n_lat, D_MODEL)
```

```python
import functools
import math

import jax
import jax.numpy as jnp
import numpy as np
from jax import lax
from jax.experimental import pallas as pl
from jax.experimental.pallas import tpu as pltpu

D_MODEL = 2048
GRID_W = 64
HEAD_DIM = 128
N_Q_HEADS = 16
N_KV_HEADS = 4
Q_PER_KV = N_Q_HEADS // N_KV_HEADS
ATTN_WIDTH = N_Q_HEADS * HEAD_DIM
KV_WIDTH = N_KV_HEADS * HEAD_DIM
N_FOURIER_GROUPS = 4
FOURIER_GROUP_DIM = 256
FOURIER_WIDTH = N_FOURIER_GROUPS * FOURIER_GROUP_DIM
Q_END = ATTN_WIDTH
K_END = Q_END + KV_WIDTH
V_END = K_END + KV_WIDTH
F_END = V_END + FOURIER_WIDTH
IN_WIDTH = F_END + 2 * D_MODEL
D_FF = 4 * D_MODEL
N_MOD = 6
ROPE_THETA = 10000.0
ROPE_AXIS_PAIRS = HEAD_DIM // 4
EPS = 1e-6

LANES = 128
SUBLANES = 8
VMEM_LIMIT = 56 * 1024 * 1024

BF16 = jnp.bfloat16
F32 = jnp.float32


def _cparams(sem):
    return pltpu.CompilerParams(dimension_semantics=sem, vmem_limit_bytes=VMEM_LIMIT)


def _dot(a, b):
    return jnp.dot(a, b, preferred_element_type=F32)


def _rmsnorm_rows(x, g):
    ms = jnp.mean(x * x, axis=-1, keepdims=True)
    return x * lax.rsqrt(ms + EPS) * g


NORM_ROWS = 128


def _norm_modulate(x_ref, g, scale, shift, h_ref):
    gs = g * (1.0 + scale)
    for r0 in range(0, x_ref.shape[0], NORM_ROWS):
        xa = x_ref[r0:r0 + NORM_ROWS, :]
        r = lax.rsqrt(jnp.mean(xa * xa, axis=-1, keepdims=True) + EPS)
        xb = x_ref[r0:r0 + NORM_ROWS, :]
        h_ref[r0:r0 + NORM_ROWS, :] = (xb * r * gs + shift).astype(BF16)


ADA_TN = 1024


def _adaln_kernel(cb_ref, w_ref, b_ref, out_ref, act_ref):
    n_rows = cb_ref.shape[0]

    @pl.when(pl.program_id(0) == 0)
    def _():
        for r in range(n_rows):
            cr = cb_ref[r]
            act_ref[r] = cr * jax.nn.sigmoid(cr)

    n_tiles = ADA_TN // LANES

    def slab(kc, accs):
        k0 = pl.multiple_of(kc * SUBLANES, SUBLANES)
        w8 = w_ref[pl.ds(k0, SUBLANES), :]
        a8 = [act_ref[r, pl.ds(k0, SUBLANES), :] for r in range(n_rows)]
        return tuple(accs[r * n_tiles + t] + w8[:, t * LANES:(t + 1) * LANES] * a8[r]
                     for r in range(n_rows) for t in range(n_tiles))

    zero = jnp.zeros((SUBLANES, LANES), F32)
    accs = lax.fori_loop(0, D_MODEL // SUBLANES, slab, (zero,) * (n_rows * n_tiles), unroll=4)
    out_ref[...] = jnp.zeros(out_ref.shape, F32)
    for r in range(n_rows):
        for t in range(n_tiles):
            tot = accs[r * n_tiles + t].sum(axis=0, keepdims=True)
            out_ref[r:r + 1, t * LANES:(t + 1) * LANES] = tot + b_ref[:, t * LANES:(t + 1) * LANES]


def _adaln(cb, w_ada, b_ada):
    n_out = w_ada.shape[1]
    return pl.pallas_call(
        _adaln_kernel,
        grid=(n_out // ADA_TN,),
        in_specs=[
            pl.BlockSpec(cb.shape, lambda j: (0, 0, 0)),
            pl.BlockSpec((D_MODEL, ADA_TN), lambda j: (0, j)),
            pl.BlockSpec((1, ADA_TN), lambda j: (0, j)),
        ],
        out_specs=pl.BlockSpec((SUBLANES, ADA_TN), lambda j: (0, j)),
        out_shape=jax.ShapeDtypeStruct((SUBLANES, n_out), F32),
        scratch_shapes=[pltpu.VMEM(cb.shape, F32)],
        compiler_params=_cparams(("arbitrary",)),
        name="adaln",
    )(cb, w_ada, b_ada)


INP_TM = 1024
INP_TN = 1024
INP_SUB = 256
INP_GROUP = 2
Q_SCALE = HEAD_DIM ** -0.5 * math.log2(math.e)


def _head_norm_rope(acc, g, sel_ref, rope_ref, mult):
    xg = acc * g
    ssq = _dot((acc * acc).astype(BF16), sel_ref[0])
    r = lax.rsqrt(ssq * (1.0 / HEAD_DIM) + EPS) * mult
    if rope_ref is None:
        return xg * r
    swapped = _dot(xg.astype(BF16), sel_ref[1])
    heads = []
    for hh in range(INP_SUB // HEAD_DIM):
        sl = slice(hh * HEAD_DIM, (hh + 1) * HEAD_DIM)
        heads.append(xg[:, sl] * rope_ref[0] + swapped[:, sl] * rope_ref[1])
    return jnp.concatenate(heads, axis=1) * r


def _col_kind(col):
    if col < Q_END:
        return "q"
    if col < K_END:
        return "k"
    if col < F_END:
        return "plain"
    return "gate"


def _inproj_kernel(x_ref, mod_ref, n1g_ref, w_ref, rope_ref, qg_ref, kg_ref, sel_ref, out_ref, h_ref,
                   *, col_block0, n_col_blocks, use_rope):
    j = pl.program_id(1)
    s = pl.program_id(2)

    def prologue():
        _norm_modulate(x_ref, n1g_ref[...], mod_ref[0, 1:2, :], mod_ref[0, 0:1, :], h_ref.at[s])

    def epilogue(acc, kind, c0):
        if kind in ("q", "k"):
            g_ref, mult = (qg_ref, Q_SCALE) if kind == "q" else (kg_ref, 1.0)
            res = _head_norm_rope(acc, g_ref[...], sel_ref, rope_ref if use_rope else None, mult)
            out_ref[:, c0:c0 + INP_SUB] = res.astype(BF16)
        elif kind == "plain":
            out_ref[:, c0:c0 + INP_SUB] = acc.astype(BF16)
        else:
            out_ref[:, c0:c0 + INP_SUB] = (0.5 * jnp.tanh(0.5 * acc) + 0.5).astype(BF16)

    def tile(col_block):
        subs = range(0, INP_TN, INP_SUB)
        pending = None
        for c0 in subs:
            acc = _dot(h_ref[s], w_ref[:, c0:c0 + INP_SUB].astype(BF16))
            if pending is not None:
                epilogue(*pending)
            pending = (acc, _col_kind(col_block * INP_TN + c0), c0)
        epilogue(*pending)

    def kinds_of(cb):
        return tuple(_col_kind(cb * INP_TN + c0) for c0 in range(0, INP_TN, INP_SUB))

    @pl.when(j == 0)
    def _():
        prologue()
        tile(col_block0)

    lo = 1
    while lo < n_col_blocks:
        hi = lo
        while hi + 1 < n_col_blocks and kinds_of(col_block0 + hi + 1) == kinds_of(col_block0 + lo):
            hi += 1
        pl.when(jnp.logical_and(j >= lo, j <= hi))(functools.partial(tile, col_block0 + lo))
        lo = hi + 1


def _head_select_matrices():
    idx = np.arange(INP_SUB)
    ones_bd = (idx[:, None] // HEAD_DIM == idx[None, :] // HEAD_DIM).astype(np.float32)
    swap = np.zeros((INP_SUB, INP_SUB), np.float32)
    swap[idx[1::2], idx[0::2]] = -1.0
    swap[idx[0::2], idx[1::2]] = 1.0
    return jnp.asarray(np.stack([ones_bd, swap]), dtype=BF16)


def _inproj(x2, mod3, mod_row_of_tile, n1g, w_in, rope, qg, kg, *, tm, group, col_block0,
            n_col_blocks, use_rope, rope_tiles, name):
    rows = x2.shape[0]
    sel = _head_select_matrices()
    kern = functools.partial(_inproj_kernel, col_block0=col_block0, n_col_blocks=n_col_blocks,
                             use_rope=use_rope)
    assert rows % (tm * group) == 0

    def tile_of(g, s):
        return g * group + s

    def x_index(g, j, s):
        return (tile_of(g, jnp.where(j == 0, s, group - 1)), 0)

    return pl.pallas_call(
        kern,
        grid=(rows // (tm * group), n_col_blocks, group),
        in_specs=[
            pl.BlockSpec((tm, D_MODEL), x_index),
            pl.BlockSpec((1, N_MOD, D_MODEL), lambda g, j, s: (mod_row_of_tile(tile_of(g, s)), 0, 0)),
            pl.BlockSpec((1, D_MODEL), lambda g, j, s: (0, 0)),
            pl.BlockSpec((D_MODEL, INP_TN), lambda g, j, s: (0, j + col_block0)),
            pl.BlockSpec((2, tm, HEAD_DIM), lambda g, j, s: (0, tile_of(g, s) % rope_tiles, 0)),
            pl.BlockSpec((1, INP_SUB), lambda g, j, s: (0, 0)),
            pl.BlockSpec((1, INP_SUB), lambda g, j, s: (0, 0)),
            pl.BlockSpec(sel.shape, lambda g, j, s: (0, 0, 0)),
        ],
        out_specs=pl.BlockSpec((tm, INP_TN), lambda g, j, s: (tile_of(g, s), j)),
        out_shape=jax.ShapeDtypeStruct((rows, n_col_blocks * INP_TN), BF16),
        scratch_shapes=[pltpu.VMEM((group, tm, D_MODEL), BF16)],
        compiler_params=_cparams(("arbitrary", "arbitrary", "arbitrary")),
        name=name,
    )(x2, mod3, n1g, w_in, rope, qg, kg, sel)


ATT_SUB = 256
ATT_NSUB = 2
ATT_TQ = ATT_SUB * ATT_NSUB
ATT_CHUNK = 1024


def _key_chunks(n_keys):
    chunks = [(c0, ATT_CHUNK) for c0 in range(0, n_keys - ATT_CHUNK + 1, ATT_CHUNK)]
    done = len(chunks) * ATT_CHUNK
    if done < n_keys:
        chunks.append((done, n_keys - done))
    return chunks


def _attn_kernel(q_ref, qn_ref, k_ref, v_ref, kc_ref, vc_ref, *rest, n_cast):
    w_refs, o_ref, wb_refs = rest[:n_cast], rest[n_cast], rest[n_cast + 1:2 * n_cast + 1]
    kall_ref, vt_ref, s_ref, m_ref = rest[2 * n_cast + 1:]
    for w_ref, wb_ref in zip(w_refs, wb_refs):
        wb_ref[...] = w_ref[...].astype(BF16)

    qi = pl.program_id(2)
    n_lat = k_ref.shape[0]
    n_keys = kall_ref.shape[0]

    units = [(sub, g) for sub in range(ATT_NSUB) for g in range(Q_PER_KV)]
    assert len(units) % 2 == 0
    chunks = _key_chunks(n_keys)

    def q_of(ref, u):
        sub, g = units[u]
        return ref[sub * ATT_SUB:(sub + 1) * ATT_SUB, g * HEAD_DIM:(g + 1) * HEAD_DIM]

    def scores_chunk(q_u, slot, c0, cs, m_run):
        s_c = lax.dot_general(kall_ref[c0:c0 + cs, :], q_u, (((1,), (1,)), ((), ())),
                              preferred_element_type=F32)
        s_ref[slot, c0:c0 + cs, :] = s_c
        mc = jnp.max(s_c, axis=0, keepdims=True)
        return mc if m_run is None else jnp.maximum(m_run, mc)

    @pl.when(qi == 0)
    def _():
        kall_ref[0:n_lat, :] = k_ref[...]
        kall_ref[n_lat:, :] = kc_ref[...]
        vt_ref[:, 0:n_lat] = v_ref[...].astype(F32).T.astype(BF16)
        vt_ref[:, n_lat:] = vc_ref[...].astype(F32).T.astype(BF16)
        m0 = None
        for c0, cs in chunks:
            m0 = scores_chunk(q_of(q_ref, 0), 0, c0, cs, m0)
        m_ref[...] = m0

    m_cur = m_ref[...]
    for t in range(len(units)):
        last = t + 1 == len(units)
        q_nxt = q_of(qn_ref, 0) if last else q_of(q_ref, t + 1)
        m_nxt = l = o_acc = None
        for c0, cs in chunks:
            m_nxt = scores_chunk(q_nxt, (t + 1) % 2, c0, cs, m_nxt)
            p_c = jnp.exp2(s_ref[t % 2, c0:c0 + cs, :] - m_cur)
            lc = jnp.sum(p_c, axis=0, keepdims=True)
            l = lc if l is None else l + lc
            pv = _dot(vt_ref[:, c0:c0 + cs], p_c.astype(BF16))
            o_acc = pv if o_acc is None else o_acc + pv
        sub, g = units[t]
        o_ref[sub * ATT_SUB:(sub + 1) * ATT_SUB, g * HEAD_DIM:(g + 1) * HEAD_DIM] = (
            (o_acc * (1.0 / l)).T.astype(BF16))
        m_cur = m_nxt
    m_ref[...] = m_cur


def _attention(p, pc, batch, n_lat, n_ctx, weights):
    q_tiles = n_lat // ATT_TQ
    kcol = Q_END // HEAD_DIM
    vcol = K_END // HEAD_DIM
    n_keys = n_lat + n_ctx
    n_steps = batch * N_KV_HEADS * q_tiles
    slab_specs = []
    for w in weights:
        slab = w.shape[0] // n_steps
        assert slab * n_steps == w.shape[0] and slab % (2 * SUBLANES) == 0
        slab_specs.append(pl.BlockSpec(
            (slab, w.shape[1]), lambda b, h, qi: ((b * N_KV_HEADS + h) * q_tiles + qi, 0)))
    results = pl.pallas_call(
        functools.partial(_attn_kernel, n_cast=len(weights)),
        grid=(batch, N_KV_HEADS, q_tiles),
        in_specs=[
            pl.BlockSpec((ATT_TQ, Q_PER_KV * HEAD_DIM), lambda b, h, qi: (b * q_tiles + qi, h)),
            pl.BlockSpec((ATT_TQ, Q_PER_KV * HEAD_DIM),
                         lambda b, h, qi: (b * q_tiles + jnp.minimum(qi + 1, q_tiles - 1), h)),
            pl.BlockSpec((n_lat, HEAD_DIM), lambda b, h, qi: (b, kcol + h)),
            pl.BlockSpec((n_lat, HEAD_DIM), lambda b, h, qi: (b, vcol + h)),
            pl.BlockSpec((n_ctx, HEAD_DIM), lambda b, h, qi: (b, h)),
            pl.BlockSpec((n_ctx, HEAD_DIM), lambda b, h, qi: (b, N_KV_HEADS + h)),
        ] + slab_specs,
        out_specs=[pl.BlockSpec((ATT_TQ, Q_PER_KV * HEAD_DIM),
                                lambda b, h, qi: (b * q_tiles + qi, h))] + slab_specs,
        out_shape=[jax.ShapeDtypeStruct((batch * n_lat, ATTN_WIDTH), BF16)]
        + [jax.ShapeDtypeStruct(w.shape, BF16) for w in weights],
        scratch_shapes=[pltpu.VMEM((n_keys, HEAD_DIM), BF16), pltpu.VMEM((HEAD_DIM, n_keys), BF16),
                        pltpu.VMEM((2, n_keys, ATT_SUB), F32), pltpu.VMEM((1, ATT_SUB), F32)],
        compiler_params=_cparams(("arbitrary", "arbitrary", "arbitrary")),
        name="attn",
    )(p, p, p, p, pc, pc, *weights)
    return results[0], results[1:]


FOU_T = 256


def _fourier_kernel(u_ref, ct_ref, st_ref, cc_ref, sc_ref, jrev_ref, alt_ref, f_ref,
                    e_ref, o_ref, carry_ref, *, n_lat):
    s = pl.program_id(1)
    half = n_lat // 2
    tiles = half // FOU_T
    row = lax.broadcasted_iota(jnp.int32, (FOU_T, FOURIER_WIDTH), 0)

    def chan_dft(g_mat, h_mat):
        fd, fm = [], []
        for g in range(N_FOURIER_GROUPS):
            sl = slice(g * FOURIER_GROUP_DIM, (g + 1) * FOURIER_GROUP_DIM)
            gc = _dot(g_mat[:, sl].astype(BF16), cc_ref[...])
            if h_mat is None:
                fd.append(gc)
                continue
            hs = _dot(h_mat[:, sl].astype(BF16), sc_ref[...])
            fd.append(gc - hs)
            fm.append(gc + hs)
        return jnp.concatenate(fd, axis=1), (jnp.concatenate(fm, axis=1) if fm else None)

    @pl.when(s == 0)
    def _():
        for t in range(tiles):
            lo = u_ref[t * FOU_T:(t + 1) * FOU_T, :].astype(F32)
            hi_tile = u_ref[(2 * tiles - 1 - t) * FOU_T:(2 * tiles - t) * FOU_T, :]
            part = _dot(jrev_ref[...], hi_tile)
            if t == 0:
                first = jnp.zeros((1, FOURIER_WIDTH), F32)
            else:
                r0 = (2 * tiles - t) * FOU_T
                first = u_ref[r0:r0 + 1, :].astype(F32)
            part = jnp.where(row == 0, first, part)
            e_ref[t * FOU_T:(t + 1) * FOU_T, :] = (lo + part).astype(BF16)
            o_ref[t * FOU_T:(t + 1) * FOU_T, :] = (lo - part).astype(BF16)
        g_nyq = _dot(alt_ref[...], u_ref[...])
        f_nyq, _ = chan_dft(g_nyq, None)
        carry_ref[...] = f_nyq

    scale = n_lat ** -0.5
    u_half = u_ref[half:half + 1, :].astype(F32) * scale
    g_mat = _dot(ct_ref[...], e_ref[...]) + jnp.where(row % 2 == 0, u_half, -u_half)
    h_mat = _dot(st_ref[...], o_ref[...])
    fd, fm = chan_dft(g_mat, h_mat)
    f_ref[0, 0, 0] = fd.astype(BF16)
    fm_bf = fm.astype(BF16)
    mirrored = jnp.where(row == 0, carry_ref[0:1, :], _dot(jrev_ref[...], fm_bf))
    f_ref[0, 1, 0] = mirrored.astype(BF16)
    carry_ref[0:1, :] = fm_bf[0:1, :].astype(F32)


def _fourier(p, tables, batch, n_lat):
    ct, st, cc, sc, jrev, alt = tables
    ucol = V_END // FOURIER_WIDTH
    tiles = n_lat // 2 // FOU_T
    const2 = lambda b, s: (0, 0)
    return pl.pallas_call(
        functools.partial(_fourier_kernel, n_lat=n_lat),
        grid=(batch, tiles),
        in_specs=[
            pl.BlockSpec((n_lat, FOURIER_WIDTH), lambda b, s: (b, ucol)),
            pl.BlockSpec((FOU_T, n_lat // 2), lambda b, s: (tiles - 1 - s, 0)),
            pl.BlockSpec((FOU_T, n_lat // 2), lambda b, s: (tiles - 1 - s, 0)),
            pl.BlockSpec(cc.shape, const2),
            pl.BlockSpec(sc.shape, const2),
            pl.BlockSpec(jrev.shape, const2),
            pl.BlockSpec(alt.shape, const2),
        ],
        out_specs=pl.BlockSpec((1, 2, 1, FOU_T, FOURIER_WIDTH), lambda b, s: (b, 0, tiles - 1 - s, 0, 0)),
        out_shape=jax.ShapeDtypeStruct((batch, 2, tiles, FOU_T, FOURIER_WIDTH), BF16),
        scratch_shapes=[pltpu.VMEM((n_lat // 2, FOURIER_WIDTH), BF16),
                        pltpu.VMEM((n_lat // 2, FOURIER_WIDTH), BF16),
                        pltpu.VMEM((SUBLANES, FOURIER_WIDTH), F32)],
        compiler_params=_cparams(("arbitrary", "arbitrary")),
        name="fourier",
    )(p, ct, st, cc, sc, jrev, alt)


MRG_TM = FOU_T


def _merge_kernel(o_ref, f_ref, gates_ref, x_ref, mod_ref, wa_ref, wf_ref, wo_ref, x1_ref):
    a = _dot(o_ref[...], wa_ref[...])
    fm = _dot(f_ref[0, 0, 0], wf_ref[...])
    z = gates_ref[:, 0:D_MODEL].astype(F32) * a + gates_ref[:, D_MODEL:].astype(F32) * fm
    y = _dot(z.astype(BF16), wo_ref[...])
    x1_ref[...] = x_ref[...] + mod_ref[0, 2:3, :] * y


def _merge(o, f, p, x2, mod3, wa, wf, wo, n_lat):
    rows = x2.shape[0]
    tiles_per_batch = n_lat // MRG_TM
    const = lambda i: (0, 0)
    single = pl.Buffered(1)

    def f_index(i):
        t = i % tiles_per_batch
        upper = t // (tiles_per_batch // 2)
        return (i // tiles_per_batch, upper, jnp.where(upper == 1, tiles_per_batch - 1 - t, t), 0, 0)

    return pl.pallas_call(
        _merge_kernel,
        grid=(rows // MRG_TM,),
        in_specs=[
            pl.BlockSpec((MRG_TM, ATTN_WIDTH), lambda i: (i, 0)),
            pl.BlockSpec((1, 1, 1, MRG_TM, FOURIER_WIDTH), f_index),
            pl.BlockSpec((MRG_TM, 2 * D_MODEL), lambda i: (i, F_END // (2 * D_MODEL))),
            pl.BlockSpec((MRG_TM, D_MODEL), lambda i: (i, 0)),
            pl.BlockSpec((1, N_MOD, D_MODEL), lambda i: (i // tiles_per_batch, 0, 0)),
            pl.BlockSpec(wa.shape, const, pipeline_mode=single),
            pl.BlockSpec(wf.shape, const, pipeline_mode=single),
            pl.BlockSpec(wo.shape, const, pipeline_mode=single),
        ],
        out_specs=pl.BlockSpec((MRG_TM, D_MODEL), lambda i: (i, 0)),
        out_shape=jax.ShapeDtypeStruct((rows, D_MODEL), F32),
        compiler_params=_cparams(("arbitrary",)),
        name="merge",
    )(o, f, p, x2, mod3, wa, wf, wo)


MLP_TM = 512
MLP_TF = 1024


def _mlp_kernel(x1_ref, mod_ref, n2g_ref, w1_ref, w2_ref, fg_ref, out_ref, h_ref, acc_ref):
    kf = pl.program_id(1)

    @pl.when(kf == 0)
    def _():
        _norm_modulate(x1_ref, n2g_ref[...], mod_ref[0, 4:5, :], mod_ref[0, 3:4, :], h_ref)

    def chunk():
        hid = jnp.maximum(_dot(h_ref[...], w1_ref[...]), 0.0)
        return _dot((hid * hid).astype(BF16), w2_ref[...])

    last = pl.num_programs(1) - 1

    @pl.when(kf == 0)
    def _():
        acc_ref[...] = chunk()

    @pl.when(jnp.logical_and(kf > 0, kf < last))
    def _():
        acc_ref[...] += chunk()

    @pl.when(kf == last)
    def _():
        x2 = x1_ref[...] + mod_ref[0, 5:6, :] * (acc_ref[...] + chunk())
        out_ref[...] = _rmsnorm_rows(x2, fg_ref[...])


def _mlp(x1, mod3, n2g, w1, w2, fg, n_lat):
    rows = x1.shape[0]
    tiles_per_batch = n_lat // MLP_TM
    return pl.pallas_call(
        _mlp_kernel,
        grid=(rows // MLP_TM, D_FF // MLP_TF),
        in_specs=[
            pl.BlockSpec((MLP_TM, D_MODEL), lambda i, kf: (i, 0)),
            pl.BlockSpec((1, N_MOD, D_MODEL), lambda i, kf: (i // tiles_per_batch, 0, 0)),
            pl.BlockSpec((1, D_MODEL), lambda i, kf: (0, 0)),
            pl.BlockSpec((D_MODEL, MLP_TF), lambda i, kf: (0, kf)),
            pl.BlockSpec((MLP_TF, D_MODEL), lambda i, kf: (kf, 0)),
            pl.BlockSpec((1, D_MODEL), lambda i, kf: (0, 0)),
        ],
        out_specs=pl.BlockSpec((MLP_TM, D_MODEL), lambda i, kf: (i, 0)),
        out_shape=jax.ShapeDtypeStruct((rows, D_MODEL), F32),
        scratch_shapes=[pltpu.VMEM((MLP_TM, D_MODEL), BF16), pltpu.VMEM((MLP_TM, D_MODEL), F32)],
        compiler_params=_cparams(("arbitrary", "arbitrary")),
        name="mlp",
    )(x1, mod3, n2g, w1, w2, fg)


def _rope_tables(rows):
    pair = np.arange(HEAD_DIM) // 2
    inv = (np.float32(1.0) / np.float32(ROPE_THETA) ** (
        (pair % ROPE_AXIS_PAIRS).astype(np.float32) / np.float32(ROPE_AXIS_PAIRS))).astype(np.float32)
    pos = np.arange(rows * GRID_W)[:, None]
    coord = np.where(pair[None, :] < ROPE_AXIS_PAIRS, pos // GRID_W, pos % GRID_W).astype(np.float32)
    ang = coord * inv[None, :]
    return jnp.asarray(np.stack([np.cos(ang), np.sin(ang)]).astype(np.float32))


def _dft_tables(size, period, scale):
    n_lo = 1 << ((size.bit_length() - 1) // 2)
    n_hi = size // n_lo
    assert n_lo * n_hi == size
    idx = jnp.arange(size, dtype=jnp.int32)

    def cos_sin(mult):
        ang = ((mult[:, None] * idx[None, :]) % period).astype(F32) * (2.0 * math.pi / period)
        return jnp.cos(ang), jnp.sin(ang)

    c_hi, s_hi = cos_sin(jnp.arange(n_hi, dtype=jnp.int32) * n_lo)
    c_lo, s_lo = cos_sin(jnp.arange(n_lo, dtype=jnp.int32))
    c_hi, s_hi = c_hi[:, None, :] * scale, s_hi[:, None, :] * scale
    c_lo, s_lo = c_lo[None, :, :], s_lo[None, :, :]
    cos = (c_hi * c_lo - s_hi * s_lo).reshape(size, size)
    sin = (s_hi * c_lo + c_hi * s_lo).reshape(size, size)
    return cos.astype(BF16), sin.astype(BF16)


def _fourier_tables(n_lat):
    half = n_lat // 2
    ct, st = _dft_tables(half, n_lat, n_lat ** -0.5)
    cc, sc = _dft_tables(FOURIER_GROUP_DIM, FOURIER_GROUP_DIM, FOURIER_GROUP_DIM ** -0.5)
    jrev = np.zeros((FOU_T, FOU_T), np.float32)
    jrev[np.arange(1, FOU_T), FOU_T - np.arange(1, FOU_T)] = 1.0
    alt = np.zeros((SUBLANES, n_lat), np.float32)
    alt[0] = (1.0 - 2.0 * (np.arange(n_lat) % 2)) * n_lat ** -0.5
    return ct, st, cc, sc, jnp.asarray(jrev, dtype=BF16), jnp.asarray(alt, dtype=BF16)


def kernel(x, c, ctx, c_ctx, w_ada, b_ada, norm1_g, w_in, q_norm_g, k_norm_g, w_attn_o, w_fourier,
           w_out, norm2_g, w1, w2, final_norm_g):
    batch, n_lat, _ = x.shape
    n_ctx = ctx.shape[1]
    assert w_ada.shape[0] == 1, "single-layer block"
    assert n_lat % INP_TM == 0 and n_ctx % SUBLANES == 0

    cc = jnp.concatenate([c, c_ctx[None, :]], axis=0)
    cb = jnp.broadcast_to(cc[:, :, None], (batch + 1, D_MODEL, LANES))
    mod = _adaln(cb, w_ada[0], b_ada)
    mod3 = mod.reshape(SUBLANES, N_MOD, D_MODEL)

    w_in_bf = w_in[0]
    n1g = norm1_g
    heads_per_sub = INP_SUB // HEAD_DIM
    qg, kg = jnp.tile(q_norm_g, (1, heads_per_sub)), jnp.tile(k_norm_g, (1, heads_per_sub))
    rope = _rope_tables(n_lat // GRID_W)

    x2 = x.reshape(batch * n_lat, D_MODEL)
    tiles_per_batch = n_lat // INP_TM
    p = _inproj(x2, mod3, lambda i: i // tiles_per_batch, n1g, w_in_bf, rope, qg, kg,
                tm=INP_TM, group=INP_GROUP, col_block0=0, n_col_blocks=IN_WIDTH // INP_TN, use_rope=True,
                rope_tiles=tiles_per_batch, name="inproj")
    ctx2 = ctx.reshape(batch * n_ctx, D_MODEL)
    pc = _inproj(ctx2, mod3, lambda i: batch, n1g, w_in_bf, rope, qg, kg,
                 tm=batch * n_ctx, group=1, col_block0=Q_END // INP_TN,
                 n_col_blocks=(V_END - Q_END) // INP_TN,
                 use_rope=False, rope_tiles=1, name="ctxproj")

    o, (wa_bf, wf_bf, wo_bf, w1_bf, w2_bf) = _attention(
        p, pc, batch, n_lat, n_ctx, (w_attn_o[0], w_fourier[0], w_out[0], w1[0], w2[0]))

    f = _fourier(p, _fourier_tables(n_lat), batch, n_lat)

    x1 = _merge(o, f, p, x2, mod3, wa_bf, wf_bf, wo_bf, n_lat)
    out = _mlp(x1, mod3, norm2_g, w1_bf, w2_bf, final_norm_g[None, :], n_lat)
    return out.reshape(batch, n_lat, D_MODEL)
```

```python
import functools
import math

import jax
import jax.numpy as jnp
import numpy as np
from jax import lax
from jax.experimental import pallas as pl
from jax.experimental.pallas import tpu as pltpu

D_MODEL = 2048
GRID_W = 64
HEAD_DIM = 128
N_Q_HEADS = 16
N_KV_HEADS = 4
Q_PER_KV = N_Q_HEADS // N_KV_HEADS
ATTN_WIDTH = N_Q_HEADS * HEAD_DIM
KV_WIDTH = N_KV_HEADS * HEAD_DIM
N_FOURIER_GROUPS = 4
FOURIER_GROUP_DIM = 256
FOURIER_WIDTH = N_FOURIER_GROUPS * FOURIER_GROUP_DIM
Q_END = ATTN_WIDTH
K_END = Q_END + KV_WIDTH
V_END = K_END + KV_WIDTH
F_END = V_END + FOURIER_WIDTH
IN_WIDTH = F_END + 2 * D_MODEL
D_FF = 4 * D_MODEL
N_MOD = 6
ROPE_THETA = 10000.0
ROPE_AXIS_PAIRS = HEAD_DIM // 4
EPS = 1e-6

LANES = 128
SUBLANES = 8
VMEM_LIMIT = 56 * 1024 * 1024

BF16 = jnp.bfloat16
F32 = jnp.float32


def _cparams(sem):
    return pltpu.CompilerParams(dimension_semantics=sem, vmem_limit_bytes=VMEM_LIMIT)


def _dot(a, b):
    return jnp.dot(a, b, preferred_element_type=F32)


def _rmsnorm_rows(x, g):
    ms = jnp.mean(x * x, axis=-1, keepdims=True)
    return x * lax.rsqrt(ms + EPS) * g


NORM_ROWS = 128


def _norm_modulate(x_ref, g, scale, shift, h_ref):
    gs = g * (1.0 + scale)
    for r0 in range(0, x_ref.shape[0], NORM_ROWS):
        xa = x_ref[r0:r0 + NORM_ROWS, :]
        r = lax.rsqrt(jnp.mean(xa * xa, axis=-1, keepdims=True) + EPS)
        xb = x_ref[r0:r0 + NORM_ROWS, :]
        h_ref[r0:r0 + NORM_ROWS, :] = (xb * r * gs + shift).astype(BF16)


ADA_TN = 1024


def _adaln_kernel(cb_ref, w_ref, b_ref, out_ref, act_ref):
    n_rows = cb_ref.shape[0]

    @pl.when(pl.program_id(0) == 0)
    def _():
        for r in range(n_rows):
            cr = cb_ref[r]
            act_ref[r] = cr * jax.nn.sigmoid(cr)

    n_tiles = ADA_TN // LANES

    def slab(kc, accs):
        k0 = pl.multiple_of(kc * SUBLANES, SUBLANES)
        w8 = w_ref[pl.ds(k0, SUBLANES), :]
        a8 = [act_ref[r, pl.ds(k0, SUBLANES), :] for r in range(n_rows)]
        return tuple(accs[r * n_tiles + t] + w8[:, t * LANES:(t + 1) * LANES] * a8[r]
                     for r in range(n_rows) for t in range(n_tiles))

    zero = jnp.zeros((SUBLANES, LANES), F32)
    accs = lax.fori_loop(0, D_MODEL // SUBLANES, slab, (zero,) * (n_rows * n_tiles), unroll=4)
    out_ref[...] = jnp.zeros(out_ref.shape, F32)
    for r in range(n_rows):
        for t in range(n_tiles):
            tot = accs[r * n_tiles + t].sum(axis=0, keepdims=True)
            out_ref[r:r + 1, t * LANES:(t + 1) * LANES] = tot + b_ref[:, t * LANES:(t + 1) * LANES]


def _adaln(cb, w_ada, b_ada):
    n_out = w_ada.shape[1]
    return pl.pallas_call(
        _adaln_kernel,
        grid=(n_out // ADA_TN,),
        in_specs=[
            pl.BlockSpec(cb.shape, lambda j: (0, 0, 0)),
            pl.BlockSpec((D_MODEL, ADA_TN), lambda j: (0, j)),
            pl.BlockSpec((1, ADA_TN), lambda j: (0, j)),
        ],
        out_specs=pl.BlockSpec((SUBLANES, ADA_TN), lambda j: (0, j)),
        out_shape=jax.ShapeDtypeStruct((SUBLANES, n_out), F32),
        scratch_shapes=[pltpu.VMEM(cb.shape, F32)],
        compiler_params=_cparams(("arbitrary",)),
        name="adaln",
    )(cb, w_ada, b_ada)


INP_TM = 1024
INP_TN = 1024
INP_SUB = 256
INP_GROUP = 2
Q_SCALE = HEAD_DIM ** -0.5 * math.log2(math.e)


def _head_norm_rope(acc, g, sel_ref, rope_ref, mult):
    xg = acc * g
    ssq = _dot((acc * acc).astype(BF16), sel_ref[0])
    r = lax.rsqrt(ssq * (1.0 / HEAD_DIM) + EPS) * mult
    if rope_ref is None:
        return xg * r
    swapped = _dot(xg.astype(BF16), sel_ref[1])
    heads = []
    for hh in range(INP_SUB // HEAD_DIM):
        sl = slice(hh * HEAD_DIM, (hh + 1) * HEAD_DIM)
        heads.append(xg[:, sl] * rope_ref[0] + swapped[:, sl] * rope_ref[1])
    return jnp.concatenate(heads, axis=1) * r


def _col_kind(col):
    if col < Q_END:
        return "q"
    if col < K_END:
        return "k"
    if col < F_END:
        return "plain"
    return "gate"


def _inproj_kernel(x_ref, mod_ref, n1g_ref, w_ref, rope_ref, qg_ref, kg_ref, sel_ref, out_ref, h_ref,
                   *, col_block0, n_col_blocks, use_rope):
    j = pl.program_id(1)
    s = pl.program_id(2)

    def prologue():
        _norm_modulate(x_ref, n1g_ref[...], mod_ref[0, 1:2, :], mod_ref[0, 0:1, :], h_ref.at[s])

    def epilogue(acc, kind, c0):
        if kind in ("q", "k"):
            g_ref, mult = (qg_ref, Q_SCALE) if kind == "q" else (kg_ref, 1.0)
            res = _head_norm_rope(acc, g_ref[...], sel_ref, rope_ref if use_rope else None, mult)
            out_ref[:, c0:c0 + INP_SUB] = res.astype(BF16)
        elif kind == "plain":
            out_ref[:, c0:c0 + INP_SUB] = acc.astype(BF16)
        else:
            out_ref[:, c0:c0 + INP_SUB] = (0.5 * jnp.tanh(0.5 * acc) + 0.5).astype(BF16)

    def tile(col_block):
        subs = range(0, INP_TN, INP_SUB)
        pending = None
        for c0 in subs:
            acc = _dot(h_ref[s], w_ref[:, c0:c0 + INP_SUB].astype(BF16))
            if pending is not None:
                epilogue(*pending)
            pending = (acc, _col_kind(col_block * INP_TN + c0), c0)
        epilogue(*pending)

    def kinds_of(cb):
        return tuple(_col_kind(cb * INP_TN + c0) for c0 in range(0, INP_TN, INP_SUB))

    @pl.when(j == 0)
    def _():
        prologue()
        tile(col_block0)

    lo = 1
    while lo < n_col_blocks:
        hi = lo
        while hi + 1 < n_col_blocks and kinds_of(col_block0 + hi + 1) == kinds_of(col_block0 + lo):
            hi += 1
        pl.when(jnp.logical_and(j >= lo, j <= hi))(functools.partial(tile, col_block0 + lo))
        lo = hi + 1


def _head_select_matrices():
    idx = np.arange(INP_SUB)
    ones_bd = (idx[:, None] // HEAD_DIM == idx[None, :] // HEAD_DIM).astype(np.float32)
    swap = np.zeros((INP_SUB, INP_SUB), np.float32)
    swap[idx[1::2], idx[0::2]] = -1.0
    swap[idx[0::2], idx[1::2]] = 1.0
    return jnp.asarray(np.stack([ones_bd, swap]), dtype=BF16)


def _inproj(x2, mod3, mod_row_of_tile, n1g, w_in, rope, qg, kg, *, tm, group, col_block0,
            n_col_blocks, use_rope, rope_tiles, name):
    rows = x2.shape[0]
    sel = _head_select_matrices()
    kern = functools.partial(_inproj_kernel, col_block0=col_block0, n_col_blocks=n_col_blocks,
                             use_rope=use_rope)
    assert rows % (tm * group) == 0

    def tile_of(g, s):
        return g * group + s

    def x_index(g, j, s):
        return (tile_of(g, jnp.where(j == 0, s, group - 1)), 0)

    return pl.pallas_call(
        kern,
        grid=(rows // (tm * group), n_col_blocks, group),
        in_specs=[
            pl.BlockSpec((tm, D_MODEL), x_index),
            pl.BlockSpec((1, N_MOD, D_MODEL), lambda g, j, s: (mod_row_of_tile(tile_of(g, s)), 0, 0)),
            pl.BlockSpec((1, D_MODEL), lambda g, j, s: (0, 0)),
            pl.BlockSpec((D_MODEL, INP_TN), lambda g, j, s: (0, j + col_block0)),
            pl.BlockSpec((2, tm, HEAD_DIM), lambda g, j, s: (0, tile_of(g, s) % rope_tiles, 0)),
            pl.BlockSpec((1, INP_SUB), lambda g, j, s: (0, 0)),
            pl.BlockSpec((1, INP_SUB), lambda g, j, s: (0, 0)),
            pl.BlockSpec(sel.shape, lambda g, j, s: (0, 0, 0)),
        ],
        out_specs=pl.BlockSpec((tm, INP_TN), lambda g, j, s: (tile_of(g, s), j)),
        out_shape=jax.ShapeDtypeStruct((rows, n_col_blocks * INP_TN), BF16),
        scratch_shapes=[pltpu.VMEM((group, tm, D_MODEL), BF16)],
        compiler_params=_cparams(("arbitrary", "arbitrary", "arbitrary")),
        name=name,
    )(x2, mod3, n1g, w_in, rope, qg, kg, sel)


ATT_SUB = 256
ATT_NSUB = 2
ATT_TQ = ATT_SUB * ATT_NSUB
ATT_CHUNK = 1024


def _key_chunks(n_keys):
    chunks = [(c0, ATT_CHUNK) for c0 in range(0, n_keys - ATT_CHUNK + 1, ATT_CHUNK)]
    done = len(chunks) * ATT_CHUNK
    if done < n_keys:
        chunks.append((done, n_keys - done))
    return chunks


def _attn_kernel(q_ref, qn_ref, k_ref, v_ref, kc_ref, vc_ref, *rest, n_cast):
    w_refs, o_ref, wb_refs = rest[:n_cast], rest[n_cast], rest[n_cast + 1:2 * n_cast + 1]
    kall_ref, vt_ref, s_ref, m_ref = rest[2 * n_cast + 1:]
    qi = pl.program_id(2)
    n_lat = k_ref.shape[0]
    n_keys = kall_ref.shape[0]

    units = [(sub, g) for sub in range(ATT_NSUB) for g in range(Q_PER_KV)]
    assert len(units) % 2 == 0
    chunks = _key_chunks(n_keys)

    def q_of(ref, u):
        sub, g = units[u]
        return ref[sub * ATT_SUB:(sub + 1) * ATT_SUB, g * HEAD_DIM:(g + 1) * HEAD_DIM]

    def scores_chunk(q_u, slot, c0, cs, m_run):
        s_c = lax.dot_general(kall_ref[c0:c0 + cs, :], q_u, (((1,), (1,)), ((), ())),
                              preferred_element_type=F32)
        s_ref[slot, c0:c0 + cs, :] = s_c
        mc = jnp.max(s_c, axis=0, keepdims=True)
        return mc if m_run is None else jnp.maximum(m_run, mc)

    @pl.when(qi == 0)
    def _():
        kall_ref[0:n_lat, :] = k_ref[...]
        kall_ref[n_lat:, :] = kc_ref[...]
        vt_ref[:, 0:n_lat] = v_ref[...].astype(F32).T.astype(BF16)
        vt_ref[:, n_lat:] = vc_ref[...].astype(F32).T.astype(BF16)
        m0 = None
        for c0, cs in chunks:
            m0 = scores_chunk(q_of(q_ref, 0), 0, c0, cs, m0)
        m_ref[...] = m0

    for w_ref, wb_ref in zip(w_refs, wb_refs):
        if len(wb_ref.shape) == 2:
            wb_ref[...] = w_ref[...].astype(BF16)
        else:
            width = wb_ref.shape[2]
            for c in range(wb_ref.shape[0]):
                wb_ref[c] = w_ref[:, c * width:(c + 1) * width].astype(BF16)

    m_cur = m_ref[...]
    for t in range(len(units)):
        last = t + 1 == len(units)
        q_nxt = q_of(qn_ref, 0) if last else q_of(q_ref, t + 1)
        m_nxt = l = o_acc = None
        for c0, cs in chunks:
            m_nxt = scores_chunk(q_nxt, (t + 1) % 2, c0, cs, m_nxt)
            p_c = jnp.exp2(s_ref[t % 2, c0:c0 + cs, :] - m_cur)
            lc = jnp.sum(p_c, axis=0, keepdims=True)
            l = lc if l is None else l + lc
            pv = _dot(vt_ref[:, c0:c0 + cs], p_c.astype(BF16))
            o_acc = pv if o_acc is None else o_acc + pv
        sub, g = units[t]
        o_ref[sub * ATT_SUB:(sub + 1) * ATT_SUB, g * HEAD_DIM:(g + 1) * HEAD_DIM] = (
            (o_acc * (1.0 / l)).T.astype(BF16))
        m_cur = m_nxt
    m_ref[...] = m_cur


def _attention(p, pc, batch, n_lat, n_ctx, weights):
    q_tiles = n_lat // ATT_TQ
    kcol = Q_END // HEAD_DIM
    vcol = K_END // HEAD_DIM
    n_keys = n_lat + n_ctx
    n_steps = batch * N_KV_HEADS * q_tiles
    in_slabs, out_slabs, out_shapes = [], [], []
    for w, chunks in weights:
        rows, cols = w.shape
        slab = rows // n_steps
        assert slab * n_steps == rows and slab % (2 * SUBLANES) == 0 and cols % chunks == 0
        step_of = lambda b, h, qi: (b * N_KV_HEADS + h) * q_tiles + qi
        in_slabs.append(pl.BlockSpec((slab, cols), lambda b, h, qi: (step_of(b, h, qi), 0)))
        if chunks == 1:
            out_slabs.append(in_slabs[-1])
            out_shapes.append(jax.ShapeDtypeStruct((rows, cols), BF16))
        else:
            out_slabs.append(pl.BlockSpec((chunks, slab, cols // chunks),
                                          lambda b, h, qi: (0, step_of(b, h, qi), 0)))
            out_shapes.append(jax.ShapeDtypeStruct((chunks, rows, cols // chunks), BF16))
    results = pl.pallas_call(
        functools.partial(_attn_kernel, n_cast=len(weights)),
        grid=(batch, N_KV_HEADS, q_tiles),
        in_specs=[
            pl.BlockSpec((ATT_TQ, Q_PER_KV * HEAD_DIM), lambda b, h, qi: (b * q_tiles + qi, h)),
            pl.BlockSpec((ATT_TQ, Q_PER_KV * HEAD_DIM),
                         lambda b, h, qi: (b * q_tiles + jnp.minimum(qi + 1, q_tiles - 1), h)),
            pl.BlockSpec((n_lat, HEAD_DIM), lambda b, h, qi: (b, kcol + h)),
            pl.BlockSpec((n_lat, HEAD_DIM), lambda b, h, qi: (b, vcol + h)),
            pl.BlockSpec((n_ctx, HEAD_DIM), lambda b, h, qi: (b, h)),
            pl.BlockSpec((n_ctx, HEAD_DIM), lambda b, h, qi: (b, N_KV_HEADS + h)),
        ] + in_slabs,
        out_specs=[pl.BlockSpec((ATT_TQ, Q_PER_KV * HEAD_DIM),
                                lambda b, h, qi: (b * q_tiles + qi, h))] + out_slabs,
        out_shape=[jax.ShapeDtypeStruct((batch * n_lat, ATTN_WIDTH), BF16)] + out_shapes,
        scratch_shapes=[pltpu.VMEM((n_keys, HEAD_DIM), BF16), pltpu.VMEM((HEAD_DIM, n_keys), BF16),
                        pltpu.VMEM((2, n_keys, ATT_SUB), F32), pltpu.VMEM((1, ATT_SUB), F32)],
        compiler_params=_cparams(("arbitrary", "arbitrary", "arbitrary")),
        name="attn",
    )(p, p, p, p, pc, pc, *[w for w, _ in weights])
    return results[0], results[1:]


FOU_T = 256


def _fourier_kernel(u_ref, ct_ref, st_ref, cc_ref, sc_ref, jrev_ref, alt_ref, f_ref,
                    e_ref, o_ref, carry_ref, *, n_lat):
    s = pl.program_id(1)
    half = n_lat // 2
    tiles = half // FOU_T
    row = lax.broadcasted_iota(jnp.int32, (FOU_T, FOURIER_WIDTH), 0)

    def chan_dft(g_mat, h_mat):
        fd, fm = [], []
        for g in range(N_FOURIER_GROUPS):
            sl = slice(g * FOURIER_GROUP_DIM, (g + 1) * FOURIER_GROUP_DIM)
            gc = _dot(g_mat[:, sl].astype(BF16), cc_ref[...])
            if h_mat is None:
                fd.append(gc)
                continue
            hs = _dot(h_mat[:, sl].astype(BF16), sc_ref[...])
            fd.append(gc - hs)
            fm.append(gc + hs)
        return jnp.concatenate(fd, axis=1), (jnp.concatenate(fm, axis=1) if fm else None)

    @pl.when(s == 0)
    def _():
        for t in range(tiles):
            lo = u_ref[t * FOU_T:(t + 1) * FOU_T, :].astype(F32)
            hi_tile = u_ref[(2 * tiles - 1 - t) * FOU_T:(2 * tiles - t) * FOU_T, :]
            part = _dot(jrev_ref[...], hi_tile)
            if t == 0:
                first = jnp.zeros((1, FOURIER_WIDTH), F32)
            else:
                r0 = (2 * tiles - t) * FOU_T
                first = u_ref[r0:r0 + 1, :].astype(F32)
            part = jnp.where(row == 0, first, part)
            e_ref[t * FOU_T:(t + 1) * FOU_T, :] = (lo + part).astype(BF16)
            o_ref[t * FOU_T:(t + 1) * FOU_T, :] = (lo - part).astype(BF16)
        g_nyq = _dot(alt_ref[...], u_ref[...])
        f_nyq, _ = chan_dft(g_nyq, None)
        carry_ref[...] = f_nyq

    scale = n_lat ** -0.5
    u_half = u_ref[half:half + 1, :].astype(F32) * scale
    g_mat = _dot(ct_ref[...], e_ref[...]) + jnp.where(row % 2 == 0, u_half, -u_half)
    h_mat = _dot(st_ref[...], o_ref[...])
    fd, fm = chan_dft(g_mat, h_mat)
    f_ref[0, 0, 0] = fd.astype(BF16)
    fm_bf = fm.astype(BF16)
    mirrored = jnp.where(row == 0, carry_ref[0:1, :], _dot(jrev_ref[...], fm_bf))
    f_ref[0, 1, 0] = mirrored.astype(BF16)
    carry_ref[0:1, :] = fm_bf[0:1, :].astype(F32)


def _fourier(p, tables, batch, n_lat):
    ct, st, cc, sc, jrev, alt = tables
    ucol = V_END // FOURIER_WIDTH
    tiles = n_lat // 2 // FOU_T
    const2 = lambda b, s: (0, 0)
    return pl.pallas_call(
        functools.partial(_fourier_kernel, n_lat=n_lat),
        grid=(batch, tiles),
        in_specs=[
            pl.BlockSpec((n_lat, FOURIER_WIDTH), lambda b, s: (b, ucol)),
            pl.BlockSpec((FOU_T, n_lat // 2), lambda b, s: (tiles - 1 - s, 0)),
            pl.BlockSpec((FOU_T, n_lat // 2), lambda b, s: (tiles - 1 - s, 0)),
            pl.BlockSpec(cc.shape, const2),
            pl.BlockSpec(sc.shape, const2),
            pl.BlockSpec(jrev.shape, const2),
            pl.BlockSpec(alt.shape, const2),
        ],
        out_specs=pl.BlockSpec((1, 2, 1, FOU_T, FOURIER_WIDTH), lambda b, s: (b, 0, tiles - 1 - s, 0, 0)),
        out_shape=jax.ShapeDtypeStruct((batch, 2, tiles, FOU_T, FOURIER_WIDTH), BF16),
        scratch_shapes=[pltpu.VMEM((n_lat // 2, FOURIER_WIDTH), BF16),
                        pltpu.VMEM((n_lat // 2, FOURIER_WIDTH), BF16),
                        pltpu.VMEM((SUBLANES, FOURIER_WIDTH), F32)],
        compiler_params=_cparams(("arbitrary", "arbitrary")),
        name="fourier",
    )(p, ct, st, cc, sc, jrev, alt)


MRG_TM = FOU_T


def _merge_kernel(o_ref, f_ref, gates_ref, x_ref, mod_ref, wa_ref, wf_ref, wo_ref, x1_ref):
    a = _dot(o_ref[...], wa_ref[...])
    fm = _dot(f_ref[0, 0, 0], wf_ref[...])
    z = gates_ref[:, 0:D_MODEL].astype(F32) * a + gates_ref[:, D_MODEL:].astype(F32) * fm
    y = _dot(z.astype(BF16), wo_ref[...])
    x1_ref[...] = x_ref[...] + mod_ref[0, 2:3, :] * y


def _merge(o, f, p, x2, mod3, wa, wf, wo, n_lat):
    rows = x2.shape[0]
    tiles_per_batch = n_lat // MRG_TM
    const = lambda i: (0, 0)
    single = pl.Buffered(1)

    def f_index(i):
        t = i % tiles_per_batch
        upper = t // (tiles_per_batch // 2)
        return (i // tiles_per_batch, upper, jnp.where(upper == 1, tiles_per_batch - 1 - t, t), 0, 0)

    return pl.pallas_call(
        _merge_kernel,
        grid=(rows // MRG_TM,),
        in_specs=[
            pl.BlockSpec((MRG_TM, ATTN_WIDTH), lambda i: (i, 0)),
            pl.BlockSpec((1, 1, 1, MRG_TM, FOURIER_WIDTH), f_index),
            pl.BlockSpec((MRG_TM, 2 * D_MODEL), lambda i: (i, F_END // (2 * D_MODEL))),
            pl.BlockSpec((MRG_TM, D_MODEL), lambda i: (i, 0)),
            pl.BlockSpec((1, N_MOD, D_MODEL), lambda i: (i // tiles_per_batch, 0, 0)),
            pl.BlockSpec(wa.shape, const, pipeline_mode=single),
            pl.BlockSpec(wf.shape, const, pipeline_mode=single),
            pl.BlockSpec(wo.shape, const, pipeline_mode=single),
        ],
        out_specs=pl.BlockSpec((MRG_TM, D_MODEL), lambda i: (i, 0)),
        out_shape=jax.ShapeDtypeStruct((rows, D_MODEL), F32),
        compiler_params=_cparams(("arbitrary",)),
        name="merge",
    )(o, f, p, x2, mod3, wa, wf, wo)


MLP_TM = 512
MLP_TF = 1024


def _mlp_kernel(x1_ref, mod_ref, n2g_ref, w1_ref, w2_ref, fg_ref, out_ref, h_ref, acc_ref):
    kf = pl.program_id(1)

    @pl.when(kf == 0)
    def _():
        _norm_modulate(x1_ref, n2g_ref[...], mod_ref[0, 4:5, :], mod_ref[0, 3:4, :], h_ref)

    def chunk():
        hid = jnp.maximum(_dot(h_ref[...], w1_ref[0]), 0.0)
        return _dot((hid * hid).astype(BF16), w2_ref[...])

    last = pl.num_programs(1) - 1

    @pl.when(kf == 0)
    def _():
        acc_ref[...] = chunk()

    @pl.when(jnp.logical_and(kf > 0, kf < last))
    def _():
        acc_ref[...] += chunk()

    @pl.when(kf == last)
    def _():
        x2 = x1_ref[...] + mod_ref[0, 5:6, :] * (acc_ref[...] + chunk())
        out_ref[...] = _rmsnorm_rows(x2, fg_ref[...])


def _mlp(x1, mod3, n2g, w1, w2, fg, n_lat):
    rows = x1.shape[0]
    tiles_per_batch = n_lat // MLP_TM
    return pl.pallas_call(
        _mlp_kernel,
        grid=(rows // MLP_TM, D_FF // MLP_TF),
        in_specs=[
            pl.BlockSpec((MLP_TM, D_MODEL), lambda i, kf: (i, 0)),
            pl.BlockSpec((1, N_MOD, D_MODEL), lambda i, kf: (i // tiles_per_batch, 0, 0)),
            pl.BlockSpec((1, D_MODEL), lambda i, kf: (0, 0)),
            pl.BlockSpec((1, D_MODEL, MLP_TF), lambda i, kf: (kf, 0, 0)),
            pl.BlockSpec((MLP_TF, D_MODEL), lambda i, kf: (kf, 0)),
            pl.BlockSpec((1, D_MODEL), lambda i, kf: (0, 0)),
        ],
        out_specs=pl.BlockSpec((MLP_TM, D_MODEL), lambda i, kf: (i, 0)),
        out_shape=jax.ShapeDtypeStruct((rows, D_MODEL), F32),
        scratch_shapes=[pltpu.VMEM((MLP_TM, D_MODEL), BF16), pltpu.VMEM((MLP_TM, D_MODEL), F32)],
        compiler_params=_cparams(("arbitrary", "arbitrary")),
        name="mlp",
    )(x1, mod3, n2g, w1, w2, fg)


def _rope_tables(rows):
    pair = np.arange(HEAD_DIM) // 2
    inv = (np.float32(1.0) / np.float32(ROPE_THETA) ** (
        (pair % ROPE_AXIS_PAIRS).astype(np.float32) / np.float32(ROPE_AXIS_PAIRS))).astype(np.float32)
    pos = np.arange(rows * GRID_W)[:, None]
    coord = np.where(pair[None, :] < ROPE_AXIS_PAIRS, pos // GRID_W, pos % GRID_W).astype(np.float32)
    ang = coord * inv[None, :]
    return jnp.asarray(np.stack([np.cos(ang), np.sin(ang)]).astype(np.float32))


def _dft_tables(size, period, scale):
    n_lo = 1 << ((size.bit_length() - 1) // 2)
    n_hi = size // n_lo
    assert n_lo * n_hi == size
    idx = jnp.arange(size, dtype=jnp.int32)

    def cos_sin(mult):
        ang = ((mult[:, None] * idx[None, :]) % period).astype(F32) * (2.0 * math.pi / period)
        return jnp.cos(ang), jnp.sin(ang)

    c_hi, s_hi = cos_sin(jnp.arange(n_hi, dtype=jnp.int32) * n_lo)
    c_lo, s_lo = cos_sin(jnp.arange(n_lo, dtype=jnp.int32))
    c_hi, s_hi = c_hi[:, None, :] * scale, s_hi[:, None, :] * scale
    c_lo, s_lo = c_lo[None, :, :], s_lo[None, :, :]
    cos = (c_hi * c_lo - s_hi * s_lo).reshape(size, size)
    sin = (s_hi * c_lo + c_hi * s_lo).reshape(size, size)
    return cos.astype(BF16), sin.astype(BF16)


def _fourier_tables(n_lat):
    half = n_lat // 2
    ct, st = _dft_tables(half, n_lat, n_lat ** -0.5)
    cc, sc = _dft_tables(FOURIER_GROUP_DIM, FOURIER_GROUP_DIM, FOURIER_GROUP_DIM ** -0.5)
    jrev = np.zeros((FOU_T, FOU_T), np.float32)
    jrev[np.arange(1, FOU_T), FOU_T - np.arange(1, FOU_T)] = 1.0
    alt = np.zeros((SUBLANES, n_lat), np.float32)
    alt[0] = (1.0 - 2.0 * (np.arange(n_lat) % 2)) * n_lat ** -0.5
    return ct, st, cc, sc, jnp.asarray(jrev, dtype=BF16), jnp.asarray(alt, dtype=BF16)


def kernel(x, c, ctx, c_ctx, w_ada, b_ada, norm1_g, w_in, q_norm_g, k_norm_g, w_attn_o, w_fourier,
           w_out, norm2_g, w1, w2, final_norm_g):
    batch, n_lat, _ = x.shape
    n_ctx = ctx.shape[1]
    assert w_ada.shape[0] == 1, "single-layer block"
    assert n_lat % INP_TM == 0 and n_ctx % SUBLANES == 0

    cc = jnp.concatenate([c, c_ctx[None, :]], axis=0)
    cb = jnp.broadcast_to(cc[:, :, None], (batch + 1, D_MODEL, LANES))
    mod = _adaln(cb, w_ada[0], b_ada)
    mod3 = mod.reshape(SUBLANES, N_MOD, D_MODEL)

    w_in_bf = w_in[0]
    n1g = norm1_g
    heads_per_sub = INP_SUB // HEAD_DIM
    qg, kg = jnp.tile(q_norm_g, (1, heads_per_sub)), jnp.tile(k_norm_g, (1, heads_per_sub))
    rope = _rope_tables(n_lat // GRID_W)

    x2 = x.reshape(batch * n_lat, D_MODEL)
    tiles_per_batch = n_lat // INP_TM
    p = _inproj(x2, mod3, lambda i: i // tiles_per_batch, n1g, w_in_bf, rope, qg, kg,
                tm=INP_TM, group=INP_GROUP, col_block0=0, n_col_blocks=IN_WIDTH // INP_TN, use_rope=True,
                rope_tiles=tiles_per_batch, name="inproj")
    ctx2 = ctx.reshape(batch * n_ctx, D_MODEL)
    pc = _inproj(ctx2, mod3, lambda i: batch, n1g, w_in_bf, rope, qg, kg,
                 tm=batch * n_ctx, group=1, col_block0=Q_END // INP_TN,
                 n_col_blocks=(V_END - Q_END) // INP_TN,
                 use_rope=False, rope_tiles=1, name="ctxproj")

    o, (wa_bf, wf_bf, wo_bf, w1_bf, w2_bf) = _attention(
        p, pc, batch, n_lat, n_ctx,
        ((w_attn_o[0], 1), (w_fourier[0], 1), (w_out[0], 1), (w1[0], D_FF // MLP_TF), (w2[0], 1)))

    f = _fourier(p, _fourier_tables(n_lat), batch, n_lat)

    x1 = _merge(o, f, p, x2, mod3, wa_bf, wf_bf, wo_bf, n_lat)
    out = _mlp(x1, mod3, norm2_g, w1_bf, w2_bf, final_norm_g[None, :], n_lat)
    return out.reshape(batch, n_lat, D_MODEL)
```

```python
import functools
import math

import jax
import jax.numpy as jnp
import numpy as np
from jax import lax
from jax.experimental import pallas as pl
from jax.experimental.pallas import tpu as pltpu

D_MODEL = 2048
GRID_W = 64
HEAD_DIM = 128
N_Q_HEADS = 16
N_KV_HEADS = 4
Q_PER_KV = N_Q_HEADS // N_KV_HEADS
ATTN_WIDTH = N_Q_HEADS * HEAD_DIM
KV_WIDTH = N_KV_HEADS * HEAD_DIM
N_FOURIER_GROUPS = 4
FOURIER_GROUP_DIM = 256
FOURIER_WIDTH = N_FOURIER_GROUPS * FOURIER_GROUP_DIM
Q_END = ATTN_WIDTH
K_END = Q_END + KV_WIDTH
V_END = K_END + KV_WIDTH
F_END = V_END + FOURIER_WIDTH
IN_WIDTH = F_END + 2 * D_MODEL
D_FF = 4 * D_MODEL
N_MOD = 6
ROPE_THETA = 10000.0
ROPE_AXIS_PAIRS = HEAD_DIM // 4
EPS = 1e-6

LANES = 128
SUBLANES = 8
VMEM_LIMIT = 56 * 1024 * 1024

BF16 = jnp.bfloat16
F32 = jnp.float32


def _cparams(sem):
    return pltpu.CompilerParams(dimension_semantics=sem, vmem_limit_bytes=VMEM_LIMIT)


def _dot(a, b):
    return jnp.dot(a, b, preferred_element_type=F32)


def _rmsnorm_rows(x, g):
    ms = jnp.mean(x * x, axis=-1, keepdims=True)
    return x * lax.rsqrt(ms + EPS) * g


NORM_ROWS = 128


def _norm_modulate(x_ref, g, scale, shift, h_ref):
    gs = g * (1.0 + scale)
    for r0 in range(0, x_ref.shape[0], NORM_ROWS):
        xa = x_ref[r0:r0 + NORM_ROWS, :]
        r = lax.rsqrt(jnp.mean(xa * xa, axis=-1, keepdims=True) + EPS)
        xb = x_ref[r0:r0 + NORM_ROWS, :]
        h_ref[r0:r0 + NORM_ROWS, :] = (xb * r * gs + shift).astype(BF16)


ADA_TN = 1024


def _adaln_kernel(cb_ref, w_ref, b_ref, out_ref, act_ref):
    n_rows = cb_ref.shape[0]

    @pl.when(pl.program_id(0) == 0)
    def _():
        for r in range(n_rows):
            cr = cb_ref[r]
            act_ref[r] = cr * jax.nn.sigmoid(cr)

    n_tiles = ADA_TN // LANES

    def slab(kc, accs):
        k0 = pl.multiple_of(kc * SUBLANES, SUBLANES)
        w8 = w_ref[pl.ds(k0, SUBLANES), :]
        a8 = [act_ref[r, pl.ds(k0, SUBLANES), :] for r in range(n_rows)]
        return tuple(accs[r * n_tiles + t] + w8[:, t * LANES:(t + 1) * LANES] * a8[r]
                     for r in range(n_rows) for t in range(n_tiles))

    zero = jnp.zeros((SUBLANES, LANES), F32)
    accs = lax.fori_loop(0, D_MODEL // SUBLANES, slab, (zero,) * (n_rows * n_tiles), unroll=4)
    out_ref[...] = jnp.zeros(out_ref.shape, F32)
    for r in range(n_rows):
        for t in range(n_tiles):
            tot = accs[r * n_tiles + t].sum(axis=0, keepdims=True)
            out_ref[r:r + 1, t * LANES:(t + 1) * LANES] = tot + b_ref[:, t * LANES:(t + 1) * LANES]


def _adaln(cb, w_ada, b_ada):
    n_out = w_ada.shape[1]
    return pl.pallas_call(
        _adaln_kernel,
        grid=(n_out // ADA_TN,),
        in_specs=[
            pl.BlockSpec(cb.shape, lambda j: (0, 0, 0)),
            pl.BlockSpec((D_MODEL, ADA_TN), lambda j: (0, j)),
            pl.BlockSpec((1, ADA_TN), lambda j: (0, j)),
        ],
        out_specs=pl.BlockSpec((SUBLANES, ADA_TN), lambda j: (0, j)),
        out_shape=jax.ShapeDtypeStruct((SUBLANES, n_out), F32),
        scratch_shapes=[pltpu.VMEM(cb.shape, F32)],
        compiler_params=_cparams(("arbitrary",)),
        name="adaln",
    )(cb, w_ada, b_ada)


INP_TM = 1024
INP_TN = 1024
INP_SUB = 256
INP_GROUP = 2
Q_SCALE = HEAD_DIM ** -0.5 * math.log2(math.e)


def _head_norm_rope(acc, g, sel_ref, rope_ref, mult):
    xg = acc * g
    ssq = _dot((acc * acc).astype(BF16), sel_ref[0])
    r = lax.rsqrt(ssq * (1.0 / HEAD_DIM) + EPS) * mult
    if rope_ref is None:
        return xg * r
    swapped = _dot(xg.astype(BF16), sel_ref[1])
    heads = []
    for hh in range(INP_SUB // HEAD_DIM):
        sl = slice(hh * HEAD_DIM, (hh + 1) * HEAD_DIM)
        heads.append(xg[:, sl] * rope_ref[0] + swapped[:, sl] * rope_ref[1])
    return jnp.concatenate(heads, axis=1) * r


def _col_kind(col):
    if col < Q_END:
        return "q"
    if col < K_END:
        return "k"
    if col < F_END:
        return "plain"
    return "gate"


def _inproj_kernel(x_ref, mod_ref, n1g_ref, w_ref, rope_ref, qg_ref, kg_ref, sel_ref, out_ref, h_ref,
                   *, col_block0, n_col_blocks, use_rope):
    j = pl.program_id(1)
    s = pl.program_id(2)

    def prologue():
        _norm_modulate(x_ref, n1g_ref[...], mod_ref[0, 1:2, :], mod_ref[0, 0:1, :], h_ref.at[s])

    def epilogue(acc, kind, c0):
        if kind in ("q", "k"):
            g_ref, mult = (qg_ref, Q_SCALE) if kind == "q" else (kg_ref, 1.0)
            res = _head_norm_rope(acc, g_ref[...], sel_ref, rope_ref if use_rope else None, mult)
            out_ref[:, c0:c0 + INP_SUB] = res.astype(BF16)
        elif kind == "plain":
            out_ref[:, c0:c0 + INP_SUB] = acc.astype(BF16)
        else:
            out_ref[:, c0:c0 + INP_SUB] = (0.5 * jnp.tanh(0.5 * acc) + 0.5).astype(BF16)

    def tile(col_block):
        subs = range(0, INP_TN, INP_SUB)
        pending = None
        for c0 in subs:
            acc = _dot(h_ref[s], w_ref[:, c0:c0 + INP_SUB].astype(BF16))
            if pending is not None:
                epilogue(*pending)
            pending = (acc, _col_kind(col_block * INP_TN + c0), c0)
        epilogue(*pending)

    def kinds_of(cb):
        return tuple(_col_kind(cb * INP_TN + c0) for c0 in range(0, INP_TN, INP_SUB))

    @pl.when(j == 0)
    def _():
        prologue()
        tile(col_block0)

    lo = 1
    while lo < n_col_blocks:
        hi = lo
        while hi + 1 < n_col_blocks and kinds_of(col_block0 + hi + 1) == kinds_of(col_block0 + lo):
            hi += 1
        pl.when(jnp.logical_and(j >= lo, j <= hi))(functools.partial(tile, col_block0 + lo))
        lo = hi + 1


def _head_select_matrices():
    idx = np.arange(INP_SUB)
    ones_bd = (idx[:, None] // HEAD_DIM == idx[None, :] // HEAD_DIM).astype(np.float32)
    swap = np.zeros((INP_SUB, INP_SUB), np.float32)
    swap[idx[1::2], idx[0::2]] = -1.0
    swap[idx[0::2], idx[1::2]] = 1.0
    return jnp.asarray(np.stack([ones_bd, swap]), dtype=BF16)


def _inproj(x2, mod3, mod_row_of_tile, n1g, w_in, rope, qg, kg, *, tm, group, col_block0,
            n_col_blocks, use_rope, rope_tiles, name):
    rows = x2.shape[0]
    sel = _head_select_matrices()
    kern = functools.partial(_inproj_kernel, col_block0=col_block0, n_col_blocks=n_col_blocks,
                             use_rope=use_rope)
    assert rows % (tm * group) == 0

    def tile_of(g, s):
        return g * group + s

    def x_index(g, j, s):
        return (tile_of(g, jnp.where(j == 0, s, group - 1)), 0)

    return pl.pallas_call(
        kern,
        grid=(rows // (tm * group), n_col_blocks, group),
        in_specs=[
            pl.BlockSpec((tm, D_MODEL), x_index),
            pl.BlockSpec((1, N_MOD, D_MODEL), lambda g, j, s: (mod_row_of_tile(tile_of(g, s)), 0, 0)),
            pl.BlockSpec((1, D_MODEL), lambda g, j, s: (0, 0)),
            pl.BlockSpec((D_MODEL, INP_TN), lambda g, j, s: (0, j + col_block0)),
            pl.BlockSpec((2, tm, HEAD_DIM), lambda g, j, s: (0, tile_of(g, s) % rope_tiles, 0)),
            pl.BlockSpec((1, INP_SUB), lambda g, j, s: (0, 0)),
            pl.BlockSpec((1, INP_SUB), lambda g, j, s: (0, 0)),
            pl.BlockSpec(sel.shape, lambda g, j, s: (0, 0, 0)),
        ],
        out_specs=pl.BlockSpec((tm, INP_TN), lambda g, j, s: (tile_of(g, s), j)),
        out_shape=jax.ShapeDtypeStruct((rows, n_col_blocks * INP_TN), BF16),
        scratch_shapes=[pltpu.VMEM((group, tm, D_MODEL), BF16)],
        compiler_params=_cparams(("arbitrary", "arbitrary", "arbitrary")),
        name=name,
    )(x2, mod3, n1g, w_in, rope, qg, kg, sel)


ATT_SUB = 256
ATT_NSUB = 4
ATT_TQ = ATT_SUB * ATT_NSUB
ATT_CHUNK = 1024


def _key_chunks(n_keys):
    chunks = [(c0, ATT_CHUNK) for c0 in range(0, n_keys - ATT_CHUNK + 1, ATT_CHUNK)]
    done = len(chunks) * ATT_CHUNK
    if done < n_keys:
        chunks.append((done, n_keys - done))
    return chunks


def _attn_kernel(q_ref, qn_ref, k_ref, v_ref, kc_ref, vc_ref, *rest, n_cast):
    w_refs, o_ref, wb_refs = rest[:n_cast], rest[n_cast], rest[n_cast + 1:2 * n_cast + 1]
    kall_ref, vt_ref, s_ref, m_ref = rest[2 * n_cast + 1:]
    qi = pl.program_id(2)
    n_lat = k_ref.shape[0]
    n_keys = kall_ref.shape[0]

    units = [(sub, g) for sub in range(ATT_NSUB) for g in range(Q_PER_KV)]
    assert len(units) % 2 == 0
    chunks = _key_chunks(n_keys)

    def q_of(ref, u):
        sub, g = units[u]
        return ref[sub * ATT_SUB:(sub + 1) * ATT_SUB, g * HEAD_DIM:(g + 1) * HEAD_DIM]

    def scores_chunk(q_u, slot, c0, cs, m_run):
        s_c = lax.dot_general(kall_ref[c0:c0 + cs, :], q_u, (((1,), (1,)), ((), ())),
                              preferred_element_type=F32)
        s_ref[slot, c0:c0 + cs, :] = s_c
        mc = jnp.max(s_c, axis=0, keepdims=True)
        return mc if m_run is None else jnp.maximum(m_run, mc)

    @pl.when(qi == 0)
    def _():
        kall_ref[0:n_lat, :] = k_ref[...]
        kall_ref[n_lat:, :] = kc_ref[...]
        vt_ref[:, 0:n_lat] = v_ref[...].astype(F32).T.astype(BF16)
        vt_ref[:, n_lat:] = vc_ref[...].astype(F32).T.astype(BF16)
        m0 = None
        for c0, cs in chunks:
            m0 = scores_chunk(q_of(q_ref, 0), 0, c0, cs, m0)
        m_ref[...] = m0

    for w_ref, wb_ref in zip(w_refs, wb_refs):
        if len(wb_ref.shape) == 2:
            wb_ref[...] = w_ref[...].astype(BF16)
        else:
            width = wb_ref.shape[2]
            for c in range(wb_ref.shape[0]):
                wb_ref[c] = w_ref[:, c * width:(c + 1) * width].astype(BF16)

    m_cur = m_ref[...]
    for t in range(len(units)):
        last = t + 1 == len(units)
        q_nxt = q_of(qn_ref, 0) if last else q_of(q_ref, t + 1)
        m_nxt = l = o_acc = None
        for c0, cs in chunks:
            m_nxt = scores_chunk(q_nxt, (t + 1) % 2, c0, cs, m_nxt)
            p_c = jnp.exp2(s_ref[t % 2, c0:c0 + cs, :] - m_cur)
            lc = jnp.sum(p_c, axis=0, keepdims=True)
            l = lc if l is None else l + lc
            pv = _dot(vt_ref[:, c0:c0 + cs], p_c.astype(BF16))
            o_acc = pv if o_acc is None else o_acc + pv
        sub, g = units[t]
        o_ref[sub * ATT_SUB:(sub + 1) * ATT_SUB, g * HEAD_DIM:(g + 1) * HEAD_DIM] = (
            (o_acc * (1.0 / l)).T.astype(BF16))
        m_cur = m_nxt
    m_ref[...] = m_cur


def _attention(p, pc, batch, n_lat, n_ctx, weights):
    q_tiles = n_lat // ATT_TQ
    kcol = Q_END // HEAD_DIM
    vcol = K_END // HEAD_DIM
    n_keys = n_lat + n_ctx
    n_steps = batch * N_KV_HEADS * q_tiles
    in_slabs, out_slabs, out_shapes = [], [], []
    for w, chunks in weights:
        rows, cols = w.shape
        slab = rows // n_steps
        assert slab * n_steps == rows and slab % (2 * SUBLANES) == 0 and cols % chunks == 0
        step_of = lambda b, h, qi: (b * N_KV_HEADS + h) * q_tiles + qi
        in_slabs.append(pl.BlockSpec((slab, cols), lambda b, h, qi: (step_of(b, h, qi), 0)))
        if chunks == 1:
            out_slabs.append(in_slabs[-1])
            out_shapes.append(jax.ShapeDtypeStruct((rows, cols), BF16))
        else:
            out_slabs.append(pl.BlockSpec((chunks, slab, cols // chunks),
                                          lambda b, h, qi: (0, step_of(b, h, qi), 0)))
            out_shapes.append(jax.ShapeDtypeStruct((chunks, rows, cols // chunks), BF16))
    results = pl.pallas_call(
        functools.partial(_attn_kernel, n_cast=len(weights)),
        grid=(batch, N_KV_HEADS, q_tiles),
        in_specs=[
            pl.BlockSpec((ATT_TQ, Q_PER_KV * HEAD_DIM), lambda b, h, qi: (b * q_tiles + qi, h)),
            pl.BlockSpec((ATT_TQ, Q_PER_KV * HEAD_DIM),
                         lambda b, h, qi: (b * q_tiles + jnp.minimum(qi + 1, q_tiles - 1), h)),
            pl.BlockSpec((n_lat, HEAD_DIM), lambda b, h, qi: (b, kcol + h)),
            pl.BlockSpec((n_lat, HEAD_DIM), lambda b, h, qi: (b, vcol + h)),
            pl.BlockSpec((n_ctx, HEAD_DIM), lambda b, h, qi: (b, h)),
            pl.BlockSpec((n_ctx, HEAD_DIM), lambda b, h, qi: (b, N_KV_HEADS + h)),
        ] + in_slabs,
        out_specs=[pl.BlockSpec((ATT_TQ, Q_PER_KV * HEAD_DIM),
                                lambda b, h, qi: (b * q_tiles + qi, h))] + out_slabs,
        out_shape=[jax.ShapeDtypeStruct((batch * n_lat, ATTN_WIDTH), BF16)] + out_shapes,
        scratch_shapes=[pltpu.VMEM((n_keys, HEAD_DIM), BF16), pltpu.VMEM((HEAD_DIM, n_keys), BF16),
                        pltpu.VMEM((2, n_keys, ATT_SUB), F32), pltpu.VMEM((1, ATT_SUB), F32)],
        compiler_params=_cparams(("arbitrary", "arbitrary", "arbitrary")),
        name="attn",
    )(p, p, p, p, pc, pc, *[w for w, _ in weights])
    return results[0], results[1:]


FOU_T = 256


def _fourier_kernel(u_ref, ct_ref, st_ref, cc_ref, sc_ref, jrev_ref, alt_ref, f_ref,
                    e_ref, o_ref, carry_ref, *, n_lat):
    s = pl.program_id(1)
    half = n_lat // 2
    tiles = half // FOU_T
    row = lax.broadcasted_iota(jnp.int32, (FOU_T, FOURIER_WIDTH), 0)

    def chan_dft(g_mat, h_mat):
        fd, fm = [], []
        for g in range(N_FOURIER_GROUPS):
            sl = slice(g * FOURIER_GROUP_DIM, (g + 1) * FOURIER_GROUP_DIM)
            gc = _dot(g_mat[:, sl].astype(BF16), cc_ref[...].astype(BF16))
            if h_mat is None:
                fd.append(gc)
                continue
            hs = _dot(h_mat[:, sl].astype(BF16), sc_ref[...].astype(BF16))
            fd.append(gc - hs)
            fm.append(gc + hs)
        return jnp.concatenate(fd, axis=1), (jnp.concatenate(fm, axis=1) if fm else None)

    @pl.when(s == 0)
    def _():
        for t in range(tiles):
            lo = u_ref[t * FOU_T:(t + 1) * FOU_T, :].astype(F32)
            hi_tile = u_ref[(2 * tiles - 1 - t) * FOU_T:(2 * tiles - t) * FOU_T, :]
            part = _dot(jrev_ref[...], hi_tile)
            if t == 0:
                first = jnp.zeros((1, FOURIER_WIDTH), F32)
            else:
                r0 = (2 * tiles - t) * FOU_T
                first = u_ref[r0:r0 + 1, :].astype(F32)
            part = jnp.where(row == 0, first, part)
            e_ref[t * FOU_T:(t + 1) * FOU_T, :] = (lo + part).astype(BF16)
            o_ref[t * FOU_T:(t + 1) * FOU_T, :] = (lo - part).astype(BF16)
        g_nyq = _dot(alt_ref[...], u_ref[...])
        f_nyq, _ = chan_dft(g_nyq, None)
        carry_ref[...] = f_nyq

    scale = n_lat ** -0.5
    u_half = u_ref[half:half + 1, :].astype(F32) * scale
    g_mat = (_dot(ct_ref[...].astype(BF16), e_ref[...])
             + jnp.where(row % 2 == 0, u_half, -u_half))
    h_mat = _dot(st_ref[...].astype(BF16), o_ref[...])
    fd, fm = chan_dft(g_mat, h_mat)
    f_ref[0, 0, 0] = fd.astype(BF16)
    fm_bf = fm.astype(BF16)
    mirrored = jnp.where(row == 0, carry_ref[0:1, :], _dot(jrev_ref[...], fm_bf))
    f_ref[0, 1, 0] = mirrored.astype(BF16)
    carry_ref[0:1, :] = fm_bf[0:1, :].astype(F32)


def _fourier(p, tables, batch, n_lat):
    ct, st, cc, sc, jrev, alt = tables
    ucol = V_END // FOURIER_WIDTH
    tiles = n_lat // 2 // FOU_T
    const2 = lambda b, s: (0, 0)
    return pl.pallas_call(
        functools.partial(_fourier_kernel, n_lat=n_lat),
        grid=(batch, tiles),
        in_specs=[
            pl.BlockSpec((n_lat, FOURIER_WIDTH), lambda b, s: (b, ucol)),
            pl.BlockSpec((FOU_T, n_lat // 2), lambda b, s: (tiles - 1 - s, 0)),
            pl.BlockSpec((FOU_T, n_lat // 2), lambda b, s: (tiles - 1 - s, 0)),
            pl.BlockSpec(cc.shape, const2),
            pl.BlockSpec(sc.shape, const2),
            pl.BlockSpec(jrev.shape, const2),
            pl.BlockSpec(alt.shape, const2),
        ],
        out_specs=pl.BlockSpec((1, 2, 1, FOU_T, FOURIER_WIDTH), lambda b, s: (b, 0, tiles - 1 - s, 0, 0)),
        out_shape=jax.ShapeDtypeStruct((batch, 2, tiles, FOU_T, FOURIER_WIDTH), BF16),
        scratch_shapes=[pltpu.VMEM((n_lat // 2, FOURIER_WIDTH), BF16),
                        pltpu.VMEM((n_lat // 2, FOURIER_WIDTH), BF16),
                        pltpu.VMEM((SUBLANES, FOURIER_WIDTH), F32)],
        compiler_params=_cparams(("arbitrary", "arbitrary")),
        name="fourier",
    )(p, ct, st, cc, sc, jrev, alt)


MRG_TM = FOU_T


def _merge_kernel(o_ref, f_ref, gates_ref, x_ref, mod_ref, wa_ref, wf_ref, wo_ref, x1_ref):
    a = _dot(o_ref[...], wa_ref[...])
    fm = _dot(f_ref[0, 0, 0], wf_ref[...])
    z = gates_ref[:, 0:D_MODEL].astype(F32) * a + gates_ref[:, D_MODEL:].astype(F32) * fm
    y = _dot(z.astype(BF16), wo_ref[...])
    x1_ref[...] = x_ref[...] + mod_ref[0, 2:3, :] * y


def _merge(o, f, p, x2, mod3, wa, wf, wo, n_lat):
    rows = x2.shape[0]
    tiles_per_batch = n_lat // MRG_TM
    const = lambda i: (0, 0)
    single = pl.Buffered(1)

    def f_index(i):
        t = i % tiles_per_batch
        upper = t // (tiles_per_batch // 2)
        return (i // tiles_per_batch, upper, jnp.where(upper == 1, tiles_per_batch - 1 - t, t), 0, 0)

    return pl.pallas_call(
        _merge_kernel,
        grid=(rows // MRG_TM,),
        in_specs=[
            pl.BlockSpec((MRG_TM, ATTN_WIDTH), lambda i: (i, 0)),
            pl.BlockSpec((1, 1, 1, MRG_TM, FOURIER_WIDTH), f_index),
            pl.BlockSpec((MRG_TM, 2 * D_MODEL), lambda i: (i, F_END // (2 * D_MODEL))),
            pl.BlockSpec((MRG_TM, D_MODEL), lambda i: (i, 0)),
            pl.BlockSpec((1, N_MOD, D_MODEL), lambda i: (i // tiles_per_batch, 0, 0)),
            pl.BlockSpec(wa.shape, const, pipeline_mode=single),
            pl.BlockSpec(wf.shape, const, pipeline_mode=single),
            pl.BlockSpec(wo.shape, const, pipeline_mode=single),
        ],
        out_specs=pl.BlockSpec((MRG_TM, D_MODEL), lambda i: (i, 0)),
        out_shape=jax.ShapeDtypeStruct((rows, D_MODEL), F32),
        compiler_params=_cparams(("arbitrary",)),
        name="merge",
    )(o, f, p, x2, mod3, wa, wf, wo)


MLP_TM = 512
MLP_TF = 1024


def _mlp_kernel(x1_ref, mod_ref, n2g_ref, w1_ref, w2_ref, fg_ref, out_ref, h_ref, acc_ref):
    kf = pl.program_id(1)

    @pl.when(kf == 0)
    def _():
        _norm_modulate(x1_ref, n2g_ref[...], mod_ref[0, 4:5, :], mod_ref[0, 3:4, :], h_ref)

    def chunk():
        hid = jnp.maximum(_dot(h_ref[...], w1_ref[0]), 0.0)
        return _dot((hid * hid).astype(BF16), w2_ref[...])

    last = pl.num_programs(1) - 1

    @pl.when(kf == 0)
    def _():
        acc_ref[...] = chunk()

    @pl.when(jnp.logical_and(kf > 0, kf < last))
    def _():
        acc_ref[...] += chunk()

    @pl.when(kf == last)
    def _():
        x2 = x1_ref[...] + mod_ref[0, 5:6, :] * (acc_ref[...] + chunk())
        out_ref[...] = _rmsnorm_rows(x2, fg_ref[...])


def _mlp(x1, mod3, n2g, w1, w2, fg, n_lat):
    rows = x1.shape[0]
    tiles_per_batch = n_lat // MLP_TM
    return pl.pallas_call(
        _mlp_kernel,
        grid=(rows // MLP_TM, D_FF // MLP_TF),
        in_specs=[
            pl.BlockSpec((MLP_TM, D_MODEL), lambda i, kf: (i, 0)),
            pl.BlockSpec((1, N_MOD, D_MODEL), lambda i, kf: (i // tiles_per_batch, 0, 0)),
            pl.BlockSpec((1, D_MODEL), lambda i, kf: (0, 0)),
            pl.BlockSpec((1, D_MODEL, MLP_TF), lambda i, kf: (kf, 0, 0)),
            pl.BlockSpec((MLP_TF, D_MODEL), lambda i, kf: (kf, 0)),
            pl.BlockSpec((1, D_MODEL), lambda i, kf: (0, 0)),
        ],
        out_specs=pl.BlockSpec((MLP_TM, D_MODEL), lambda i, kf: (i, 0)),
        out_shape=jax.ShapeDtypeStruct((rows, D_MODEL), F32),
        scratch_shapes=[pltpu.VMEM((MLP_TM, D_MODEL), BF16), pltpu.VMEM((MLP_TM, D_MODEL), F32)],
        compiler_params=_cparams(("arbitrary", "arbitrary")),
        name="mlp",
    )(x1, mod3, n2g, w1, w2, fg)


def _rope_tables(rows):
    pair = np.arange(HEAD_DIM) // 2
    inv = (np.float32(1.0) / np.float32(ROPE_THETA) ** (
        (pair % ROPE_AXIS_PAIRS).astype(np.float32) / np.float32(ROPE_AXIS_PAIRS))).astype(np.float32)
    pos = np.arange(rows * GRID_W)[:, None]
    coord = np.where(pair[None, :] < ROPE_AXIS_PAIRS, pos // GRID_W, pos % GRID_W).astype(np.float32)
    ang = coord * inv[None, :]
    return jnp.asarray(np.stack([np.cos(ang), np.sin(ang)]).astype(np.float32))


def _dft_tables(size, period, scale):
    idx = np.arange(size, dtype=np.int64)
    ang = ((idx[:, None] * idx[None, :]) % period) * (2.0 * math.pi / period)
    return (jnp.asarray((scale * np.cos(ang)).astype(np.float32)),
            jnp.asarray((scale * np.sin(ang)).astype(np.float32)))


def _fourier_tables(n_lat):
    half = n_lat // 2
    ct, st = _dft_tables(half, n_lat, n_lat ** -0.5)
    cc, sc = _dft_tables(FOURIER_GROUP_DIM, FOURIER_GROUP_DIM, FOURIER_GROUP_DIM ** -0.5)
    jrev = np.zeros((FOU_T, FOU_T), np.float32)
    jrev[np.arange(1, FOU_T), FOU_T - np.arange(1, FOU_T)] = 1.0
    alt = np.zeros((SUBLANES, n_lat), np.float32)
    alt[0] = (1.0 - 2.0 * (np.arange(n_lat) % 2)) * n_lat ** -0.5
    return ct, st, cc, sc, jnp.asarray(jrev, dtype=BF16), jnp.asarray(alt, dtype=BF16)


def kernel(x, c, ctx, c_ctx, w_ada, b_ada, norm1_g, w_in, q_norm_g, k_norm_g, w_attn_o, w_fourier,
           w_out, norm2_g, w1, w2, final_norm_g):
    batch, n_lat, _ = x.shape
    n_ctx = ctx.shape[1]
    assert w_ada.shape[0] == 1, "single-layer block"
    assert n_lat % INP_TM == 0 and n_ctx % SUBLANES == 0

    cc = jnp.concatenate([c, c_ctx[None, :]], axis=0)
    cb = jnp.broadcast_to(cc[:, :, None], (batch + 1, D_MODEL, LANES))
    mod = _adaln(cb, w_ada[0], b_ada)
    mod3 = mod.reshape(SUBLANES, N_MOD, D_MODEL)

    w_in_bf = w_in[0]
    n1g = norm1_g
    heads_per_sub = INP_SUB // HEAD_DIM
    qg, kg = jnp.tile(q_norm_g, (1, heads_per_sub)), jnp.tile(k_norm_g, (1, heads_per_sub))
    rope = _rope_tables(n_lat // GRID_W)

    x2 = x.reshape(batch * n_lat, D_MODEL)
    tiles_per_batch = n_lat // INP_TM
    p = _inproj(x2, mod3, lambda i: i // tiles_per_batch, n1g, w_in_bf, rope, qg, kg,
                tm=INP_TM, group=INP_GROUP, col_block0=0, n_col_blocks=IN_WIDTH // INP_TN, use_rope=True,
                rope_tiles=tiles_per_batch, name="inproj")
    ctx2 = ctx.reshape(batch * n_ctx, D_MODEL)
    pc = _inproj(ctx2, mod3, lambda i: batch, n1g, w_in_bf, rope, qg, kg,
                 tm=batch * n_ctx, group=1, col_block0=Q_END // INP_TN,
                 n_col_blocks=(V_END - Q_END) // INP_TN,
                 use_rope=False, rope_tiles=1, name="ctxproj")

    o, (wa_bf, wf_bf, wo_bf, w1_bf, w2_bf) = _attention(
        p, pc, batch, n_lat, n_ctx,
        ((w_attn_o[0], 1), (w_fourier[0], 1), (w_out[0], 1), (w1[0], D_FF // MLP_TF), (w2[0], 1)))

    f = _fourier(p, _fourier_tables(n_lat), batch, n_lat)

    x1 = _merge(o, f, p, x2, mod3, wa_bf, wf_bf, wo_bf, n_lat)
    out = _mlp(x1, mod3, norm2_g, w1_bf, w2_bf, final_norm_g[None, :], n_lat)
    return out.reshape(batch, n_lat, D_MODEL)
```

```python
import functools
import math

import jax
import jax.numpy as jnp
import numpy as np
from jax import lax
from jax.experimental import pallas as pl
from jax.experimental.pallas import tpu as pltpu

D_MODEL = 2048
GRID_W = 64
HEAD_DIM = 128
N_Q_HEADS = 16
N_KV_HEADS = 4
Q_PER_KV = N_Q_HEADS // N_KV_HEADS
ATTN_WIDTH = N_Q_HEADS * HEAD_DIM
KV_WIDTH = N_KV_HEADS * HEAD_DIM
N_FOURIER_GROUPS = 4
FOURIER_GROUP_DIM = 256
FOURIER_WIDTH = N_FOURIER_GROUPS * FOURIER_GROUP_DIM
Q_END = ATTN_WIDTH
K_END = Q_END + KV_WIDTH
V_END = K_END + KV_WIDTH
F_END = V_END + FOURIER_WIDTH
IN_WIDTH = F_END + 2 * D_MODEL
D_FF = 4 * D_MODEL
N_MOD = 6
ROPE_THETA = 10000.0
ROPE_AXIS_PAIRS = HEAD_DIM // 4
EPS = 1e-6

LANES = 128
SUBLANES = 8
VMEM_LIMIT = 56 * 1024 * 1024

BF16 = jnp.bfloat16
F32 = jnp.float32


def _cparams(sem):
    return pltpu.CompilerParams(dimension_semantics=sem, vmem_limit_bytes=VMEM_LIMIT)


def _dot(a, b):
    return jnp.dot(a, b, preferred_element_type=F32)


def _rmsnorm_rows(x, g):
    ms = jnp.mean(x * x, axis=-1, keepdims=True)
    return x * lax.rsqrt(ms + EPS) * g


NORM_ROWS = 128


def _norm_modulate(x_ref, g, scale, shift, h_ref):
    gs = g * (1.0 + scale)
    for r0 in range(0, x_ref.shape[0], NORM_ROWS):
        xa = x_ref[r0:r0 + NORM_ROWS, :]
        r = lax.rsqrt(jnp.mean(xa * xa, axis=-1, keepdims=True) + EPS)
        xb = x_ref[r0:r0 + NORM_ROWS, :]
        h_ref[r0:r0 + NORM_ROWS, :] = (xb * r * gs + shift).astype(BF16)


ADA_TN = 1024


def _adaln_kernel(cb_ref, w_ref, b_ref, out_ref, act_ref):
    n_rows = cb_ref.shape[0]

    @pl.when(pl.program_id(0) == 0)
    def _():
        for r in range(n_rows):
            cr = cb_ref[r]
            act_ref[r] = cr * jax.nn.sigmoid(cr)

    n_tiles = ADA_TN // LANES

    def slab(kc, accs):
        k0 = pl.multiple_of(kc * SUBLANES, SUBLANES)
        w8 = w_ref[pl.ds(k0, SUBLANES), :]
        a8 = [act_ref[r, pl.ds(k0, SUBLANES), :] for r in range(n_rows)]
        return tuple(accs[r * n_tiles + t] + w8[:, t * LANES:(t + 1) * LANES] * a8[r]
                     for r in range(n_rows) for t in range(n_tiles))

    zero = jnp.zeros((SUBLANES, LANES), F32)
    accs = lax.fori_loop(0, D_MODEL // SUBLANES, slab, (zero,) * (n_rows * n_tiles), unroll=4)
    out_ref[...] = jnp.zeros(out_ref.shape, F32)
    for r in range(n_rows):
        for t in range(n_tiles):
            tot = accs[r * n_tiles + t].sum(axis=0, keepdims=True)
            out_ref[r:r + 1, t * LANES:(t + 1) * LANES] = tot + b_ref[:, t * LANES:(t + 1) * LANES]


def _adaln(cb, w_ada, b_ada):
    n_out = w_ada.shape[1]
    return pl.pallas_call(
        _adaln_kernel,
        grid=(n_out // ADA_TN,),
        in_specs=[
            pl.BlockSpec(cb.shape, lambda j: (0, 0, 0)),
            pl.BlockSpec((D_MODEL, ADA_TN), lambda j: (0, j)),
            pl.BlockSpec((1, ADA_TN), lambda j: (0, j)),
        ],
        out_specs=pl.BlockSpec((SUBLANES, ADA_TN), lambda j: (0, j)),
        out_shape=jax.ShapeDtypeStruct((SUBLANES, n_out), F32),
        scratch_shapes=[pltpu.VMEM(cb.shape, F32)],
        compiler_params=_cparams(("arbitrary",)),
        name="adaln",
    )(cb, w_ada, b_ada)


INP_TM = 1024
INP_TN = 1024
INP_SUB = 256
INP_GROUP = 2
Q_SCALE = HEAD_DIM ** -0.5 * math.log2(math.e)


def _head_norm_rope(acc, g, sel_ref, rope_ref, mult):
    xg = acc * g
    swapped = None
    if rope_ref is not None:
        swapped = _dot(xg.astype(BF16), sel_ref[...])
    heads = []
    for hh in range(INP_SUB // HEAD_DIM):
        sl = slice(hh * HEAD_DIM, (hh + 1) * HEAD_DIM)
        ah = acc[:, sl]
        r = lax.rsqrt(jnp.mean(ah * ah, axis=-1, keepdims=True) + EPS) * mult
        if swapped is None:
            heads.append(xg[:, sl] * r)
        else:
            heads.append((xg[:, sl] * rope_ref[0] + swapped[:, sl] * rope_ref[1]) * r)
    return jnp.concatenate(heads, axis=1)


def _col_kind(col):
    if col < Q_END:
        return "q"
    if col < K_END:
        return "k"
    if col < F_END:
        return "plain"
    return "gate"


def _inproj_kernel(x_ref, mod_ref, n1g_ref, w_ref, rope_ref, qg_ref, kg_ref, sel_ref, out_ref, h_ref,
                   *, col_block0, n_col_blocks, use_rope):
    j = pl.program_id(1)
    s = pl.program_id(2)

    def prologue():
        _norm_modulate(x_ref, n1g_ref[...], mod_ref[0, 1:2, :], mod_ref[0, 0:1, :], h_ref.at[s])

    def epilogue(acc, kind, c0):
        if kind in ("q", "k"):
            g_ref, mult = (qg_ref, Q_SCALE) if kind == "q" else (kg_ref, 1.0)
            res = _head_norm_rope(acc, g_ref[...], sel_ref, rope_ref if use_rope else None, mult)
            out_ref[:, c0:c0 + INP_SUB] = res.astype(BF16)
        elif kind == "plain":
            out_ref[:, c0:c0 + INP_SUB] = acc.astype(BF16)
        else:
            out_ref[:, c0:c0 + INP_SUB] = (0.5 * jnp.tanh(0.5 * acc) + 0.5).astype(BF16)

    def tile(col_block):
        subs = range(0, INP_TN, INP_SUB)
        pending = None
        for c0 in subs:
            acc = _dot(h_ref[s], w_ref[:, c0:c0 + INP_SUB].astype(BF16))
            if pending is not None:
                epilogue(*pending)
            pending = (acc, _col_kind(col_block * INP_TN + c0), c0)
        epilogue(*pending)

    def kinds_of(cb):
        return tuple(_col_kind(cb * INP_TN + c0) for c0 in range(0, INP_TN, INP_SUB))

    @pl.when(j == 0)
    def _():
        prologue()
        tile(col_block0)

    lo = 1
    while lo < n_col_blocks:
        hi = lo
        while hi + 1 < n_col_blocks and kinds_of(col_block0 + hi + 1) == kinds_of(col_block0 + lo):
            hi += 1
        pl.when(jnp.logical_and(j >= lo, j <= hi))(functools.partial(tile, col_block0 + lo))
        lo = hi + 1


def _pair_swap_matrix():
    idx = np.arange(INP_SUB)
    swap = np.zeros((INP_SUB, INP_SUB), np.float32)
    swap[idx[1::2], idx[0::2]] = -1.0
    swap[idx[0::2], idx[1::2]] = 1.0
    return jnp.asarray(swap, dtype=BF16)


def _inproj(x2, mod3, mod_row_of_tile, n1g, w_in, rope, qg, kg, *, tm, group, col_block0,
            n_col_blocks, use_rope, rope_tiles, name):
    rows = x2.shape[0]
    sel = _pair_swap_matrix()
    kern = functools.partial(_inproj_kernel, col_block0=col_block0, n_col_blocks=n_col_blocks,
                             use_rope=use_rope)
    assert rows % (tm * group) == 0

    def tile_of(g, s):
        return g * group + s

    def x_index(g, j, s):
        return (tile_of(g, jnp.where(j == 0, s, group - 1)), 0)

    return pl.pallas_call(
        kern,
        grid=(rows // (tm * group), n_col_blocks, group),
        in_specs=[
            pl.BlockSpec((tm, D_MODEL), x_index),
            pl.BlockSpec((1, N_MOD, D_MODEL), lambda g, j, s: (mod_row_of_tile(tile_of(g, s)), 0, 0)),
            pl.BlockSpec((1, D_MODEL), lambda g, j, s: (0, 0)),
            pl.BlockSpec((D_MODEL, INP_TN), lambda g, j, s: (0, j + col_block0)),
            pl.BlockSpec((2, tm, HEAD_DIM), lambda g, j, s: (0, tile_of(g, s) % rope_tiles, 0)),
            pl.BlockSpec((1, INP_SUB), lambda g, j, s: (0, 0)),
            pl.BlockSpec((1, INP_SUB), lambda g, j, s: (0, 0)),
            pl.BlockSpec(sel.shape, lambda g, j, s: (0, 0)),
        ],
        out_specs=pl.BlockSpec((tm, INP_TN), lambda g, j, s: (tile_of(g, s), j)),
        out_shape=jax.ShapeDtypeStruct((rows, n_col_blocks * INP_TN), BF16),
        scratch_shapes=[pltpu.VMEM((group, tm, D_MODEL), BF16)],
        compiler_params=_cparams(("arbitrary", "arbitrary", "arbitrary")),
        name=name,
    )(x2, mod3, n1g, w_in, rope, qg, kg, sel)


ATT_SUB = 256
ATT_NSUB = 4
ATT_TQ = ATT_SUB * ATT_NSUB
ATT_CHUNK = 2048


def _key_chunks(n_keys):
    chunks = [(c0, ATT_CHUNK) for c0 in range(0, n_keys - ATT_CHUNK + 1, ATT_CHUNK)]
    done = len(chunks) * ATT_CHUNK
    if done < n_keys:
        chunks.append((done, n_keys - done))
    return chunks


def _attn_kernel(q_ref, qn_ref, k_ref, v_ref, kc_ref, vc_ref, *rest, n_cast):
    w_refs, o_ref, wb_refs = rest[:n_cast], rest[n_cast], rest[n_cast + 1:2 * n_cast + 1]
    kall_ref, vt_ref, s_ref, m_ref = rest[2 * n_cast + 1:]
    qi = pl.program_id(2)
    n_lat = k_ref.shape[0]
    n_keys = kall_ref.shape[0]

    units = [(sub, g) for sub in range(ATT_NSUB) for g in range(Q_PER_KV)]
    assert len(units) % 2 == 0
    chunks = _key_chunks(n_keys)

    def q_of(ref, u):
        sub, g = units[u]
        return ref[sub * ATT_SUB:(sub + 1) * ATT_SUB, g * HEAD_DIM:(g + 1) * HEAD_DIM]

    def scores_chunk(q_u, slot, c0, cs, m_run):
        s_c = lax.dot_general(kall_ref[c0:c0 + cs, :], q_u, (((1,), (1,)), ((), ())),
                              preferred_element_type=F32)
        s_ref[slot, c0:c0 + cs, :] = s_c
        mc = jnp.max(s_c, axis=0, keepdims=True)
        return mc if m_run is None else jnp.maximum(m_run, mc)

    @pl.when(qi == 0)
    def _():
        kall_ref[0:n_lat, :] = k_ref[...]
        kall_ref[n_lat:, :] = kc_ref[...]
        vt_ref[:, 0:n_lat] = v_ref[...].astype(F32).T.astype(BF16)
        vt_ref[:, n_lat:] = vc_ref[...].astype(F32).T.astype(BF16)
        m0 = None
        for c0, cs in chunks:
            m0 = scores_chunk(q_of(q_ref, 0), 0, c0, cs, m0)
        m_ref[...] = m0

    for w_ref, wb_ref in zip(w_refs, wb_refs):
        if len(wb_ref.shape) == 2:
            wb_ref[...] = w_ref[...].astype(BF16)
        else:
            width = wb_ref.shape[2]
            for c in range(wb_ref.shape[0]):
                wb_ref[c] = w_ref[:, c * width:(c + 1) * width].astype(BF16)

    m_cur = m_ref[...]
    for t in range(len(units)):
        last = t + 1 == len(units)
        q_nxt = q_of(qn_ref, 0) if last else q_of(q_ref, t + 1)
        m_nxt = l = o_acc = None
        for c0, cs in chunks:
            m_nxt = scores_chunk(q_nxt, (t + 1) % 2, c0, cs, m_nxt)
            p_c = jnp.exp2(s_ref[t % 2, c0:c0 + cs, :] - m_cur)
            lc = jnp.sum(p_c, axis=0, keepdims=True)
            l = lc if l is None else l + lc
            pv = _dot(vt_ref[:, c0:c0 + cs], p_c.astype(BF16))
            o_acc = pv if o_acc is None else o_acc + pv
        sub, g = units[t]
        o_ref[sub * ATT_SUB:(sub + 1) * ATT_SUB, g * HEAD_DIM:(g + 1) * HEAD_DIM] = (
            (o_acc * (1.0 / l)).T.astype(BF16))
        m_cur = m_nxt
    m_ref[...] = m_cur


def _attention(p, pc, batch, n_lat, n_ctx, weights):
    q_tiles = n_lat // ATT_TQ
    kcol = Q_END // HEAD_DIM
    vcol = K_END // HEAD_DIM
    n_keys = n_lat + n_ctx
    n_steps = batch * N_KV_HEADS * q_tiles
    in_slabs, out_slabs, out_shapes = [], [], []
    for w, chunks in weights:
        rows, cols = w.shape
        slab = rows // n_steps
        assert slab * n_steps == rows and slab % (2 * SUBLANES) == 0 and cols % chunks == 0
        step_of = lambda b, h, qi: (b * N_KV_HEADS + h) * q_tiles + qi
        in_slabs.append(pl.BlockSpec((slab, cols), lambda b, h, qi: (step_of(b, h, qi), 0)))
        if chunks == 1:
            out_slabs.append(in_slabs[-1])
            out_shapes.append(jax.ShapeDtypeStruct((rows, cols), BF16))
        else:
            out_slabs.append(pl.BlockSpec((chunks, slab, cols // chunks),
                                          lambda b, h, qi: (0, step_of(b, h, qi), 0)))
            out_shapes.append(jax.ShapeDtypeStruct((chunks, rows, cols // chunks), BF16))
    results = pl.pallas_call(
        functools.partial(_attn_kernel, n_cast=len(weights)),
        grid=(batch, N_KV_HEADS, q_tiles),
        in_specs=[
            pl.BlockSpec((ATT_TQ, Q_PER_KV * HEAD_DIM), lambda b, h, qi: (b * q_tiles + qi, h)),
            pl.BlockSpec((ATT_TQ, Q_PER_KV * HEAD_DIM),
                         lambda b, h, qi: (b * q_tiles + jnp.minimum(qi + 1, q_tiles - 1), h)),
            pl.BlockSpec((n_lat, HEAD_DIM), lambda b, h, qi: (b, kcol + h)),
            pl.BlockSpec((n_lat, HEAD_DIM), lambda b, h, qi: (b, vcol + h)),
            pl.BlockSpec((n_ctx, HEAD_DIM), lambda b, h, qi: (b, h)),
            pl.BlockSpec((n_ctx, HEAD_DIM), lambda b, h, qi: (b, N_KV_HEADS + h)),
        ] + in_slabs,
        out_specs=[pl.BlockSpec((ATT_TQ, Q_PER_KV * HEAD_DIM),
                                lambda b, h, qi: (b * q_tiles + qi, h))] + out_slabs,
        out_shape=[jax.ShapeDtypeStruct((batch * n_lat, ATTN_WIDTH), BF16)] + out_shapes,
        scratch_shapes=[pltpu.VMEM((n_keys, HEAD_DIM), BF16), pltpu.VMEM((HEAD_DIM, n_keys), BF16),
                        pltpu.VMEM((2, n_keys, ATT_SUB), F32), pltpu.VMEM((1, ATT_SUB), F32)],
        compiler_params=_cparams(("arbitrary", "arbitrary", "arbitrary")),
        name="attn",
    )(p, p, p, p, pc, pc, *[w for w, _ in weights])
    return results[0], results[1:]


FOU_T = 256


def _fourier_kernel(u_ref, ct_ref, st_ref, cc_ref, sc_ref, jrev_ref, alt_ref, f_ref,
                    e_ref, o_ref, carry_ref, *, n_lat):
    s = pl.program_id(1)
    half = n_lat // 2
    tiles = half // FOU_T
    row = lax.broadcasted_iota(jnp.int32, (FOU_T, FOURIER_WIDTH), 0)

    def chan_dft(g_mat, h_mat):
        fd, fm = [], []
        for g in range(N_FOURIER_GROUPS):
            sl = slice(g * FOURIER_GROUP_DIM, (g + 1) * FOURIER_GROUP_DIM)
            gc = _dot(g_mat[:, sl].astype(BF16), cc_ref[...].astype(BF16))
            if h_mat is None:
                fd.append(gc)
                continue
            hs = _dot(h_mat[:, sl].astype(BF16), sc_ref[...].astype(BF16))
            fd.append(gc - hs)
            fm.append(gc + hs)
        return jnp.concatenate(fd, axis=1), (jnp.concatenate(fm, axis=1) if fm else None)

    @pl.when(s == 0)
    def _():
        for t in range(tiles):
            lo = u_ref[t * FOU_T:(t + 1) * FOU_T, :].astype(F32)
            hi_tile = u_ref[(2 * tiles - 1 - t) * FOU_T:(2 * tiles - t) * FOU_T, :]
            part = _dot(jrev_ref[...], hi_tile)
            if t == 0:
                first = jnp.zeros((1, FOURIER_WIDTH), F32)
            else:
                r0 = (2 * tiles - t) * FOU_T
                first = u_ref[r0:r0 + 1, :].astype(F32)
            part = jnp.where(row == 0, first, part)
            e_ref[t * FOU_T:(t + 1) * FOU_T, :] = (lo + part).astype(BF16)
            o_ref[t * FOU_T:(t + 1) * FOU_T, :] = (lo - part).astype(BF16)
        g_nyq = _dot(alt_ref[...], u_ref[...])
        f_nyq, _ = chan_dft(g_nyq, None)
        carry_ref[...] = f_nyq

    scale = n_lat ** -0.5
    u_half = u_ref[half:half + 1, :].astype(F32) * scale
    g_mat = (_dot(ct_ref[...].astype(BF16), e_ref[...])
             + jnp.where(row % 2 == 0, u_half, -u_half))
    h_mat = _dot(st_ref[...].astype(BF16), o_ref[...])
    fd, fm = chan_dft(g_mat, h_mat)
    f_ref[0, 0, 0] = fd.astype(BF16)
    fm_bf = fm.astype(BF16)
    mirrored = jnp.where(row == 0, carry_ref[0:1, :], _dot(jrev_ref[...], fm_bf))
    f_ref[0, 1, 0] = mirrored.astype(BF16)
    carry_ref[0:1, :] = fm_bf[0:1, :].astype(F32)


def _fourier(p, tables, batch, n_lat):
    ct, st, cc, sc, jrev, alt = tables
    ucol = V_END // FOURIER_WIDTH
    tiles = n_lat // 2 // FOU_T
    const2 = lambda b, s: (0, 0)
    return pl.pallas_call(
        functools.partial(_fourier_kernel, n_lat=n_lat),
        grid=(batch, tiles),
        in_specs=[
            pl.BlockSpec((n_lat, FOURIER_WIDTH), lambda b, s: (b, ucol)),
            pl.BlockSpec((FOU_T, n_lat // 2), lambda b, s: (tiles - 1 - s, 0)),
            pl.BlockSpec((FOU_T, n_lat // 2), lambda b, s: (tiles - 1 - s, 0)),
            pl.BlockSpec(cc.shape, const2),
            pl.BlockSpec(sc.shape, const2),
            pl.BlockSpec(jrev.shape, const2),
            pl.BlockSpec(alt.shape, const2),
        ],
        out_specs=pl.BlockSpec((1, 2, 1, FOU_T, FOURIER_WIDTH), lambda b, s: (b, 0, tiles - 1 - s, 0, 0)),
        out_shape=jax.ShapeDtypeStruct((batch, 2, tiles, FOU_T, FOURIER_WIDTH), BF16),
        scratch_shapes=[pltpu.VMEM((n_lat // 2, FOURIER_WIDTH), BF16),
                        pltpu.VMEM((n_lat // 2, FOURIER_WIDTH), BF16),
                        pltpu.VMEM((SUBLANES, FOURIER_WIDTH), F32)],
        compiler_params=_cparams(("arbitrary", "arbitrary")),
        name="fourier",
    )(p, ct, st, cc, sc, jrev, alt)


MRG_TM = FOU_T


def _merge_kernel(o_ref, f_ref, gates_ref, x_ref, mod_ref, wa_ref, wf_ref, wo_ref, x1_ref):
    a = _dot(o_ref[...], wa_ref[...])
    fm = _dot(f_ref[0, 0, 0], wf_ref[...])
    z = gates_ref[:, 0:D_MODEL].astype(F32) * a + gates_ref[:, D_MODEL:].astype(F32) * fm
    y = _dot(z.astype(BF16), wo_ref[...])
    x1_ref[...] = x_ref[...] + mod_ref[0, 2:3, :] * y


def _merge(o, f, p, x2, mod3, wa, wf, wo, n_lat):
    rows = x2.shape[0]
    tiles_per_batch = n_lat // MRG_TM
    const = lambda i: (0, 0)
    single = pl.Buffered(1)

    def f_index(i):
        t = i % tiles_per_batch
        upper = t // (tiles_per_batch // 2)
        return (i // tiles_per_batch, upper, jnp.where(upper == 1, tiles_per_batch - 1 - t, t), 0, 0)

    return pl.pallas_call(
        _merge_kernel,
        grid=(rows // MRG_TM,),
        in_specs=[
            pl.BlockSpec((MRG_TM, ATTN_WIDTH), lambda i: (i, 0)),
            pl.BlockSpec((1, 1, 1, MRG_TM, FOURIER_WIDTH), f_index),
            pl.BlockSpec((MRG_TM, 2 * D_MODEL), lambda i: (i, F_END // (2 * D_MODEL))),
            pl.BlockSpec((MRG_TM, D_MODEL), lambda i: (i, 0)),
            pl.BlockSpec((1, N_MOD, D_MODEL), lambda i: (i // tiles_per_batch, 0, 0)),
            pl.BlockSpec(wa.shape, const, pipeline_mode=single),
            pl.BlockSpec(wf.shape, const, pipeline_mode=single),
            pl.BlockSpec(wo.shape, const, pipeline_mode=single),
        ],
        out_specs=pl.BlockSpec((MRG_TM, D_MODEL), lambda i: (i, 0)),
        out_shape=jax.ShapeDtypeStruct((rows, D_MODEL), F32),
        compiler_params=_cparams(("arbitrary",)),
        name="merge",
    )(o, f, p, x2, mod3, wa, wf, wo)


MLP_TM = 512
MLP_TF = 1024


def _mlp_kernel(x1_ref, mod_ref, n2g_ref, w1_ref, w2_ref, fg_ref, out_ref, h_ref, acc_ref):
    kf = pl.program_id(1)

    @pl.when(kf == 0)
    def _():
        _norm_modulate(x1_ref, n2g_ref[...], mod_ref[0, 4:5, :], mod_ref[0, 3:4, :], h_ref)

    def chunk():
        hid = jnp.maximum(_dot(h_ref[...], w1_ref[0]), 0.0)
        return _dot((hid * hid).astype(BF16), w2_ref[...])

    last = pl.num_programs(1) - 1

    @pl.when(kf == 0)
    def _():
        acc_ref[...] = chunk()

    @pl.when(jnp.logical_and(kf > 0, kf < last))
    def _():
        acc_ref[...] += chunk()

    @pl.when(kf == last)
    def _():
        x2 = x1_ref[...] + mod_ref[0, 5:6, :] * (acc_ref[...] + chunk())
        out_ref[...] = _rmsnorm_rows(x2, fg_ref[...])


def _mlp(x1, mod3, n2g, w1, w2, fg, n_lat):
    rows = x1.shape[0]
    tiles_per_batch = n_lat // MLP_TM
    return pl.pallas_call(
        _mlp_kernel,
        grid=(rows // MLP_TM, D_FF // MLP_TF),
        in_specs=[
            pl.BlockSpec((MLP_TM, D_MODEL), lambda i, kf: (i, 0)),
            pl.BlockSpec((1, N_MOD, D_MODEL), lambda i, kf: (i // tiles_per_batch, 0, 0)),
            pl.BlockSpec((1, D_MODEL), lambda i, kf: (0, 0)),
            pl.BlockSpec((1, D_MODEL, MLP_TF), lambda i, kf: (kf, 0, 0)),
            pl.BlockSpec((MLP_TF, D_MODEL), lambda i, kf: (kf, 0)),
            pl.BlockSpec((1, D_MODEL), lambda i, kf: (0, 0)),
        ],
        out_specs=pl.BlockSpec((MLP_TM, D_MODEL), lambda i, kf: (i, 0)),
        out_shape=jax.ShapeDtypeStruct((rows, D_MODEL), F32),
        scratch_shapes=[pltpu.VMEM((MLP_TM, D_MODEL), BF16), pltpu.VMEM((MLP_TM, D_MODEL), F32)],
        compiler_params=_cparams(("arbitrary", "arbitrary")),
        name="mlp",
    )(x1, mod3, n2g, w1, w2, fg)


def _rope_tables(rows):
    pair = np.arange(HEAD_DIM) // 2
    inv = (np.float32(1.0) / np.float32(ROPE_THETA) ** (
        (pair % ROPE_AXIS_PAIRS).astype(np.float32) / np.float32(ROPE_AXIS_PAIRS))).astype(np.float32)
    pos = np.arange(rows * GRID_W)[:, None]
    coord = np.where(pair[None, :] < ROPE_AXIS_PAIRS, pos // GRID_W, pos % GRID_W).astype(np.float32)
    ang = coord * inv[None, :]
    return jnp.asarray(np.stack([np.cos(ang), np.sin(ang)]).astype(np.float32))


def _dft_tables(size, period, scale):
    idx = np.arange(size, dtype=np.int64)
    ang = ((idx[:, None] * idx[None, :]) % period) * (2.0 * math.pi / period)
    return (jnp.asarray((scale * np.cos(ang)).astype(np.float32)),
            jnp.asarray((scale * np.sin(ang)).astype(np.float32)))


def _fourier_tables(n_lat):
    half = n_lat // 2
    ct, st = _dft_tables(half, n_lat, n_lat ** -0.5)
    cc, sc = _dft_tables(FOURIER_GROUP_DIM, FOURIER_GROUP_DIM, FOURIER_GROUP_DIM ** -0.5)
    jrev = np.zeros((FOU_T, FOU_T), np.float32)
    jrev[np.arange(1, FOU_T), FOU_T - np.arange(1, FOU_T)] = 1.0
    alt = np.zeros((SUBLANES, n_lat), np.float32)
    alt[0] = (1.0 - 2.0 * (np.arange(n_lat) % 2)) * n_lat ** -0.5
    return ct, st, cc, sc, jnp.asarray(jrev, dtype=BF16), jnp.asarray(alt, dtype=BF16)


def kernel(x, c, ctx, c_ctx, w_ada, b_ada, norm1_g, w_in, q_norm_g, k_norm_g, w_attn_o, w_fourier,
           w_out, norm2_g, w1, w2, final_norm_g):
    batch, n_lat, _ = x.shape
    n_ctx = ctx.shape[1]
    assert w_ada.shape[0] == 1, "single-layer block"
    assert n_lat % INP_TM == 0 and n_ctx % SUBLANES == 0

    cc = jnp.concatenate([c, c_ctx[None, :]], axis=0)
    cb = jnp.broadcast_to(cc[:, :, None], (batch + 1, D_MODEL, LANES))
    mod = _adaln(cb, w_ada[0], b_ada)
    mod3 = mod.reshape(SUBLANES, N_MOD, D_MODEL)

    w_in_bf = w_in[0]
    n1g = norm1_g
    heads_per_sub = INP_SUB // HEAD_DIM
    qg, kg = jnp.tile(q_norm_g, (1, heads_per_sub)), jnp.tile(k_norm_g, (1, heads_per_sub))
    rope = _rope_tables(n_lat // GRID_W)

    x2 = x.reshape(batch * n_lat, D_MODEL)
    tiles_per_batch = n_lat // INP_TM
    p = _inproj(x2, mod3, lambda i: i // tiles_per_batch, n1g, w_in_bf, rope, qg, kg,
                tm=INP_TM, group=INP_GROUP, col_block0=0, n_col_blocks=IN_WIDTH // INP_TN, use_rope=True,
                rope_tiles=tiles_per_batch, name="inproj")
    ctx2 = ctx.reshape(batch * n_ctx, D_MODEL)
    pc = _inproj(ctx2, mod3, lambda i: batch, n1g, w_in_bf, rope, qg, kg,
                 tm=batch * n_ctx, group=1, col_block0=Q_END // INP_TN,
                 n_col_blocks=(V_END - Q_END) // INP_TN,
                 use_rope=False, rope_tiles=1, name="ctxproj")

    o, (wa_bf, wf_bf, wo_bf, w1_bf, w2_bf) = _attention(
        p, pc, batch, n_lat, n_ctx,
        ((w_attn_o[0], 1), (w_fourier[0], 1), (w_out[0], 1), (w1[0], D_FF // MLP_TF), (w2[0], 1)))

    f = _fourier(p, _fourier_tables(n_lat), batch, n_lat)

    x1 = _merge(o, f, p, x2, mod3, wa_bf, wf_bf, wo_bf, n_lat)
    out = _mlp(x1, mod3, norm2_g, w1_bf, w2_bf, final_norm_g[None, :], n_lat)
    return out.reshape(batch, n_lat, D_MODEL)
```

```python
import functools
import math

import jax
import jax.numpy as jnp
import numpy as np
from jax import lax
from jax.experimental import pallas as pl
from jax.experimental.pallas import tpu as pltpu

D_MODEL = 2048
GRID_W = 64
HEAD_DIM = 128
N_Q_HEADS = 16
N_KV_HEADS = 4
Q_PER_KV = N_Q_HEADS // N_KV_HEADS
ATTN_WIDTH = N_Q_HEADS * HEAD_DIM
KV_WIDTH = N_KV_HEADS * HEAD_DIM
N_FOURIER_GROUPS = 4
FOURIER_GROUP_DIM = 256
FOURIER_WIDTH = N_FOURIER_GROUPS * FOURIER_GROUP_DIM
Q_END = ATTN_WIDTH
K_END = Q_END + KV_WIDTH
V_END = K_END + KV_WIDTH
F_END = V_END + FOURIER_WIDTH
IN_WIDTH = F_END + 2 * D_MODEL
D_FF = 4 * D_MODEL
N_MOD = 6
ROPE_THETA = 10000.0
ROPE_AXIS_PAIRS = HEAD_DIM // 4
EPS = 1e-6

LANES = 128
SUBLANES = 8
VMEM_LIMIT = 56 * 1024 * 1024

BF16 = jnp.bfloat16
F32 = jnp.float32


def _cparams(sem):
    return pltpu.CompilerParams(dimension_semantics=sem, vmem_limit_bytes=VMEM_LIMIT)


def _dot(a, b):
    return jnp.dot(a, b, preferred_element_type=F32)


def _rmsnorm_rows(x, g):
    ms = jnp.mean(x * x, axis=-1, keepdims=True)
    return x * lax.rsqrt(ms + EPS) * g


NORM_ROWS = 128


def _norm_modulate(x_ref, g, scale, shift, h_ref):
    gs = g * (1.0 + scale)
    for r0 in range(0, x_ref.shape[0], NORM_ROWS):
        xa = x_ref[r0:r0 + NORM_ROWS, :]
        r = lax.rsqrt(jnp.mean(xa * xa, axis=-1, keepdims=True) + EPS)
        xb = x_ref[r0:r0 + NORM_ROWS, :]
        h_ref[r0:r0 + NORM_ROWS, :] = (xb * r * gs + shift).astype(BF16)


ADA_TN = 1024


def _adaln_kernel(cb_ref, w_ref, b_ref, out_ref, act_ref):
    n_rows = cb_ref.shape[0]

    @pl.when(pl.program_id(0) == 0)
    def _():
        for r in range(n_rows):
            cr = cb_ref[r]
            act_ref[r] = cr * jax.nn.sigmoid(cr)

    n_tiles = ADA_TN // LANES

    def slab(kc, accs):
        k0 = pl.multiple_of(kc * SUBLANES, SUBLANES)
        w8 = w_ref[pl.ds(k0, SUBLANES), :]
        a8 = [act_ref[r, pl.ds(k0, SUBLANES), :] for r in range(n_rows)]
        return tuple(accs[r * n_tiles + t] + w8[:, t * LANES:(t + 1) * LANES] * a8[r]
                     for r in range(n_rows) for t in range(n_tiles))

    zero = jnp.zeros((SUBLANES, LANES), F32)
    accs = lax.fori_loop(0, D_MODEL // SUBLANES, slab, (zero,) * (n_rows * n_tiles), unroll=4)
    out_ref[...] = jnp.zeros(out_ref.shape, F32)
    for r in range(n_rows):
        for t in range(n_tiles):
            tot = accs[r * n_tiles + t].sum(axis=0, keepdims=True)
            out_ref[r:r + 1, t * LANES:(t + 1) * LANES] = tot + b_ref[:, t * LANES:(t + 1) * LANES]


def _adaln(cb, w_ada, b_ada):
    n_out = w_ada.shape[1]
    return pl.pallas_call(
        _adaln_kernel,
        grid=(n_out // ADA_TN,),
        in_specs=[
            pl.BlockSpec(cb.shape, lambda j: (0, 0, 0)),
            pl.BlockSpec((D_MODEL, ADA_TN), lambda j: (0, j)),
            pl.BlockSpec((1, ADA_TN), lambda j: (0, j)),
        ],
        out_specs=pl.BlockSpec((SUBLANES, ADA_TN), lambda j: (0, j)),
        out_shape=jax.ShapeDtypeStruct((SUBLANES, n_out), F32),
        scratch_shapes=[pltpu.VMEM(cb.shape, F32)],
        compiler_params=_cparams(("arbitrary",)),
        name="adaln",
    )(cb, w_ada, b_ada)


INP_TM = 1024
INP_TN = 1024
INP_SUB = 256
INP_GROUP = 2
Q_SCALE = HEAD_DIM ** -0.5 * math.log2(math.e)


def _head_norm_rope(acc, g, sel_ref, rope_ref, mult):
    xg = acc * g
    swapped = None
    if rope_ref is not None:
        swapped = _dot(xg.astype(BF16), sel_ref[...])
    heads = []
    for hh in range(INP_SUB // HEAD_DIM):
        sl = slice(hh * HEAD_DIM, (hh + 1) * HEAD_DIM)
        ah = acc[:, sl]
        r = lax.rsqrt(jnp.mean(ah * ah, axis=-1, keepdims=True) + EPS) * mult
        if swapped is None:
            heads.append(xg[:, sl] * r)
        else:
            heads.append((xg[:, sl] * rope_ref[0] + swapped[:, sl] * rope_ref[1]) * r)
    return jnp.concatenate(heads, axis=1)


def _col_kind(col):
    if col < Q_END:
        return "q"
    if col < K_END:
        return "k"
    if col < F_END:
        return "plain"
    return "gate"


def _inproj_kernel(x_ref, mod_ref, n1g_ref, w_ref, rope_ref, qg_ref, kg_ref, sel_ref, out_ref, h_ref,
                   *, col_block0, n_col_blocks, use_rope):
    j = pl.program_id(1)
    s = pl.program_id(2)

    def prologue():
        _norm_modulate(x_ref, n1g_ref[...], mod_ref[0, 1:2, :], mod_ref[0, 0:1, :], h_ref.at[s])

    def epilogue(acc, kind, c0):
        if kind in ("q", "k"):
            g_ref, mult = (qg_ref, Q_SCALE) if kind == "q" else (kg_ref, 1.0)
            res = _head_norm_rope(acc, g_ref[...], sel_ref, rope_ref if use_rope else None, mult)
            out_ref[:, c0:c0 + INP_SUB] = res.astype(BF16)
        elif kind == "plain":
            out_ref[:, c0:c0 + INP_SUB] = acc.astype(BF16)
        else:
            out_ref[:, c0:c0 + INP_SUB] = (0.5 * jnp.tanh(0.5 * acc) + 0.5).astype(BF16)

    def tile(col_block):
        subs = range(0, INP_TN, INP_SUB)
        pending = None
        for c0 in subs:
            acc = _dot(h_ref[s], w_ref[:, c0:c0 + INP_SUB].astype(BF16))
            if pending is not None:
                epilogue(*pending)
            pending = (acc, _col_kind(col_block * INP_TN + c0), c0)
        epilogue(*pending)

    def kinds_of(cb):
        return tuple(_col_kind(cb * INP_TN + c0) for c0 in range(0, INP_TN, INP_SUB))

    @pl.when(j == 0)
    def _():
        prologue()
        tile(col_block0)

    lo = 1
    while lo < n_col_blocks:
        hi = lo
        while hi + 1 < n_col_blocks and kinds_of(col_block0 + hi + 1) == kinds_of(col_block0 + lo):
            hi += 1
        pl.when(jnp.logical_and(j >= lo, j <= hi))(functools.partial(tile, col_block0 + lo))
        lo = hi + 1


def _pair_swap_matrix():
    idx = np.arange(INP_SUB)
    swap = np.zeros((INP_SUB, INP_SUB), np.float32)
    swap[idx[1::2], idx[0::2]] = -1.0
    swap[idx[0::2], idx[1::2]] = 1.0
    return jnp.asarray(swap, dtype=BF16)


def _inproj(x2, mod3, mod_row_of_tile, n1g, w_in, rope, qg, kg, *, tm, group, col_block0,
            n_col_blocks, use_rope, rope_tiles, name):
    rows = x2.shape[0]
    sel = _pair_swap_matrix()
    kern = functools.partial(_inproj_kernel, col_block0=col_block0, n_col_blocks=n_col_blocks,
                             use_rope=use_rope)
    assert rows % (tm * group) == 0

    def tile_of(g, s):
        return g * group + s

    def x_index(g, j, s):
        return (tile_of(g, jnp.where(j == 0, s, group - 1)), 0)

    return pl.pallas_call(
        kern,
        grid=(rows // (tm * group), n_col_blocks, group),
        in_specs=[
            pl.BlockSpec((tm, D_MODEL), x_index),
            pl.BlockSpec((1, N_MOD, D_MODEL), lambda g, j, s: (mod_row_of_tile(tile_of(g, s)), 0, 0)),
            pl.BlockSpec((1, D_MODEL), lambda g, j, s: (0, 0)),
            pl.BlockSpec((D_MODEL, INP_TN), lambda g, j, s: (0, j + col_block0)),
            pl.BlockSpec((2, tm, HEAD_DIM), lambda g, j, s: (0, tile_of(g, s) % rope_tiles, 0)),
            pl.BlockSpec((1, INP_SUB), lambda g, j, s: (0, 0)),
            pl.BlockSpec((1, INP_SUB), lambda g, j, s: (0, 0)),
            pl.BlockSpec(sel.shape, lambda g, j, s: (0, 0)),
        ],
        out_specs=pl.BlockSpec((tm, INP_TN), lambda g, j, s: (tile_of(g, s), j)),
        out_shape=jax.ShapeDtypeStruct((rows, n_col_blocks * INP_TN), BF16),
        scratch_shapes=[pltpu.VMEM((group, tm, D_MODEL), BF16)],
        compiler_params=_cparams(("arbitrary", "arbitrary", "arbitrary")),
        name=name,
    )(x2, mod3, n1g, w_in, rope, qg, kg, sel)


ATT_SUB = 256
ATT_NSUB = 4
ATT_TQ = ATT_SUB * ATT_NSUB
ATT_CHUNK = 512


def _key_chunks(n_keys):
    chunks = [(c0, ATT_CHUNK) for c0 in range(0, n_keys - ATT_CHUNK + 1, ATT_CHUNK)]
    done = len(chunks) * ATT_CHUNK
    if done < n_keys:
        chunks.append((done, n_keys - done))
    return chunks


def _attn_kernel(q_ref, qn_ref, k_ref, v_ref, kc_ref, vc_ref, *rest, n_cast):
    w_refs, o_ref, wb_refs = rest[:n_cast], rest[n_cast], rest[n_cast + 1:2 * n_cast + 1]
    kall_ref, vt_ref, s_ref, m_ref = rest[2 * n_cast + 1:]
    qi = pl.program_id(2)
    n_lat = k_ref.shape[0]
    n_keys = kall_ref.shape[0]

    units = [(sub, g) for sub in range(ATT_NSUB) for g in range(Q_PER_KV)]
    assert len(units) % 2 == 0
    chunks = _key_chunks(n_keys)

    def q_of(ref, u):
        sub, g = units[u]
        return ref[sub * ATT_SUB:(sub + 1) * ATT_SUB, g * HEAD_DIM:(g + 1) * HEAD_DIM]

    def scores_chunk(q_u, slot, c0, cs, m_run):
        s_c = lax.dot_general(kall_ref[c0:c0 + cs, :], q_u, (((1,), (1,)), ((), ())),
                              preferred_element_type=F32)
        s_ref[slot, c0:c0 + cs, :] = s_c
        mc = jnp.max(s_c, axis=0, keepdims=True)
        return mc if m_run is None else jnp.maximum(m_run, mc)

    @pl.when(qi == 0)
    def _():
        kall_ref[0:n_lat, :] = k_ref[...]
        kall_ref[n_lat:, :] = kc_ref[...]
        vt_ref[:, 0:n_lat] = v_ref[...].astype(F32).T.astype(BF16)
        vt_ref[:, n_lat:] = vc_ref[...].astype(F32).T.astype(BF16)
        m0 = None
        for c0, cs in chunks:
            m0 = scores_chunk(q_of(q_ref, 0), 0, c0, cs, m0)
        m_ref[...] = m0

    for w_ref, wb_ref in zip(w_refs, wb_refs):
        if len(wb_ref.shape) == 2:
            wb_ref[...] = w_ref[...].astype(BF16)
        else:
            width = wb_ref.shape[2]
            for c in range(wb_ref.shape[0]):
                wb_ref[c] = w_ref[:, c * width:(c + 1) * width].astype(BF16)

    m_cur = m_ref[...]
    for t in range(len(units)):
        last = t + 1 == len(units)
        q_nxt = q_of(qn_ref, 0) if last else q_of(q_ref, t + 1)
        m_nxt = l = o_acc = None
        for c0, cs in chunks:
            m_nxt = scores_chunk(q_nxt, (t + 1) % 2, c0, cs, m_nxt)
            p_c = jnp.exp2(s_ref[t % 2, c0:c0 + cs, :] - m_cur)
            lc = jnp.sum(p_c, axis=0, keepdims=True)
            l = lc if l is None else l + lc
            pv = _dot(vt_ref[:, c0:c0 + cs], p_c.astype(BF16))
            o_acc = pv if o_acc is None else o_acc + pv
        sub, g = units[t]
        o_ref[sub * ATT_SUB:(sub + 1) * ATT_SUB, g * HEAD_DIM:(g + 1) * HEAD_DIM] = (
            (o_acc * (1.0 / l)).T.astype(BF16))
        m_cur = m_nxt
    m_ref[...] = m_cur


def _attention(p, pc, batch, n_lat, n_ctx, weights):
    q_tiles = n_lat // ATT_TQ
    kcol = Q_END // HEAD_DIM
    vcol = K_END // HEAD_DIM
    n_keys = n_lat + n_ctx
    n_steps = batch * N_KV_HEADS * q_tiles
    in_slabs, out_slabs, out_shapes = [], [], []
    for w, chunks in weights:
        rows, cols = w.shape
        slab = rows // n_steps
        assert slab * n_steps == rows and slab % (2 * SUBLANES) == 0 and cols % chunks == 0
        step_of = lambda b, h, qi: (b * N_KV_HEADS + h) * q_tiles + qi
        in_slabs.append(pl.BlockSpec((slab, cols), lambda b, h, qi: (step_of(b, h, qi), 0)))
        if chunks == 1:
            out_slabs.append(in_slabs[-1])
            out_shapes.append(jax.ShapeDtypeStruct((rows, cols), BF16))
        else:
            out_slabs.append(pl.BlockSpec((chunks, slab, cols // chunks),
                                          lambda b, h, qi: (0, step_of(b, h, qi), 0)))
            out_shapes.append(jax.ShapeDtypeStruct((chunks, rows, cols // chunks), BF16))
    results = pl.pallas_call(
        functools.partial(_attn_kernel, n_cast=len(weights)),
        grid=(batch, N_KV_HEADS, q_tiles),
        in_specs=[
            pl.BlockSpec((ATT_TQ, Q_PER_KV * HEAD_DIM), lambda b, h, qi: (b * q_tiles + qi, h)),
            pl.BlockSpec((ATT_TQ, Q_PER_KV * HEAD_DIM),
                         lambda b, h, qi: (b * q_tiles + jnp.minimum(qi + 1, q_tiles - 1), h)),
            pl.BlockSpec((n_lat, HEAD_DIM), lambda b, h, qi: (b, kcol + h)),
            pl.BlockSpec((n_lat, HEAD_DIM), lambda b, h, qi: (b, vcol + h)),
            pl.BlockSpec((n_ctx, HEAD_DIM), lambda b, h, qi: (b, h)),
            pl.BlockSpec((n_ctx, HEAD_DIM), lambda b, h, qi: (b, N_KV_HEADS + h)),
        ] + in_slabs,
        out_specs=[pl.BlockSpec((ATT_TQ, Q_PER_KV * HEAD_DIM),
                                lambda b, h, qi: (b * q_tiles + qi, h))] + out_slabs,
        out_shape=[jax.ShapeDtypeStruct((batch * n_lat, ATTN_WIDTH), BF16)] + out_shapes,
        scratch_shapes=[pltpu.VMEM((n_keys, HEAD_DIM), BF16), pltpu.VMEM((HEAD_DIM, n_keys), BF16),
                        pltpu.VMEM((2, n_keys, ATT_SUB), F32), pltpu.VMEM((1, ATT_SUB), F32)],
        compiler_params=_cparams(("arbitrary", "arbitrary", "arbitrary")),
        name="attn",
    )(p, p, p, p, pc, pc, *[w for w, _ in weights])
    return results[0], results[1:]


FOU_T = 256


def _fourier_kernel(u_ref, ct_ref, st_ref, cc_ref, sc_ref, jrev_ref, alt_ref, f_ref,
                    e_ref, o_ref, carry_ref, *, n_lat):
    s = pl.program_id(1)
    half = n_lat // 2
    tiles = half // FOU_T
    row = lax.broadcasted_iota(jnp.int32, (FOU_T, FOURIER_WIDTH), 0)

    def chan_dft(g_mat, h_mat):
        fd, fm = [], []
        for g in range(N_FOURIER_GROUPS):
            sl = slice(g * FOURIER_GROUP_DIM, (g + 1) * FOURIER_GROUP_DIM)
            gc = _dot(g_mat[:, sl].astype(BF16), cc_ref[...].astype(BF16))
            if h_mat is None:
                fd.append(gc)
                continue
            hs = _dot(h_mat[:, sl].astype(BF16), sc_ref[...].astype(BF16))
            fd.append(gc - hs)
            fm.append(gc + hs)
        return jnp.concatenate(fd, axis=1), (jnp.concatenate(fm, axis=1) if fm else None)

    @pl.when(s == 0)
    def _():
        for t in range(tiles):
            lo = u_ref[t * FOU_T:(t + 1) * FOU_T, :].astype(F32)
            hi_tile = u_ref[(2 * tiles - 1 - t) * FOU_T:(2 * tiles - t) * FOU_T, :]
            part = _dot(jrev_ref[...], hi_tile)
            if t == 0:
                first = jnp.zeros((1, FOURIER_WIDTH), F32)
            else:
                r0 = (2 * tiles - t) * FOU_T
                first = u_ref[r0:r0 + 1, :].astype(F32)
            part = jnp.where(row == 0, first, part)
            e_ref[t * FOU_T:(t + 1) * FOU_T, :] = (lo + part).astype(BF16)
            o_ref[t * FOU_T:(t + 1) * FOU_T, :] = (lo - part).astype(BF16)
        g_nyq = _dot(alt_ref[...], u_ref[...])
        f_nyq, _ = chan_dft(g_nyq, None)
        carry_ref[...] = f_nyq

    scale = n_lat ** -0.5
    u_half = u_ref[half:half + 1, :].astype(F32) * scale
    g_mat = (_dot(ct_ref[...].astype(BF16), e_ref[...])
             + jnp.where(row % 2 == 0, u_half, -u_half))
    h_mat = _dot(st_ref[...].astype(BF16), o_ref[...])
    fd, fm = chan_dft(g_mat, h_mat)
    f_ref[0, 0, 0] = fd.astype(BF16)
    fm_bf = fm.astype(BF16)
    mirrored = jnp.where(row == 0, carry_ref[0:1, :], _dot(jrev_ref[...], fm_bf))
    f_ref[0, 1, 0] = mirrored.astype(BF16)
    carry_ref[0:1, :] = fm_bf[0:1, :].astype(F32)


def _fourier(p, tables, batch, n_lat):
    ct, st, cc, sc, jrev, alt = tables
    ucol = V_END // FOURIER_WIDTH
    tiles = n_lat // 2 // FOU_T
    const2 = lambda b, s: (0, 0)
    return pl.pallas_call(
        functools.partial(_fourier_kernel, n_lat=n_lat),
        grid=(batch, tiles),
        in_specs=[
            pl.BlockSpec((n_lat, FOURIER_WIDTH), lambda b, s: (b, ucol)),
            pl.BlockSpec((FOU_T, n_lat // 2), lambda b, s: (tiles - 1 - s, 0)),
            pl.BlockSpec((FOU_T, n_lat // 2), lambda b, s: (tiles - 1 - s, 0)),
            pl.BlockSpec(cc.shape, const2),
            pl.BlockSpec(sc.shape, const2),
            pl.BlockSpec(jrev.shape, const2),
            pl.BlockSpec(alt.shape, const2),
        ],
        out_specs=pl.BlockSpec((1, 2, 1, FOU_T, FOURIER_WIDTH), lambda b, s: (b, 0, tiles - 1 - s, 0, 0)),
        out_shape=jax.ShapeDtypeStruct((batch, 2, tiles, FOU_T, FOURIER_WIDTH), BF16),
        scratch_shapes=[pltpu.VMEM((n_lat // 2, FOURIER_WIDTH), BF16),
                        pltpu.VMEM((n_lat // 2, FOURIER_WIDTH), BF16),
                        pltpu.VMEM((SUBLANES, FOURIER_WIDTH), F32)],
        compiler_params=_cparams(("arbitrary", "arbitrary")),
        name="fourier",
    )(p, ct, st, cc, sc, jrev, alt)


MRG_TM = FOU_T


def _merge_kernel(o_ref, f_ref, gates_ref, x_ref, mod_ref, wa_ref, wf_ref, wo_ref, x1_ref):
    a = _dot(o_ref[...], wa_ref[...])
    fm = _dot(f_ref[0, 0, 0], wf_ref[...])
    z = gates_ref[:, 0:D_MODEL].astype(F32) * a + gates_ref[:, D_MODEL:].astype(F32) * fm
    y = _dot(z.astype(BF16), wo_ref[...])
    x1_ref[...] = x_ref[...] + mod_ref[0, 2:3, :] * y


def _merge(o, f, p, x2, mod3, wa, wf, wo, n_lat):
    rows = x2.shape[0]
    tiles_per_batch = n_lat // MRG_TM
    const = lambda i: (0, 0)
    single = pl.Buffered(1)

    def f_index(i):
        t = i % tiles_per_batch
        upper = t // (tiles_per_batch // 2)
        return (i // tiles_per_batch, upper, jnp.where(upper == 1, tiles_per_batch - 1 - t, t), 0, 0)

    return pl.pallas_call(
        _merge_kernel,
        grid=(rows // MRG_TM,),
        in_specs=[
            pl.BlockSpec((MRG_TM, ATTN_WIDTH), lambda i: (i, 0)),
            pl.BlockSpec((1, 1, 1, MRG_TM, FOURIER_WIDTH), f_index),
            pl.BlockSpec((MRG_TM, 2 * D_MODEL), lambda i: (i, F_END // (2 * D_MODEL))),
            pl.BlockSpec((MRG_TM, D_MODEL), lambda i: (i, 0)),
            pl.BlockSpec((1, N_MOD, D_MODEL), lambda i: (i // tiles_per_batch, 0, 0)),
            pl.BlockSpec(wa.shape, const, pipeline_mode=single),
            pl.BlockSpec(wf.shape, const, pipeline_mode=single),
            pl.BlockSpec(wo.shape, const, pipeline_mode=single),
        ],
        out_specs=pl.BlockSpec((MRG_TM, D_MODEL), lambda i: (i, 0)),
        out_shape=jax.ShapeDtypeStruct((rows, D_MODEL), F32),
        compiler_params=_cparams(("arbitrary",)),
        name="merge",
    )(o, f, p, x2, mod3, wa, wf, wo)


MLP_TM = 512
MLP_TF = 1024


def _mlp_kernel(x1_ref, mod_ref, n2g_ref, w1_ref, w2_ref, fg_ref, out_ref, h_ref, acc_ref):
    kf = pl.program_id(1)

    @pl.when(kf == 0)
    def _():
        _norm_modulate(x1_ref, n2g_ref[...], mod_ref[0, 4:5, :], mod_ref[0, 3:4, :], h_ref)

    def chunk():
        hid = jnp.maximum(_dot(h_ref[...], w1_ref[0]), 0.0)
        return _dot((hid * hid).astype(BF16), w2_ref[...])

    last = pl.num_programs(1) - 1

    @pl.when(kf == 0)
    def _():
        acc_ref[...] = chunk()

    @pl.when(jnp.logical_and(kf > 0, kf < last))
    def _():
        acc_ref[...] += chunk()

    @pl.when(kf == last)
    def _():
        x2 = x1_ref[...] + mod_ref[0, 5:6, :] * (acc_ref[...] + chunk())
        out_ref[...] = _rmsnorm_rows(x2, fg_ref[...])


def _mlp(x1, mod3, n2g, w1, w2, fg, n_lat):
    rows = x1.shape[0]
    tiles_per_batch = n_lat // MLP_TM
    return pl.pallas_call(
        _mlp_kernel,
        grid=(rows // MLP_TM, D_FF // MLP_TF),
        in_specs=[
            pl.BlockSpec((MLP_TM, D_MODEL), lambda i, kf: (i, 0)),
            pl.BlockSpec((1, N_MOD, D_MODEL), lambda i, kf: (i // tiles_per_batch, 0, 0)),
            pl.BlockSpec((1, D_MODEL), lambda i, kf: (0, 0)),
            pl.BlockSpec((1, D_MODEL, MLP_TF), lambda i, kf: (kf, 0, 0)),
            pl.BlockSpec((MLP_TF, D_MODEL), lambda i, kf: (kf, 0)),
            pl.BlockSpec((1, D_MODEL), lambda i, kf: (0, 0)),
        ],
        out_specs=pl.BlockSpec((MLP_TM, D_MODEL), lambda i, kf: (i, 0)),
        out_shape=jax.ShapeDtypeStruct((rows, D_MODEL), F32),
        scratch_shapes=[pltpu.VMEM((MLP_TM, D_MODEL), BF16), pltpu.VMEM((MLP_TM, D_MODEL), F32)],
        compiler_params=_cparams(("arbitrary", "arbitrary")),
        name="mlp",
    )(x1, mod3, n2g, w1, w2, fg)


def _rope_tables(rows):
    pair = np.arange(HEAD_DIM) // 2
    inv = (np.float32(1.0) / np.float32(ROPE_THETA) ** (
        (pair % ROPE_AXIS_PAIRS).astype(np.float32) / np.float32(ROPE_AXIS_PAIRS))).astype(np.float32)
    pos = np.arange(rows * GRID_W)[:, None]
    coord = np.where(pair[None, :] < ROPE_AXIS_PAIRS, pos // GRID_W, pos % GRID_W).astype(np.float32)
    ang = coord * inv[None, :]
    return jnp.asarray(np.stack([np.cos(ang), np.sin(ang)]).astype(np.float32))


def _dft_tables(size, period, scale):
    idx = np.arange(size, dtype=np.int64)
    ang = ((idx[:, None] * idx[None, :]) % period) * (2.0 * math.pi / period)
    return (jnp.asarray((scale * np.cos(ang)).astype(np.float32)),
            jnp.asarray((scale * np.sin(ang)).astype(np.float32)))


def _fourier_tables(n_lat):
    half = n_lat // 2
    ct, st = _dft_tables(half, n_lat, n_lat ** -0.5)
    cc, sc = _dft_tables(FOURIER_GROUP_DIM, FOURIER_GROUP_DIM, FOURIER_GROUP_DIM ** -0.5)
    jrev = np.zeros((FOU_T, FOU_T), np.float32)
    jrev[np.arange(1, FOU_T), FOU_T - np.arange(1, FOU_T)] = 1.0
    alt = np.zeros((SUBLANES, n_lat), np.float32)
    alt[0] = (1.0 - 2.0 * (np.arange(n_lat) % 2)) * n_lat ** -0.5
    return ct, st, cc, sc, jnp.asarray(jrev, dtype=BF16), jnp.asarray(alt, dtype=BF16)


def kernel(x, c, ctx, c_ctx, w_ada, b_ada, norm1_g, w_in, q_norm_g, k_norm_g, w_attn_o, w_fourier,
           w_out, norm2_g, w1, w2, final_norm_g):
    batch, n_lat, _ = x.shape
    n_ctx = ctx.shape[1]
    assert w_ada.shape[0] == 1, "single-layer block"
    assert n_lat % INP_TM == 0 and n_ctx % SUBLANES == 0

    cc = jnp.concatenate([c, c_ctx[None, :]], axis=0)
    cb = jnp.broadcast_to(cc[:, :, None], (batch + 1, D_MODEL, LANES))
    mod = _adaln(cb, w_ada[0], b_ada)
    mod3 = mod.reshape(SUBLANES, N_MOD, D_MODEL)

    w_in_bf = w_in[0]
    n1g = norm1_g
    heads_per_sub = INP_SUB // HEAD_DIM
    qg, kg = jnp.tile(q_norm_g, (1, heads_per_sub)), jnp.tile(k_norm_g, (1, heads_per_sub))
    rope = _rope_tables(n_lat // GRID_W)

    x2 = x.reshape(batch * n_lat, D_MODEL)
    tiles_per_batch = n_lat // INP_TM
    p = _inproj(x2, mod3, lambda i: i // tiles_per_batch, n1g, w_in_bf, rope, qg, kg,
                tm=INP_TM, group=INP_GROUP, col_block0=0, n_col_blocks=IN_WIDTH // INP_TN, use_rope=True,
                rope_tiles=tiles_per_batch, name="inproj")
    ctx2 = ctx.reshape(batch * n_ctx, D_MODEL)
    pc = _inproj(ctx2, mod3, lambda i: batch, n1g, w_in_bf, rope, qg, kg,
                 tm=batch * n_ctx, group=1, col_block0=Q_END // INP_TN,
                 n_col_blocks=(V_END - Q_END) // INP_TN,
                 use_rope=False, rope_tiles=1, name="ctxproj")

    o, (wa_bf, wf_bf, wo_bf, w1_bf, w2_bf) = _attention(
        p, pc, batch, n_lat, n_ctx,
        ((w_attn_o[0], 1), (w_fourier[0], 1), (w_out[0], 1), (w1[0], D_FF // MLP_TF), (w2[0], 1)))

    f = _fourier(p, _fourier_tables(n_lat), batch, n_lat)

    x1 = _merge(o, f, p, x2, mod3, wa_bf, wf_bf, wo_bf, n_lat)
    out = _mlp(x1, mod3, norm2_g, w1_bf, w2_bf, final_norm_g[None, :], n_lat)
    return out.reshape(batch, n_lat, D_MODEL)
```
